```python
import math
import jax, jax.numpy as jnp
from jax import lax
import numpy as np

D_MODEL = 1024
BATCH = 8
SEQ = 2048
DEPTH = 2

N_A_LAYERS = DEPTH // 2
N_B_LAYERS = DEPTH - N_A_LAYERS
N_DENSE_FFN = (DEPTH + 1) // 2
N_MOE_FFN = DEPTH // 2
RMS_EPS = 1e-6
ROPE_THETA = 500000.0
NEG_INF = -1e30

MLA_HEADS = 16
MLA_NOPE_DIM = 64
MLA_ROPE_DIM = 32
MLA_V_DIM = 64
MLA_Q_LORA = 384
MLA_KV_LORA = 256
MLA_QK_DIM = MLA_NOPE_DIM + MLA_ROPE_DIM
MLA_DOWN_DIM = MLA_Q_LORA + MLA_KV_LORA + MLA_ROPE_DIM
Q_BLOCK = 128

DSW_BRANCHES = ((128, 1), (512, 4), (2048, 16))
N_BR = len(DSW_BRANCHES)
DSW_HEADS = 16
DSW_HEAD_DIM = 64
DSW_ROT_DIM = DSW_HEAD_DIM // 4
DSW_WIDTH = DSW_HEADS * DSW_HEAD_DIM
DSW_BLOCK = 128

FFN_DIM = 2816
N_EXPERTS = 8
TOP_K = 2
EXPERT_DIM = 3584

kernel_name = "yoco_mla_dilated_moe_trunk"


def rmsnorm(x, g):
    x32 = x.astype(jnp.float32)
    y = x32 * lax.rsqrt(jnp.mean(x32 * x32, axis=-1, keepdims=True) + RMS_EPS)
    return (y * g.astype(jnp.float32)).astype(x.dtype)


def rope(t, pos, rot_dim):
    half = rot_dim // 2
    inv_freq = jnp.power(jnp.float32(ROPE_THETA), -jnp.arange(half, dtype=jnp.float32) * (2.0 / rot_dim))
    ang = pos.astype(jnp.float32)[..., None] * inv_freq
    cos = jnp.cos(ang)[:, :, None, :]
    sin = jnp.sin(ang)[:, :, None, :]
    x1 = t[..., :half].astype(jnp.float32)
    x2 = t[..., half:rot_dim].astype(jnp.float32)
    rot = jnp.concatenate([x1 * cos - x2 * sin, x2 * cos + x1 * sin], axis=-1).astype(t.dtype)
    return jnp.concatenate([rot, t[..., rot_dim:]], axis=-1)


def blocked_causal_attention(q, k, v, scale):
    B, S, H, dq = q.shape
    dv = v.shape[-1]
    nb = S // Q_BLOCK
    qb = q.reshape(B, nb, Q_BLOCK, H, dq).transpose(1, 0, 2, 3, 4)
    kpos = jnp.arange(S)

    def one_block(args):
        q_blk, i = args
        s = jnp.einsum('bqhd,bkhd->bhqk', q_blk, k).astype(jnp.float32) * scale
        qpos = i * Q_BLOCK + jnp.arange(Q_BLOCK)
        s = jnp.where(kpos[None, :] <= qpos[:, None], s, NEG_INF)
        p = jax.nn.softmax(s, axis=-1).astype(v.dtype)
        return jnp.einsum('bhqk,bkhd->bqhd', p, v)

    o = lax.map(one_block, (qb, jnp.arange(nb)))
    return o.transpose(1, 0, 2, 3, 4).reshape(B, S, H, dv)


def mla_attention(xn, pos, w_down, q_norm, w_uq, kv_norm, w_ukv, w_o):
    B, S, _ = xn.shape
    down = xn @ w_down
    c_q = rmsnorm(down[..., :MLA_Q_LORA], q_norm)
    c_kv = rmsnorm(down[..., MLA_Q_LORA:MLA_Q_LORA + MLA_KV_LORA], kv_norm)
    k_rope = rope(down[..., None, MLA_Q_LORA + MLA_KV_LORA:], pos, MLA_ROPE_DIM)
    q = (c_q @ w_uq).reshape(B, S, MLA_HEADS, MLA_QK_DIM)
    q = jnp.concatenate([q[..., :MLA_NOPE_DIM], rope(q[..., MLA_NOPE_DIM:], pos, MLA_ROPE_DIM)], axis=-1)
    kv = (c_kv @ w_ukv).reshape(B, S, MLA_HEADS, MLA_NOPE_DIM + MLA_V_DIM)
    k = jnp.concatenate([kv[..., :MLA_NOPE_DIM],
                         jnp.broadcast_to(k_rope, (B, S, MLA_HEADS, MLA_ROPE_DIM))], axis=-1)
    v = kv[..., MLA_NOPE_DIM:]
    o = blocked_causal_attention(q, k, v, 1.0 / math.sqrt(MLA_QK_DIM))
    return o.reshape(B, S, MLA_HEADS * MLA_V_DIM) @ w_o


def shared_dsw_kv(h, pos, kv_norm, w_kv):
    B, S, _ = h.shape
    kv = (rmsnorm(h, kv_norm) @ w_kv).reshape(B, S, 2, N_BR, DSW_HEADS, DSW_HEAD_DIM)
    k = rope(kv[:, :, 0].reshape(B, S, N_BR * DSW_HEADS, DSW_HEAD_DIM), pos, DSW_ROT_DIM)
    return k.reshape(B, S, N_BR, DSW_HEADS, DSW_HEAD_DIM), kv[:, :, 1]


def dilated_branch(q, k, v, window, dilation):
    B, S, H, hd = q.shape
    L = S // dilation
    w_sub = window // dilation
    nb = -(-L // DSW_BLOCK)
    Lp = nb * DSW_BLOCK

    def to_sub(t):
        return t.reshape(B, L, dilation, H, hd).transpose(0, 2, 1, 3, 4)

    qs = jnp.pad(to_sub(q), ((0, 0), (0, 0), (0, Lp - L), (0, 0), (0, 0)))
    pad_kv = ((0, 0), (0, 0), (DSW_BLOCK, Lp - L), (0, 0), (0, 0))
    ks = jnp.pad(to_sub(k), pad_kv).reshape(B, dilation, nb + 1, DSW_BLOCK, H, hd)
    vs = jnp.pad(to_sub(v), pad_kv).reshape(B, dilation, nb + 1, DSW_BLOCK, H, hd)
    k_win = jnp.concatenate([ks[:, :, :-1], ks[:, :, 1:]], axis=3)
    v_win = jnp.concatenate([vs[:, :, :-1], vs[:, :, 1:]], axis=3)
    qb = qs.reshape(B, dilation, nb, DSW_BLOCK, H, hd)

    s = jnp.einsum('brnqhd,brnkhd->brnhqk', qb, k_win).astype(jnp.float32) * (1.0 / math.sqrt(hd))
    a = jnp.arange(DSW_BLOCK)[:, None]
    b = jnp.arange(2 * DSW_BLOCK)[None, :]
    dist = a + DSW_BLOCK - b
    key_sub = jnp.arange(nb)[:, None, None] * DSW_BLOCK - DSW_BLOCK + b[None]
    valid = (dist >= 0)[None] & (dist <= w_sub)[None] & (key_sub >= 0)
    s = jnp.where(valid[None, None, :, None], s, NEG_INF)
    lse = jax.nn.logsumexp(s, axis=-1)
    p = jnp.exp(s - lse[..., None]).astype(v.dtype)
    o = jnp.einsum('brnhqk,brnkhd->brnqhd', p, v_win)

    def from_sub(t):
        t = t.reshape(B, dilation, Lp, H, t.shape[-1])[:, :, :L]
        return t.transpose(0, 2, 1, 3, 4).reshape(B, S, H, t.shape[-1])

    o = from_sub(o)
    lse = from_sub(lse.transpose(0, 1, 2, 4, 3)[..., None])[..., 0]
    return o, lse


def dsw_attention(xn, pos, k_sh, v_sh, w_q, w_o):
    B, S, _ = xn.shape
    q = (xn @ w_q).reshape(B, S, N_BR * DSW_HEADS, DSW_HEAD_DIM)
    q = rope(q, pos, DSW_ROT_DIM).reshape(B, S, N_BR, DSW_HEADS, DSW_HEAD_DIM)
    outs, lses = [], []
    for g, (window, dilation) in enumerate(DSW_BRANCHES):
        o, lse = dilated_branch(q[:, :, g], k_sh[:, :, g], v_sh[:, :, g], window, dilation)
        outs.append(o)
        lses.append(lse)
    wts = jax.nn.softmax(jnp.stack(lses, axis=0), axis=0)
    o = jnp.sum(wts[..., None].astype(xn.dtype) * jnp.stack(outs, axis=0), axis=0)
    return o.reshape(B, S, DSW_WIDTH) @ w_o


def swiglu(xn, wg, wu, wd):
    return (jax.nn.silu(xn @ wg) * (xn @ wu)) @ wd


def moe_swiglu(xn, router, wg, wu, wd):
    logits = (xn @ router).astype(jnp.float32)
    top_vals, top_idx = lax.top_k(logits, TOP_K)
    gates = jax.nn.softmax(top_vals, axis=-1)
    combine = jnp.sum(jax.nn.one_hot(top_idx, N_EXPERTS, dtype=jnp.float32) * gates[..., None], axis=-2)
    out = jnp.zeros_like(xn)
    for e in range(N_EXPERTS):
        out = out + combine[..., e:e + 1].astype(xn.dtype) * swiglu(xn, wg[e], wu[e], wd[e])
    return out


def setup_inputs(seed: int = 0) -> dict:
    key = jax.random.key(seed)
    ks = iter(jax.random.split(key, 32))

    def w(shape, fan_in):
        return jax.random.normal(next(ks), shape, jnp.float32) * (fan_in ** -0.5)

    def gain(shape):
        return 1.0 + 0.05 * jax.random.normal(next(ks), shape, jnp.float32)

    x = jax.random.normal(next(ks), (BATCH, SEQ, D_MODEL), jnp.float32)
    offs = jax.random.randint(next(ks), (BATCH, 1), 0, 8192, dtype=jnp.int32)
    positions = offs + jnp.arange(SEQ, dtype=jnp.int32)[None, :]
    return {
        "x": x,
        "positions": positions,
        "norm_attn": gain((DEPTH, D_MODEL)),
        "norm_ffn": gain((DEPTH, D_MODEL)),
        "mla_w_down": w((N_A_LAYERS, D_MODEL, MLA_DOWN_DIM), D_MODEL),
        "mla_q_norm": gain((N_A_LAYERS, MLA_Q_LORA)),
        "mla_w_uq": w((N_A_LAYERS, MLA_Q_LORA, MLA_HEADS * MLA_QK_DIM), MLA_Q_LORA),
        "mla_kv_norm": gain((N_A_LAYERS, MLA_KV_LORA)),
        "mla_w_ukv": w((N_A_LAYERS, MLA_KV_LORA, MLA_HEADS * (MLA_NOPE_DIM + MLA_V_DIM)), MLA_KV_LORA),
        "mla_w_o": w((N_A_LAYERS, MLA_HEADS * MLA_V_DIM, D_MODEL), MLA_HEADS * MLA_V_DIM),
        "dsw_kv_norm": gain((D_MODEL,)),
        "dsw_w_kv": w((D_MODEL, 2 * N_BR * DSW_WIDTH), D_MODEL),
        "dsw_w_q": w((N_B_LAYERS, D_MODEL, N_BR * DSW_WIDTH), D_MODEL),
        "dsw_w_o": w((N_B_LAYERS, DSW_WIDTH, D_MODEL), DSW_WIDTH),
        "ffn_w_gate": w((N_DENSE_FFN, D_MODEL, FFN_DIM), D_MODEL),
        "ffn_w_up": w((N_DENSE_FFN, D_MODEL, FFN_DIM), D_MODEL),
        "ffn_w_down": w((N_DENSE_FFN, FFN_DIM, D_MODEL), FFN_DIM),
        "moe_router": w((N_MOE_FFN, D_MODEL, N_EXPERTS), D_MODEL),
        "moe_w_gate": w((N_MOE_FFN, N_EXPERTS, D_MODEL, EXPERT_DIM), D_MODEL),
        "moe_w_up": w((N_MOE_FFN, N_EXPERTS, D_MODEL, EXPERT_DIM), D_MODEL),
        "moe_w_down": w((N_MOE_FFN, N_EXPERTS, EXPERT_DIM, D_MODEL), EXPERT_DIM),
        "final_norm": gain((D_MODEL,)),
    }


def reference(x, positions, norm_attn, norm_ffn, mla_w_down, mla_q_norm, mla_w_uq, mla_kv_norm,
              mla_w_ukv, mla_w_o, dsw_kv_norm, dsw_w_kv, dsw_w_q, dsw_w_o, ffn_w_gate, ffn_w_up,
              ffn_w_down, moe_router, moe_w_gate, moe_w_up, moe_w_down, final_norm):
    h = x
    k_sh = v_sh = None
    for layer in range(DEPTH):
        xn = rmsnorm(h, norm_attn[layer])
        if layer < N_A_LAYERS:
            i = layer
            h = h + mla_attention(xn, positions, mla_w_down[i], mla_q_norm[i], mla_w_uq[i],
                                  mla_kv_norm[i], mla_w_ukv[i], mla_w_o[i])
        else:
            if layer == N_A_LAYERS:
                k_sh, v_sh = shared_dsw_kv(h, positions, dsw_kv_norm, dsw_w_kv)
                xn = rmsnorm(h, norm_attn[layer])
            i = layer - N_A_LAYERS
            h = h + dsw_attention(xn, positions, k_sh, v_sh, dsw_w_q[i], dsw_w_o[i])
        hn = rmsnorm(h, norm_ffn[layer])
        j = layer // 2
        if layer % 2 == 0:
            h = h + swiglu(hn, ffn_w_gate[j], ffn_w_up[j], ffn_w_down[j])
        else:
            h = h + moe_swiglu(hn, moe_router[j], moe_w_gate[j], moe_w_up[j], moe_w_down[j])
    return rmsnorm(h, final_norm)
```

```python
import functools
import math

import numpy as np
import jax
import jax.numpy as jnp
from jax import lax
from jax.experimental import pallas as pl
from jax.experimental.pallas import tpu as pltpu

F32 = jnp.float32
BF16 = jnp.bfloat16

D_MODEL = 1024
RMS_EPS = 1e-6
ROPE_THETA = 500000.0
NEG_INF = -1e30
LANES = 128

MLA_HEADS = 16
MLA_NOPE = 64
MLA_ROPE = 32
MLA_V = 64
MLA_QK = MLA_NOPE + MLA_ROPE
MLA_Q_LORA = 384
MLA_KV_LORA = 256

DSW_BRANCHES = ((128, 1), (512, 4), (2048, 16))
N_BR = 3
DSW_HEADS = 16
DSW_HD = 64
DSW_ROT = 16
DSW_WIDTH = DSW_HEADS * DSW_HD
DSW_BLOCK = 128

FFN_DIM = 2816
N_EXPERTS = 8
EXPERT_DIM = 3584

VMEM_LIMIT = 56 * 1024 * 1024


def _cparams(sem):
    return pltpu.CompilerParams(dimension_semantics=sem, vmem_limit_bytes=VMEM_LIMIT)


def _resident(shape):
    nd = len(shape)
    return pl.BlockSpec(shape, lambda *_: (0,) * nd, pipeline_mode=pl.Buffered(1))


def _rms(x, g):
    return x * lax.rsqrt(jnp.mean(x * x, axis=-1, keepdims=True) + RMS_EPS) * g


def _dot(a, b):
    return jnp.dot(a, b, preferred_element_type=F32)


def _dot_nt(a, b):
    return lax.dot_general(a, b, (((1,), (1,)), ((), ())), preferred_element_type=F32)


def _rope(t, cos, sin_signed):
    return t * cos + pltpu.roll(t, 64, 1) * sin_signed


def _mla_proj_kernel(x_ref, pos_ref, g_ref, wdn_ref, qn_ref, kvn_ref, wuq_ref, wk_ref, wv_ref,
                     freq_ref, sign_ref, q_ref, k_ref, v_ref):
    xn = _rms(x_ref[...], g_ref[...]).astype(BF16)
    down = _dot(xn, wdn_ref[...])
    cq = _rms(down[:, :MLA_Q_LORA], qn_ref[...]).astype(BF16)
    ckv = _rms(down[:, MLA_Q_LORA:MLA_Q_LORA + MLA_KV_LORA], kvn_ref[...]).astype(BF16)
    kr = down[:, MLA_Q_LORA + MLA_KV_LORA:]
    ang = pos_ref[...].astype(F32) * freq_ref[...]
    cos = jnp.cos(ang)
    sin = jnp.sin(ang) * sign_ref[...]
    kr = _rope(kr, cos, sin)
    scale = 1.0 / math.sqrt(MLA_QK)
    cos_q = cos * scale
    sin_q = sin * scale
    q = _dot(cq, wuq_ref[...])
    for h in range(MLA_HEADS):
        q_ref[h] = _rope(q[:, h * LANES:(h + 1) * LANES], cos_q, sin_q).astype(BF16)
    kp = _dot(ckv, wk_ref[...])
    for h in range(MLA_HEADS):
        k_ref[h] = (kp[:, h * LANES:(h + 1) * LANES] + kr).astype(BF16)
    v = _dot(ckv, wv_ref[...])
    for j in range(MLA_HEADS // 2):
        v_ref[j] = v[:, j * LANES:(j + 1) * LANES].astype(BF16)


def _mla_proj(x, pos, g, wdn, qn, kvn, wuq, wk, wv, freq, sign, tm=512):
    B, S, _ = x.shape
    H = MLA_HEADS
    return pl.pallas_call(
        _mla_proj_kernel,
        grid=(B, S // tm),
        in_specs=[
            pl.BlockSpec((None, tm, D_MODEL), lambda b, i: (b, i, 0)),
            pl.BlockSpec((None, tm, 1), lambda b, i: (b, i, 0)),
            _resident(g.shape), _resident(wdn.shape), _resident(qn.shape), _resident(kvn.shape),
            _resident(wuq.shape), _resident(wk.shape), _resident(wv.shape),
            _resident(freq.shape), _resident(sign.shape),
        ],
        out_specs=[
            pl.BlockSpec((None, H, tm, LANES), lambda b, i: (b, 0, i, 0)),
            pl.BlockSpec((None, H, tm, LANES), lambda b, i: (b, 0, i, 0)),
            pl.BlockSpec((None, H // 2, tm, LANES), lambda b, i: (b, 0, i, 0)),
        ],
        out_shape=[
            jax.ShapeDtypeStruct((B, H, S, LANES), BF16),
            jax.ShapeDtypeStruct((B, H, S, LANES), BF16),
            jax.ShapeDtypeStruct((B, H // 2, S, LANES), BF16),
        ],
        compiler_params=_cparams(("parallel", "parallel")),
        name="mla_proj",
    )(x, pos, g, wdn, qn, kvn, wuq, wk, wv, freq, sign)


def _mla_attn_kernel(q_ref, k_ref, v_ref, o_ref, *, t):
    i = pl.program_id(2)
    row = lax.broadcasted_iota(jnp.int32, (t, t), 0)
    col = lax.broadcasted_iota(jnp.int32, (t, t), 1)
    causal = col <= row
    outs = []
    for hh in range(2):
        q = q_ref[hh]

        def step(s, v, carry):
            m, l, acc = carry
            m_new = jnp.maximum(m, jnp.max(s, axis=-1, keepdims=True))
            p = jnp.exp(s - m_new)
            alpha = jnp.exp(m - m_new)
            l = alpha * l + jnp.sum(p, axis=-1, keepdims=True)
            acc = alpha * acc + _dot(p.astype(BF16), v)
            return m_new, l, acc

        def body(c, carry):
            k = k_ref[hh, pl.ds(pl.multiple_of(c * t, t), t), :]
            v = v_ref[pl.ds(pl.multiple_of(c * t, t), t), :]
            return step(_dot_nt(q, k), v, carry)

        init = (jnp.full((t, 1), NEG_INF, F32), jnp.zeros((t, 1), F32), jnp.zeros((t, LANES), F32))
        carry = lax.fori_loop(0, i, body, init)
        k = k_ref[hh, pl.ds(pl.multiple_of(i * t, t), t), :]
        v = v_ref[pl.ds(pl.multiple_of(i * t, t), t), :]
        s = jnp.where(causal, _dot_nt(q, k), NEG_INF)
        _, l, acc = step(s, v, carry)
        outs.append(acc / l)
    lane = lax.broadcasted_iota(jnp.int32, (t, LANES), 1)
    o_ref[...] = jnp.where(lane < MLA_V, outs[0], outs[1]).astype(BF16)


def _mla_attn(q, k, v, t=256):
    B, H, S, _ = q.shape
    return pl.pallas_call(
        functools.partial(_mla_attn_kernel, t=t),
        grid=(B, H // 2, S // t),
        in_specs=[
            pl.BlockSpec((None, 2, t, LANES), lambda b, j, i: (b, j, i, 0)),
            pl.BlockSpec((None, 2, S, LANES), lambda b, j, i: (b, j, 0, 0)),
            pl.BlockSpec((None, None, S, LANES), lambda b, j, i: (b, j, 0, 0)),
        ],
        out_specs=pl.BlockSpec((None, t, LANES), lambda b, j, i: (b, i, j)),
        out_shape=jax.ShapeDtypeStruct((B, S, (H // 2) * LANES), BF16),
        compiler_params=_cparams(("parallel", "parallel", "parallel")),
        name="mla_attn",
    )(q, k, v)


def _swiglu_acc(hn, wg_ref, wu_ref, wd_ref, chunk, scale=None):
    acc = None
    n = wg_ref.shape[-1]
    for c0 in range(0, n, chunk):
        g = _dot(hn, wg_ref[:, c0:c0 + chunk])
        u = _dot(hn, wu_ref[:, c0:c0 + chunk])
        a = g * jax.nn.sigmoid(g) * u
        if scale is not None:
            a = a * scale
        part = _dot(a.astype(BF16), wd_ref[c0:c0 + chunk, :])
        acc = part if acc is None else acc + part
    return acc


def _attn_ffn_kernel(x_ref, o_ref, wo_ref, g_ref, wg_ref, wu_ref, wd_ref, h_ref):
    h1 = x_ref[...] + _dot(o_ref[...], wo_ref[...])
    hn = _rms(h1, g_ref[...]).astype(BF16)
    h_ref[...] = h1 + _swiglu_acc(hn, wg_ref, wu_ref, wd_ref, FFN_DIM // 2)


def _attn_ffn(x, o, wo, g, wg, wu, wd, tm=512):
    T = x.shape[0]
    return pl.pallas_call(
        _attn_ffn_kernel,
        grid=(T // tm,),
        in_specs=[
            pl.BlockSpec((tm, D_MODEL), lambda i: (i, 0)),
            pl.BlockSpec((tm, D_MODEL), lambda i: (i, 0)),
            _resident(wo.shape), _resident(g.shape), _resident(wg.shape), _resident(wu.shape),
            _resident(wd.shape),
        ],
        out_specs=pl.BlockSpec((tm, D_MODEL), lambda i: (i, 0)),
        out_shape=jax.ShapeDtypeStruct((T, D_MODEL), F32),
        compiler_params=_cparams(("parallel",)),
        name="attn_ffn",
    )(x, o, wo, g, wg, wu, wd)


def _dsw_proj_kernel(h_ref, pos_ref, gq_ref, gkv_ref, wq_ref, wk_ref, wv_ref, freq_ref, sign_ref,
                     q1, k1, v1, q2, k2, v2, q3, k3, v3,
                     xq_scr, xkv_scr, cos_scr, sin_scr, pq_scr, pkv_scr, pcos_scr, psin_scr, *, tm):
    h = h_ref[...]
    r = lax.rsqrt(jnp.mean(h * h, axis=-1, keepdims=True) + RMS_EPS)
    hr = h * r
    nblk = D_MODEL // LANES
    xq_f = hr * gq_ref[...]
    xkv_f = hr * gkv_ref[...]
    for cb in range(nblk):
        xq_scr[cb] = xq_f[:, cb * LANES:(cb + 1) * LANES]
        xkv_scr[cb] = xkv_f[:, cb * LANES:(cb + 1) * LANES]
    ang = pos_ref[...].astype(F32) * freq_ref[...]
    cos_scr[...] = jnp.cos(ang)
    sin_scr[...] = jnp.sin(ang) * sign_ref[...]
    scale = 1.0 / math.sqrt(DSW_HD)
    outs = ((q1, k1, v1), (q2, k2, v2), (q3, k3, v3))
    for g, (_, d) in enumerate(DSW_BRANCHES):
        n = tm // d
        q_out, k_out, v_out = outs[g]
        if d == 1:
            xq = xq_f.astype(BF16)
            xkv = xkv_f.astype(BF16)
            cos = cos_scr[...]
            sin = sin_scr[...]
        else:
            for rr in range(d):
                rows = slice(rr * n, (rr + 1) * n)
                for cb in range(nblk):
                    cols = slice(cb * LANES, (cb + 1) * LANES)
                    pq_scr[rows, cols] = xq_scr[cb, pl.ds(rr, n, stride=d), :].astype(BF16)
                    pkv_scr[rows, cols] = xkv_scr[cb, pl.ds(rr, n, stride=d), :].astype(BF16)
                pcos_scr[rows, :] = cos_scr[pl.ds(rr, n, stride=d), :]
                psin_scr[rows, :] = sin_scr[pl.ds(rr, n, stride=d), :]
            xq = pq_scr[...]
            xkv = pkv_scr[...]
            cos = pcos_scr[...]
            sin = psin_scr[...]
        q = _dot(xq, wq_ref[g])
        k = _dot(xkv, wk_ref[g])
        v = _dot(xkv, wv_ref[g])
        cos_q = cos * scale
        sin_q = sin * scale
        for j in range(DSW_HEADS // 2):
            sl = slice(j * LANES, (j + 1) * LANES)
            qj = _rope(q[:, sl], cos_q, sin_q).astype(BF16)
            kj = _rope(k[:, sl], cos, sin).astype(BF16)
            vj = v[:, sl].astype(BF16)
            for rr in range(d):
                q_out[j, rr] = qj[rr * n:(rr + 1) * n]
                k_out[j, rr] = kj[rr * n:(rr + 1) * n]
                v_out[j, rr] = vj[rr * n:(rr + 1) * n]


def _dsw_proj(h, pos, gq, gkv, wq, wk, wv, freq, sign, tm=256):
    B, S, _ = h.shape
    P = DSW_HEADS // 2
    out_specs, out_shape = [], []
    for _, d in DSW_BRANCHES:
        for _ in range(3):
            out_specs.append(pl.BlockSpec((None, P, d, tm // d, LANES), lambda b, i: (b, 0, 0, i, 0)))
            out_shape.append(jax.ShapeDtypeStruct((B, P, d, S // d, LANES), BF16))
    return pl.pallas_call(
        functools.partial(_dsw_proj_kernel, tm=tm),
        grid=(B, S // tm),
        in_specs=[
            pl.BlockSpec((None, tm, D_MODEL), lambda b, i: (b, i, 0)),
            pl.BlockSpec((None, tm, 1), lambda b, i: (b, i, 0)),
            _resident(gq.shape), _resident(gkv.shape), _resident(wq.shape), _resident(wk.shape),
            _resident(wv.shape), _resident(freq.shape), _resident(sign.shape),
        ],
        out_specs=out_specs,
        out_shape=out_shape,
        scratch_shapes=[
            pltpu.VMEM((D_MODEL // LANES, tm, LANES), F32), pltpu.VMEM((D_MODEL // LANES, tm, LANES), F32),
            pltpu.VMEM((tm, LANES), F32), pltpu.VMEM((tm, LANES), F32),
            pltpu.VMEM((tm, D_MODEL), BF16), pltpu.VMEM((tm, D_MODEL), BF16),
            pltpu.VMEM((tm, LANES), F32), pltpu.VMEM((tm, LANES), F32),
        ],
        compiler_params=_cparams(("parallel", "parallel")),
        name="dsw_proj",
    )(h, pos, gq, gkv, wq, wk, wv, freq, sign)


def _dsw_attn_kernel(q1, k1, v1, q2, k2, v2, q3, k3, v3, o_ref,
                     o1_scr, l1_scr, o2_scr, l2_scr, o3_scr, l3_scr, bo_scr, bl_scr, *, S):
    blk = DSW_BLOCK
    lane = lax.broadcasted_iota(jnp.int32, (blk, LANES), 1)
    mask_a = (lane < 8) | ((lane >= 16) & (lane < 72))
    first_half = lane < DSW_HD
    row1 = lax.broadcasted_iota(jnp.int32, (blk, blk), 0)
    col1 = lax.broadcasted_iota(jnp.int32, (blk, blk), 1)
    valid_first = col1 <= row1
    row2 = lax.broadcasted_iota(jnp.int32, (blk, 2 * blk), 0)
    col2 = lax.broadcasted_iota(jnp.int32, (blk, 2 * blk), 1)
    valid_next = ((col2 < blk) & (col2 >= row2)) | ((col2 >= blk) & (col2 - blk <= row2))

    def attend(qb, kk, vv, valid):
        os_, ls_ = [], []
        qf = qb.astype(F32)
        for msk in (mask_a, jnp.logical_not(mask_a)):
            qm = jnp.where(msk, qf, 0.0).astype(BF16)
            s = jnp.where(valid, _dot_nt(qm, kk), NEG_INF)
            m = jnp.max(s, axis=-1, keepdims=True)
            p = jnp.exp(s - m)
            l = jnp.sum(p, axis=-1, keepdims=True)
            os_.append(_dot(p.astype(BF16), vv) / l)
            ls_.append(m + jnp.log(l))
        return (jnp.where(first_half, os_[0], os_[1]),
                jnp.where(first_half, ls_[0], ls_[1]))

    def first_block(q_ref, k_ref, v_ref, off, o_dst, l_dst):
        sl = pl.ds(pl.multiple_of(off, blk), blk)
        o, l = attend(q_ref[sl, :], k_ref[sl, :], v_ref[sl, :], valid_first)
        o_dst[sl, :] = o
        l_dst[sl, :] = l

    def next_block(q_ref, k_ref, v_ref, off, o_dst, l_dst):
        sl = pl.ds(pl.multiple_of(off, blk), blk)
        win = pl.ds(pl.multiple_of(off - blk, blk), 2 * blk)
        o, l = attend(q_ref[sl, :], k_ref[win, :], v_ref[win, :], valid_next)
        o_dst[sl, :] = o
        l_dst[sl, :] = l

    branches = ((q1, k1, v1, o1_scr, l1_scr, o1_scr, l1_scr),
                (q2, k2, v2, bo_scr, bl_scr, o2_scr, l2_scr),
                (q3, k3, v3, bo_scr, bl_scr, o3_scr, l3_scr))
    for (_, d), (q_ref, k_ref, v_ref, o_dst, l_dst, o_tok, l_tok) in zip(DSW_BRANCHES, branches):
        L = S // d
        nb = L // blk

        def first_body(rr, c, q_ref=q_ref, k_ref=k_ref, v_ref=v_ref, o_dst=o_dst, l_dst=l_dst, L=L):
            first_block(q_ref, k_ref, v_ref, rr * L, o_dst, l_dst)
            return c

        lax.fori_loop(0, d, first_body, 0)
        if nb > 1:
            def next_body(idx, c, q_ref=q_ref, k_ref=k_ref, v_ref=v_ref, o_dst=o_dst, l_dst=l_dst,
                          L=L, nb=nb):
                rr = idx // (nb - 1)
                ii = idx - rr * (nb - 1) + 1
                next_block(q_ref, k_ref, v_ref, rr * L + ii * blk, o_dst, l_dst)
                return c

            lax.fori_loop(0, d * (nb - 1), next_body, 0)
        if d > 1:
            for rr in range(d):
                o_tok[pl.ds(rr, L, stride=d), :] = o_dst[rr * L:(rr + 1) * L, :]
                l_tok[pl.ds(rr, L, stride=d), :] = l_dst[rr * L:(rr + 1) * L, :]

    ch = 256
    for c0 in range(0, S, ch):
        sl = slice(c0, c0 + ch)
        la, lb, lc = l1_scr[sl, :], l2_scr[sl, :], l3_scr[sl, :]
        m = jnp.maximum(jnp.maximum(la, lb), lc)
        wa, wb, wc = jnp.exp(la - m), jnp.exp(lb - m), jnp.exp(lc - m)
        num = wa * o1_scr[sl, :] + wb * o2_scr[sl, :] + wc * o3_scr[sl, :]
        o_ref[sl, :] = (num / (wa + wb + wc)).astype(BF16)


def _dsw_attn(qkv):
    B, P = qkv[0].shape[0], qkv[0].shape[1]
    S = qkv[0].shape[2] * qkv[0].shape[3]
    flat = [a.reshape(B, P, S, LANES) for a in qkv]
    spec = pl.BlockSpec((None, None, S, LANES), lambda b, j: (b, j, 0, 0))
    return pl.pallas_call(
        functools.partial(_dsw_attn_kernel, S=S),
        grid=(B, P),
        in_specs=[spec] * 9,
        out_specs=pl.BlockSpec((None, S, LANES), lambda b, j: (b, 0, j)),
        out_shape=jax.ShapeDtypeStruct((B, S, P * LANES), BF16),
        scratch_shapes=[pltpu.VMEM((S, LANES), F32)] * 8,
        compiler_params=_cparams(("parallel", "parallel")),
        name="dsw_attn",
    )(*flat)


def _dsw_router_kernel(h_ref, o_ref, wo_ref, g_ref, wr_ref, h3_ref, hn_ref, comb_ref):
    h3 = h_ref[...] + _dot(o_ref[...], wo_ref[...])
    h3_ref[...] = h3
    hn = _rms(h3, g_ref[...])
    hn_ref[...] = hn.astype(BF16)
    logits = jnp.dot(hn, wr_ref[...], preferred_element_type=F32, precision=lax.Precision.HIGHEST)
    lane = lax.broadcasted_iota(jnp.int32, logits.shape, 1).astype(F32)
    logits = jnp.where(lane < N_EXPERTS, logits, -jnp.inf)
    v1 = jnp.max(logits, axis=-1, keepdims=True)
    i1 = jnp.min(jnp.where(logits == v1, lane, float(LANES)), axis=-1, keepdims=True)
    rest = jnp.where(lane == i1, -jnp.inf, logits)
    v2 = jnp.max(rest, axis=-1, keepdims=True)
    i2 = jnp.min(jnp.where(rest == v2, lane, float(LANES)), axis=-1, keepdims=True)
    e2 = jnp.exp(v2 - v1)
    den = 1.0 + e2
    comb_ref[...] = jnp.where(lane == i1, 1.0 / den, jnp.where(lane == i2, e2 / den, 0.0))


def _dsw_router(h, o, wo, g, wr, tm=512):
    T = h.shape[0]
    return pl.pallas_call(
        _dsw_router_kernel,
        grid=(T // tm,),
        in_specs=[
            pl.BlockSpec((tm, D_MODEL), lambda i: (i, 0)),
            pl.BlockSpec((tm, D_MODEL), lambda i: (i, 0)),
            _resident(wo.shape), _resident(g.shape), _resident(wr.shape),
        ],
        out_specs=[
            pl.BlockSpec((tm, D_MODEL), lambda i: (i, 0)),
            pl.BlockSpec((tm, D_MODEL), lambda i: (i, 0)),
            pl.BlockSpec((tm, LANES), lambda i: (i, 0)),
        ],
        out_shape=[
            jax.ShapeDtypeStruct((T, D_MODEL), F32),
            jax.ShapeDtypeStruct((T, D_MODEL), BF16),
            jax.ShapeDtypeStruct((T, LANES), F32),
        ],
        compiler_params=_cparams(("parallel",)),
        name="dsw_router",
    )(h, o, wo, g, wr)


def _moe_kernel(hn_ref, comb_ref, h3_ref, gf_ref, wg_ref, wu_ref, wd_ref, out_ref, acc_ref):
    e = pl.program_id(1)
    f = pl.program_id(2)

    @pl.when((e == 0) & (f == 0))
    def _():
        acc_ref[...] = jnp.zeros_like(acc_ref)

    comb = comb_ref[...]
    lane = lax.broadcasted_iota(jnp.int32, comb.shape, 1)
    gate = jnp.sum(jnp.where(lane == e, comb, 0.0), axis=-1, keepdims=True)
    acc_ref[...] += _swiglu_acc(hn_ref[...], wg_ref, wu_ref, wd_ref, wg_ref.shape[-1], scale=gate)

    @pl.when((e == pl.num_programs(1) - 1) & (f == pl.num_programs(2) - 1))
    def _():
        out_ref[...] = _rms(h3_ref[...] + acc_ref[...], gf_ref[...])


def _moe(hn, comb, h3, gf, wg, wu, wd, tm=512, tf=512):
    T = hn.shape[0]
    E, _, Fe = wg.shape
    return pl.pallas_call(
        _moe_kernel,
        grid=(T // tm, E, Fe // tf),
        in_specs=[
            pl.BlockSpec((tm, D_MODEL), lambda i, e, f: (i, 0)),
            pl.BlockSpec((tm, LANES), lambda i, e, f: (i, 0)),
            pl.BlockSpec((tm, D_MODEL), lambda i, e, f: (i, 0)),
            pl.BlockSpec((1, D_MODEL), lambda i, e, f: (0, 0)),
            pl.BlockSpec((None, D_MODEL, tf), lambda i, e, f: (e, 0, f)),
            pl.BlockSpec((None, D_MODEL, tf), lambda i, e, f: (e, 0, f)),
            pl.BlockSpec((None, tf, D_MODEL), lambda i, e, f: (e, f, 0)),
        ],
        out_specs=pl.BlockSpec((tm, D_MODEL), lambda i, e, f: (i, 0)),
        out_shape=jax.ShapeDtypeStruct((T, D_MODEL), F32),
        scratch_shapes=[pltpu.VMEM((tm, D_MODEL), F32)],
        compiler_params=_cparams(("parallel", "arbitrary", "arbitrary")),
        name="moe",
    )(hn, comb, h3, gf, wg, wu, wd)


def _take_cols(w, src):
    src = np.asarray(src)
    cols = jnp.take(w, jnp.asarray(np.maximum(src, 0)), axis=1)
    return jnp.where(jnp.asarray(src >= 0)[None, :], cols, 0.0)


def _mla_lane_of_dim():
    lane = np.zeros(MLA_QK, np.int64)
    for dd in range(MLA_NOPE):
        lane[dd] = 16 + dd if dd < 48 else 80 + (dd - 48)
    half = MLA_ROPE // 2
    for i in range(MLA_ROPE):
        lane[MLA_NOPE + i] = i if i < half else 64 + (i - half)
    return lane


def _dsw_lane_of(hh, dd):
    half = DSW_ROT // 2
    if dd < half:
        return hh * half + dd
    if dd < DSW_ROT:
        return 64 + hh * half + (dd - half)
    return (16 if hh == 0 else 80) + (dd - DSW_ROT)


def _rope_lane_tables(half, per_lane_index):
    inv_freq = jnp.power(jnp.float32(ROPE_THETA), -jnp.arange(half, dtype=F32) * (2.0 / (2 * half)))
    idx = np.zeros(LANES, np.int64)
    on = np.zeros(LANES, bool)
    sign = np.zeros(LANES, np.float32)
    for l in range(LANES):
        i = per_lane_index(l)
        if i is not None:
            idx[l], on[l] = i, True
            sign[l] = -1.0 if l < 64 else 1.0
    freq = jnp.where(jnp.asarray(on), jnp.take(inv_freq, jnp.asarray(idx)), 0.0)
    return freq.reshape(1, LANES), jnp.asarray(sign).reshape(1, LANES)


def kernel(x, positions, norm_attn, norm_ffn, mla_w_down, mla_q_norm, mla_w_uq, mla_kv_norm, mla_w_ukv,
           mla_w_o, dsw_kv_norm, dsw_w_kv, dsw_w_q, dsw_w_o, ffn_w_gate, ffn_w_up, ffn_w_down, moe_router,
           moe_w_gate, moe_w_up, moe_w_down, final_norm):
    B, S, D = x.shape
    T = B * S
    pos = positions.reshape(B, S, 1)

    lane_of = _mla_lane_of_dim()
    H = MLA_HEADS
    src_dn = np.full(MLA_Q_LORA + MLA_KV_LORA + LANES, -1, np.int64)
    src_dn[:MLA_Q_LORA + MLA_KV_LORA] = np.arange(MLA_Q_LORA + MLA_KV_LORA)
    src_uq = np.full(H * LANES, -1, np.int64)
    src_k = np.full(H * LANES, -1, np.int64)
    src_v = np.zeros(H * MLA_V, np.int64)
    for i in range(MLA_ROPE):
        src_dn[MLA_Q_LORA + MLA_KV_LORA + lane_of[MLA_NOPE + i]] = MLA_Q_LORA + MLA_KV_LORA + i
    for h in range(H):
        for dd in range(MLA_QK):
            src_uq[h * LANES + lane_of[dd]] = h * MLA_QK + dd
        for dd in range(MLA_NOPE):
            src_k[h * LANES + lane_of[dd]] = h * (MLA_NOPE + MLA_V) + dd
        for dd in range(MLA_V):
            src_v[h * MLA_V + dd] = h * (MLA_NOPE + MLA_V) + MLA_NOPE + dd
    wdn = _take_cols(mla_w_down[0], src_dn).astype(BF16)
    wuq = _take_cols(mla_w_uq[0], src_uq).astype(BF16)
    wk = _take_cols(mla_w_ukv[0], src_k).astype(BF16)
    wv = _take_cols(mla_w_ukv[0], src_v).astype(BF16)
    half = MLA_ROPE // 2
    mla_freq, mla_sign = _rope_lane_tables(
        half, lambda l: l if l < half else (l - 64 if 64 <= l < 64 + half else None))

    q, k, v = _mla_proj(x, pos, norm_attn[0:1], wdn, mla_q_norm[0:1], mla_kv_norm[0:1], wuq, wk, wv,
                        mla_freq, mla_sign)
    o = _mla_attn(q, k, v)
    h = _attn_ffn(x.reshape(T, D), o.reshape(T, D), mla_w_o[0].astype(BF16), norm_ffn[0:1],
                  ffn_w_gate[0].astype(BF16), ffn_w_up[0].astype(BF16), ffn_w_down[0].astype(BF16))

    src_pair = np.zeros(DSW_WIDTH, np.int64)
    for j in range(DSW_HEADS // 2):
        for hh in range(2):
            for dd in range(DSW_HD):
                src_pair[j * LANES + _dsw_lane_of(hh, dd)] = (2 * j + hh) * DSW_HD + dd
    wq_b, wk_b, wv_b = [], [], []
    for g in range(N_BR):
        wq_b.append(_take_cols(dsw_w_q[0], g * DSW_WIDTH + src_pair))
        wk_b.append(_take_cols(dsw_w_kv, g * DSW_WIDTH + src_pair))
        wv_b.append(dsw_w_kv[:, (N_BR + g) * DSW_WIDTH:(N_BR + g + 1) * DSW_WIDTH])
    wq_b = jnp.stack(wq_b).astype(BF16)
    wk_b = jnp.stack(wk_b).astype(BF16)
    wv_b = jnp.stack(wv_b).astype(BF16)
    half = DSW_ROT // 2
    dsw_freq, dsw_sign = _rope_lane_tables(
        half, lambda l: l % half if l < 2 * half else ((l - 64) % half if 64 <= l < 64 + 2 * half else None))

    qkv = _dsw_proj(h.reshape(B, S, D), pos, norm_attn[1:2], dsw_kv_norm.reshape(1, D), wq_b, wk_b, wv_b,
                    dsw_freq, dsw_sign)
    o = _dsw_attn(qkv)
    wr = jnp.pad(moe_router[0], ((0, 0), (0, LANES - N_EXPERTS)))
    h3, hn, comb = _dsw_router(h, o.reshape(T, D), dsw_w_o[0].astype(BF16), norm_ffn[1:2], wr)
    out = _moe(hn, comb, h3, final_norm.reshape(1, D), moe_w_gate[0].astype(BF16), moe_w_up[0].astype(BF16),
               moe_w_down[0].astype(BF16))
    return out.reshape(B, S, D)
```

```python
import functools
import math

import numpy as np
import jax
import jax.numpy as jnp
from jax import lax
from jax.experimental import pallas as pl
from jax.experimental.pallas import tpu as pltpu

F32 = jnp.float32
BF16 = jnp.bfloat16

D_MODEL = 1024
RMS_EPS = 1e-6
ROPE_THETA = 500000.0
NEG_INF = -1e30
LANES = 128

MLA_HEADS = 16
MLA_NOPE = 64
MLA_ROPE = 32
MLA_V = 64
MLA_QK = MLA_NOPE + MLA_ROPE
MLA_Q_LORA = 384
MLA_KV_LORA = 256

DSW_BRANCHES = ((128, 1), (512, 4), (2048, 16))
N_BR = 3
DSW_HEADS = 16
DSW_HD = 64
DSW_ROT = 16
DSW_WIDTH = DSW_HEADS * DSW_HD
DSW_BLOCK = 128

FFN_DIM = 2816
N_EXPERTS = 8
EXPERT_DIM = 3584

VMEM_LIMIT = 56 * 1024 * 1024


def _cparams(sem):
    return pltpu.CompilerParams(dimension_semantics=sem, vmem_limit_bytes=VMEM_LIMIT)


def _resident(shape):
    nd = len(shape)
    return pl.BlockSpec(shape, lambda *_: (0,) * nd, pipeline_mode=pl.Buffered(1))


def _rms(x, g):
    return x * lax.rsqrt(jnp.mean(x * x, axis=-1, keepdims=True) + RMS_EPS) * g


def _dot(a, b):
    return jnp.dot(a, b, preferred_element_type=F32)


def _dot_nt(a, b):
    return lax.dot_general(a, b, (((1,), (1,)), ((), ())), preferred_element_type=F32)


def _rope(t, cos, sin_signed):
    return t * cos + pltpu.roll(t, 64, 1) * sin_signed


def _mla_proj_kernel(x_ref, pos_ref, g_ref, wdn_ref, qn_ref, kvn_ref, wuq_ref, wk_ref, wv_ref,
                     freq_ref, sign_ref, q_ref, k_ref, v_ref):
    xn = _rms(x_ref[...], g_ref[...]).astype(BF16)
    down = _dot(xn, wdn_ref[...])
    cq = _rms(down[:, :MLA_Q_LORA], qn_ref[...]).astype(BF16)
    ckv = _rms(down[:, MLA_Q_LORA:MLA_Q_LORA + MLA_KV_LORA], kvn_ref[...]).astype(BF16)
    kr = down[:, MLA_Q_LORA + MLA_KV_LORA:]
    ang = pos_ref[...].astype(F32) * freq_ref[...]
    cos = jnp.cos(ang)
    sin = jnp.sin(ang) * sign_ref[...]
    kr = _rope(kr, cos, sin)
    scale = 1.0 / math.sqrt(MLA_QK)
    cos_q = cos * scale
    sin_q = sin * scale
    q = _dot(cq, wuq_ref[...])
    for h in range(MLA_HEADS):
        q_ref[h] = _rope(q[:, h * LANES:(h + 1) * LANES], cos_q, sin_q).astype(BF16)
    kp = _dot(ckv, wk_ref[...])
    for h in range(MLA_HEADS):
        k_ref[h] = (kp[:, h * LANES:(h + 1) * LANES] + kr).astype(BF16)
    v = _dot(ckv, wv_ref[...])
    for j in range(MLA_HEADS // 2):
        v_ref[j] = v[:, j * LANES:(j + 1) * LANES].astype(BF16)


def _mla_proj(x, pos, g, wdn, qn, kvn, wuq, wk, wv, freq, sign, tm=512):
    B, S, _ = x.shape
    H = MLA_HEADS
    return pl.pallas_call(
        _mla_proj_kernel,
        grid=(B, S // tm),
        in_specs=[
            pl.BlockSpec((None, tm, D_MODEL), lambda b, i: (b, i, 0)),
            pl.BlockSpec((None, tm, 1), lambda b, i: (b, i, 0)),
            _resident(g.shape), _resident(wdn.shape), _resident(qn.shape), _resident(kvn.shape),
            _resident(wuq.shape), _resident(wk.shape), _resident(wv.shape),
            _resident(freq.shape), _resident(sign.shape),
        ],
        out_specs=[
            pl.BlockSpec((None, H, tm, LANES), lambda b, i: (b, 0, i, 0)),
            pl.BlockSpec((None, H, tm, LANES), lambda b, i: (b, 0, i, 0)),
            pl.BlockSpec((None, H // 2, tm, LANES), lambda b, i: (b, 0, i, 0)),
        ],
        out_shape=[
            jax.ShapeDtypeStruct((B, H, S, LANES), BF16),
            jax.ShapeDtypeStruct((B, H, S, LANES), BF16),
            jax.ShapeDtypeStruct((B, H // 2, S, LANES), BF16),
        ],
        compiler_params=_cparams(("parallel", "parallel")),
        name="mla_proj",
    )(x, pos, g, wdn, qn, kvn, wuq, wk, wv, freq, sign)


def _mla_attn_kernel(q_ref, k_ref, v_ref, o_ref, m_scr, l_scr, acc_scr, *, t):
    i = pl.program_id(2)
    m_scr[...] = jnp.full(m_scr.shape, NEG_INF, F32)
    l_scr[...] = jnp.zeros(l_scr.shape, F32)
    acc_scr[...] = jnp.zeros(acc_scr.shape, F32)

    def chunk(c, diagonal):
        off = pl.multiple_of(c * t, t)
        v = v_ref[pl.ds(off, t), :]
        for hh in range(2):
            s = _dot_nt(q_ref[hh], k_ref[hh, pl.ds(off, t), :])
            if diagonal:
                row = lax.broadcasted_iota(jnp.int32, (t, t), 0)
                col = lax.broadcasted_iota(jnp.int32, (t, t), 1)
                s = jnp.where(col <= row, s, NEG_INF)
            m_prev = m_scr[hh]
            m_new = jnp.maximum(m_prev, jnp.max(s, axis=-1, keepdims=True))
            alpha = jnp.exp(m_prev - m_new)
            ps = [jnp.exp(s[:, j * LANES:(j + 1) * LANES] - m_new) for j in range(t // LANES)]
            psum = ps[0]
            for pj in ps[1:]:
                psum = psum + pj
            l_scr[hh] = alpha * l_scr[hh] + jnp.sum(psum, axis=-1, keepdims=True)
            p = jnp.concatenate(ps, axis=1).astype(BF16)
            acc_scr[hh] = alpha * acc_scr[hh] + _dot(p, v)
            m_scr[hh] = m_new

    def body(c, carry):
        chunk(c, False)
        return carry

    lax.fori_loop(0, i, body, 0)
    chunk(i, True)
    lane = lax.broadcasted_iota(jnp.int32, (t, LANES), 1)
    o_ref[...] = jnp.where(lane < MLA_V, acc_scr[0] / l_scr[0], acc_scr[1] / l_scr[1]).astype(BF16)


def _mla_attn(q, k, v, t=512):
    B, H, S, _ = q.shape
    return pl.pallas_call(
        functools.partial(_mla_attn_kernel, t=t),
        grid=(B, H // 2, S // t),
        in_specs=[
            pl.BlockSpec((None, 2, t, LANES), lambda b, j, i: (b, j, i, 0)),
            pl.BlockSpec((None, 2, S, LANES), lambda b, j, i: (b, j, 0, 0)),
            pl.BlockSpec((None, None, S, LANES), lambda b, j, i: (b, j, 0, 0)),
        ],
        out_specs=pl.BlockSpec((None, t, LANES), lambda b, j, i: (b, i, j)),
        out_shape=jax.ShapeDtypeStruct((B, S, (H // 2) * LANES), BF16),
        scratch_shapes=[pltpu.VMEM((2, t, LANES), F32)] * 3,
        compiler_params=_cparams(("parallel", "parallel", "parallel")),
        name="mla_attn",
    )(q, k, v)


def _swiglu_acc(hn, wg_ref, wu_ref, wd_ref, chunk, scale=None):
    acc = None
    n = wg_ref.shape[-1]
    for c0 in range(0, n, chunk):
        g = _dot(hn, wg_ref[:, c0:c0 + chunk])
        u = _dot(hn, wu_ref[:, c0:c0 + chunk])
        a = g * jax.nn.sigmoid(g) * u
        if scale is not None:
            a = a * scale
        part = _dot(a.astype(BF16), wd_ref[c0:c0 + chunk, :])
        acc = part if acc is None else acc + part
    return acc


def _attn_ffn_kernel(x_ref, o_ref, wo_ref, g_ref, wg_ref, wu_ref, wd_ref, h_ref):
    h1 = x_ref[...] + _dot(o_ref[...], wo_ref[...])
    hn = _rms(h1, g_ref[...]).astype(BF16)
    h_ref[...] = h1 + _swiglu_acc(hn, wg_ref, wu_ref, wd_ref, FFN_DIM // 2)


def _attn_ffn(x, o, wo, g, wg, wu, wd, tm=512):
    T = x.shape[0]
    return pl.pallas_call(
        _attn_ffn_kernel,
        grid=(T // tm,),
        in_specs=[
            pl.BlockSpec((tm, D_MODEL), lambda i: (i, 0)),
            pl.BlockSpec((tm, D_MODEL), lambda i: (i, 0)),
            _resident(wo.shape), _resident(g.shape), _resident(wg.shape), _resident(wu.shape),
            _resident(wd.shape),
        ],
        out_specs=pl.BlockSpec((tm, D_MODEL), lambda i: (i, 0)),
        out_shape=jax.ShapeDtypeStruct((T, D_MODEL), F32),
        compiler_params=_cparams(("parallel",)),
        name="attn_ffn",
    )(x, o, wo, g, wg, wu, wd)


def _dsw_proj_kernel(h_ref, pos_ref, gq_ref, gkv_ref, wq_ref, wk_ref, wv_ref, freq_ref, sign_ref,
                     q1, k1, v1, q2, k2, v2, q3, k3, v3,
                     xq_scr, xkv_scr, cos_scr, sin_scr, pq_scr, pkv_scr, pcos_scr, psin_scr, *, tm):
    h = h_ref[...]
    r = lax.rsqrt(jnp.mean(h * h, axis=-1, keepdims=True) + RMS_EPS)
    hr = h * r
    nblk = D_MODEL // LANES
    xq_f = hr * gq_ref[...]
    xkv_f = hr * gkv_ref[...]
    for cb in range(nblk):
        xq_scr[cb] = xq_f[:, cb * LANES:(cb + 1) * LANES]
        xkv_scr[cb] = xkv_f[:, cb * LANES:(cb + 1) * LANES]
    ang = pos_ref[...].astype(F32) * freq_ref[...]
    cos_scr[...] = jnp.cos(ang)
    sin_scr[...] = jnp.sin(ang) * sign_ref[...]
    scale = 1.0 / math.sqrt(DSW_HD)
    outs = ((q1, k1, v1), (q2, k2, v2), (q3, k3, v3))
    for g, (_, d) in enumerate(DSW_BRANCHES):
        n = tm // d
        q_out, k_out, v_out = outs[g]
        if d == 1:
            xq = xq_f.astype(BF16)
            xkv = xkv_f.astype(BF16)
            cos = cos_scr[...]
            sin = sin_scr[...]
        else:
            for rr in range(d):
                rows = slice(rr * n, (rr + 1) * n)
                for cb in range(nblk):
                    cols = slice(cb * LANES, (cb + 1) * LANES)
                    pq_scr[rows, cols] = xq_scr[cb, pl.ds(rr, n, stride=d), :].astype(BF16)
                    pkv_scr[rows, cols] = xkv_scr[cb, pl.ds(rr, n, stride=d), :].astype(BF16)
                pcos_scr[rows, :] = cos_scr[pl.ds(rr, n, stride=d), :]
                psin_scr[rows, :] = sin_scr[pl.ds(rr, n, stride=d), :]
            xq = pq_scr[...]
            xkv = pkv_scr[...]
            cos = pcos_scr[...]
            sin = psin_scr[...]
        q = _dot(xq, wq_ref[g])
        k = _dot(xkv, wk_ref[g])
        v = _dot(xkv, wv_ref[g])
        cos_q = cos * scale
        sin_q = sin * scale
        for j in range(DSW_HEADS // 2):
            sl = slice(j * LANES, (j + 1) * LANES)
            qj = _rope(q[:, sl], cos_q, sin_q).astype(BF16)
            kj = _rope(k[:, sl], cos, sin).astype(BF16)
            vj = v[:, sl].astype(BF16)
            for rr in range(d):
                q_out[j, rr] = qj[rr * n:(rr + 1) * n]
                k_out[j, rr] = kj[rr * n:(rr + 1) * n]
                v_out[j, rr] = vj[rr * n:(rr + 1) * n]


def _dsw_proj(h, pos, gq, gkv, wq, wk, wv, freq, sign, tm=256):
    B, S, _ = h.shape
    P = DSW_HEADS // 2
    out_specs, out_shape = [], []
    for _, d in DSW_BRANCHES:
        for _ in range(3):
            out_specs.append(pl.BlockSpec((None, P, d, tm // d, LANES), lambda b, i: (b, 0, 0, i, 0)))
            out_shape.append(jax.ShapeDtypeStruct((B, P, d, S // d, LANES), BF16))
    return pl.pallas_call(
        functools.partial(_dsw_proj_kernel, tm=tm),
        grid=(B, S // tm),
        in_specs=[
            pl.BlockSpec((None, tm, D_MODEL), lambda b, i: (b, i, 0)),
            pl.BlockSpec((None, tm, 1), lambda b, i: (b, i, 0)),
            _resident(gq.shape), _resident(gkv.shape), _resident(wq.shape), _resident(wk.shape),
            _resident(wv.shape), _resident(freq.shape), _resident(sign.shape),
        ],
        out_specs=out_specs,
        out_shape=out_shape,
        scratch_shapes=[
            pltpu.VMEM((D_MODEL // LANES, tm, LANES), F32), pltpu.VMEM((D_MODEL // LANES, tm, LANES), F32),
            pltpu.VMEM((tm, LANES), F32), pltpu.VMEM((tm, LANES), F32),
            pltpu.VMEM((tm, D_MODEL), BF16), pltpu.VMEM((tm, D_MODEL), BF16),
            pltpu.VMEM((tm, LANES), F32), pltpu.VMEM((tm, LANES), F32),
        ],
        compiler_params=_cparams(("parallel", "parallel")),
        name="dsw_proj",
    )(h, pos, gq, gkv, wq, wk, wv, freq, sign)


def _unroll_of(n):
    for u in (4, 5, 3, 2):
        if n % u == 0:
            return u
    return 1


def _dsw_attn_kernel(q1, k1, v1, q2, k2, v2, q3, k3, v3, o_ref,
                     o1_scr, l1_scr, o2_scr, l2_scr, o3_scr, l3_scr, bo_scr, bl_scr, *, S):
    blk = DSW_BLOCK
    lane = lax.broadcasted_iota(jnp.int32, (blk, LANES), 1)
    mask_a = (lane < 8) | ((lane >= 16) & (lane < 72))
    first_half = lane < DSW_HD
    row1 = lax.broadcasted_iota(jnp.int32, (2 * blk, blk), 0)
    col1 = lax.broadcasted_iota(jnp.int32, (2 * blk, blk), 1)
    row1 = jnp.where(row1 >= blk, row1 - blk, row1)
    valid_first = col1 <= row1
    row2 = lax.broadcasted_iota(jnp.int32, (2 * blk, 2 * blk), 0)
    col2 = lax.broadcasted_iota(jnp.int32, (2 * blk, 2 * blk), 1)
    row2 = jnp.where(row2 >= blk, row2 - blk, row2)
    valid_next = ((col2 < blk) & (col2 >= row2)) | ((col2 >= blk) & (col2 - blk <= row2))

    def attend(qb, kk, vv, valid):
        qf = qb.astype(F32)
        q2 = jnp.concatenate([jnp.where(mask_a, qf, 0.0), jnp.where(mask_a, 0.0, qf)], axis=0).astype(BF16)
        s = jnp.where(valid, _dot_nt(q2, kk), NEG_INF)
        m = jnp.max(s, axis=-1, keepdims=True)
        p = jnp.exp(s - m)
        l = jnp.sum(p, axis=-1, keepdims=True)
        o2 = _dot(p.astype(BF16), vv) / l
        lse = m + jnp.log(l)
        return (jnp.where(first_half, o2[:blk], o2[blk:]),
                jnp.where(first_half, lse[:blk], lse[blk:]))

    def first_block(q_ref, k_ref, v_ref, off, o_dst, l_dst):
        sl = pl.ds(pl.multiple_of(off, blk), blk)
        o, l = attend(q_ref[sl, :], k_ref[sl, :], v_ref[sl, :], valid_first)
        o_dst[sl, :] = o
        l_dst[sl, :] = l

    def next_block(q_ref, k_ref, v_ref, off, o_dst, l_dst):
        sl = pl.ds(pl.multiple_of(off, blk), blk)
        win = pl.ds(pl.multiple_of(off - blk, blk), 2 * blk)
        o, l = attend(q_ref[sl, :], k_ref[win, :], v_ref[win, :], valid_next)
        o_dst[sl, :] = o
        l_dst[sl, :] = l

    branches = ((q1, k1, v1, o1_scr, l1_scr, o1_scr, l1_scr),
                (q2, k2, v2, bo_scr, bl_scr, o2_scr, l2_scr),
                (q3, k3, v3, bo_scr, bl_scr, o3_scr, l3_scr))
    for (_, d), (q_ref, k_ref, v_ref, o_dst, l_dst, o_tok, l_tok) in zip(DSW_BRANCHES, branches):
        L = S // d
        nb = L // blk

        def first_body(rr, c, q_ref=q_ref, k_ref=k_ref, v_ref=v_ref, o_dst=o_dst, l_dst=l_dst, L=L):
            first_block(q_ref, k_ref, v_ref, rr * L, o_dst, l_dst)
            return c

        lax.fori_loop(0, d, first_body, 0, unroll=_unroll_of(d))
        if nb > 1:
            def next_body(idx, c, q_ref=q_ref, k_ref=k_ref, v_ref=v_ref, o_dst=o_dst, l_dst=l_dst,
                          L=L, nb=nb):
                rr = idx // (nb - 1)
                ii = idx - rr * (nb - 1) + 1
                next_block(q_ref, k_ref, v_ref, rr * L + ii * blk, o_dst, l_dst)
                return c

            lax.fori_loop(0, d * (nb - 1), next_body, 0, unroll=_unroll_of(d * (nb - 1)))
        if d > 1:
            for rr in range(d):
                o_tok[pl.ds(rr, L, stride=d), :] = o_dst[rr * L:(rr + 1) * L, :]
                l_tok[pl.ds(rr, L, stride=d), :] = l_dst[rr * L:(rr + 1) * L, :]

    ch = 256
    for c0 in range(0, S, ch):
        sl = slice(c0, c0 + ch)
        la, lb, lc = l1_scr[sl, :], l2_scr[sl, :], l3_scr[sl, :]
        m = jnp.maximum(jnp.maximum(la, lb), lc)
        wa, wb, wc = jnp.exp(la - m), jnp.exp(lb - m), jnp.exp(lc - m)
        num = wa * o1_scr[sl, :] + wb * o2_scr[sl, :] + wc * o3_scr[sl, :]
        o_ref[sl, :] = (num / (wa + wb + wc)).astype(BF16)


def _dsw_attn(qkv):
    B, P = qkv[0].shape[0], qkv[0].shape[1]
    S = qkv[0].shape[2] * qkv[0].shape[3]
    flat = [a.reshape(B, P, S, LANES) for a in qkv]
    spec = pl.BlockSpec((None, None, S, LANES), lambda b, j: (b, j, 0, 0))
    return pl.pallas_call(
        functools.partial(_dsw_attn_kernel, S=S),
        grid=(B, P),
        in_specs=[spec] * 9,
        out_specs=pl.BlockSpec((None, S, LANES), lambda b, j: (b, 0, j)),
        out_shape=jax.ShapeDtypeStruct((B, S, P * LANES), BF16),
        scratch_shapes=[pltpu.VMEM((S, LANES), F32)] * 8,
        compiler_params=_cparams(("parallel", "parallel")),
        name="dsw_attn",
    )(*flat)


def _dsw_router_kernel(h_ref, o_ref, wo_ref, g_ref, wr_ref, h3_ref, hn_ref, comb_ref):
    h3 = h_ref[...] + _dot(o_ref[...], wo_ref[...])
    h3_ref[...] = h3
    hn = _rms(h3, g_ref[...])
    hn_ref[...] = hn.astype(BF16)
    logits = jnp.dot(hn, wr_ref[...], preferred_element_type=F32, precision=lax.Precision.HIGHEST)
    lane = lax.broadcasted_iota(jnp.int32, logits.shape, 1).astype(F32)
    logits = jnp.where(lane < N_EXPERTS, logits, -jnp.inf)
    v1 = jnp.max(logits, axis=-1, keepdims=True)
    i1 = jnp.min(jnp.where(logits == v1, lane, float(LANES)), axis=-1, keepdims=True)
    rest = jnp.where(lane == i1, -jnp.inf, logits)
    v2 = jnp.max(rest, axis=-1, keepdims=True)
    i2 = jnp.min(jnp.where(rest == v2, lane, float(LANES)), axis=-1, keepdims=True)
    e2 = jnp.exp(v2 - v1)
    den = 1.0 + e2
    comb_ref[...] = jnp.where(lane == i1, 1.0 / den, jnp.where(lane == i2, e2 / den, 0.0))


def _dsw_router(h, o, wo, g, wr, tm=512):
    T = h.shape[0]
    return pl.pallas_call(
        _dsw_router_kernel,
        grid=(T // tm,),
        in_specs=[
            pl.BlockSpec((tm, D_MODEL), lambda i: (i, 0)),
            pl.BlockSpec((tm, D_MODEL), lambda i: (i, 0)),
            _resident(wo.shape), _resident(g.shape), _resident(wr.shape),
        ],
        out_specs=[
            pl.BlockSpec((tm, D_MODEL), lambda i: (i, 0)),
            pl.BlockSpec((tm, D_MODEL), lambda i: (i, 0)),
            pl.BlockSpec((tm, LANES), lambda i: (i, 0)),
        ],
        out_shape=[
            jax.ShapeDtypeStruct((T, D_MODEL), F32),
            jax.ShapeDtypeStruct((T, D_MODEL), BF16),
            jax.ShapeDtypeStruct((T, LANES), F32),
        ],
        compiler_params=_cparams(("parallel",)),
        name="dsw_router",
    )(h, o, wo, g, wr)


def _moe_kernel(hn_ref, comb_ref, h3_ref, gf_ref, wg_ref, wu_ref, wd_ref, out_ref, acc_ref):
    e = pl.program_id(1)
    f = pl.program_id(2)

    @pl.when((e == 0) & (f == 0))
    def _():
        acc_ref[...] = jnp.zeros_like(acc_ref)

    comb = comb_ref[...]
    lane = lax.broadcasted_iota(jnp.int32, comb.shape, 1)
    gate = jnp.sum(jnp.where(lane == e, comb, 0.0), axis=-1, keepdims=True)
    acc_ref[...] += _swiglu_acc(hn_ref[...], wg_ref, wu_ref, wd_ref, wg_ref.shape[-1], scale=gate)

    @pl.when((e == pl.num_programs(1) - 1) & (f == pl.num_programs(2) - 1))
    def _():
        out_ref[...] = _rms(h3_ref[...] + acc_ref[...], gf_ref[...])


def _moe(hn, comb, h3, gf, wg, wu, wd, tm=512, tf=512):
    T = hn.shape[0]
    E, _, Fe = wg.shape
    return pl.pallas_call(
        _moe_kernel,
        grid=(T // tm, E, Fe // tf),
        in_specs=[
            pl.BlockSpec((tm, D_MODEL), lambda i, e, f: (i, 0)),
            pl.BlockSpec((tm, LANES), lambda i, e, f: (i, 0)),
            pl.BlockSpec((tm, D_MODEL), lambda i, e, f: (i, 0)),
            pl.BlockSpec((1, D_MODEL), lambda i, e, f: (0, 0)),
            pl.BlockSpec((None, D_MODEL, tf), lambda i, e, f: (e, 0, f)),
            pl.BlockSpec((None, D_MODEL, tf), lambda i, e, f: (e, 0, f)),
            pl.BlockSpec((None, tf, D_MODEL), lambda i, e, f: (e, f, 0)),
        ],
        out_specs=pl.BlockSpec((tm, D_MODEL), lambda i, e, f: (i, 0)),
        out_shape=jax.ShapeDtypeStruct((T, D_MODEL), F32),
        scratch_shapes=[pltpu.VMEM((tm, D_MODEL), F32)],
        compiler_params=_cparams(("parallel", "arbitrary", "arbitrary")),
        name="moe",
    )(hn, comb, h3, gf, wg, wu, wd)


def _take_cols(w, src):
    src = np.asarray(src)
    cols = jnp.take(w, jnp.asarray(np.maximum(src, 0)), axis=1)
    return jnp.where(jnp.asarray(src >= 0)[None, :], cols, 0.0)


def _mla_lane_of_dim():
    lane = np.zeros(MLA_QK, np.int64)
    for dd in range(MLA_NOPE):
        lane[dd] = 16 + dd if dd < 48 else 80 + (dd - 48)
    half = MLA_ROPE // 2
    for i in range(MLA_ROPE):
        lane[MLA_NOPE + i] = i if i < half else 64 + (i - half)
    return lane


def _dsw_lane_of(hh, dd):
    half = DSW_ROT // 2
    if dd < half:
        return hh * half + dd
    if dd < DSW_ROT:
        return 64 + hh * half + (dd - half)
    return (16 if hh == 0 else 80) + (dd - DSW_ROT)


def _rope_lane_tables(half, per_lane_index):
    inv_freq = jnp.power(jnp.float32(ROPE_THETA), -jnp.arange(half, dtype=F32) * (2.0 / (2 * half)))
    idx = np.zeros(LANES, np.int64)
    on = np.zeros(LANES, bool)
    sign = np.zeros(LANES, np.float32)
    for l in range(LANES):
        i = per_lane_index(l)
        if i is not None:
            idx[l], on[l] = i, True
            sign[l] = -1.0 if l < 64 else 1.0
    freq = jnp.where(jnp.asarray(on), jnp.take(inv_freq, jnp.asarray(idx)), 0.0)
    return freq.reshape(1, LANES), jnp.asarray(sign).reshape(1, LANES)


def kernel(x, positions, norm_attn, norm_ffn, mla_w_down, mla_q_norm, mla_w_uq, mla_kv_norm, mla_w_ukv,
           mla_w_o, dsw_kv_norm, dsw_w_kv, dsw_w_q, dsw_w_o, ffn_w_gate, ffn_w_up, ffn_w_down, moe_router,
           moe_w_gate, moe_w_up, moe_w_down, final_norm):
    B, S, D = x.shape
    T = B * S
    pos = positions.reshape(B, S, 1)

    lane_of = _mla_lane_of_dim()
    H = MLA_HEADS
    src_dn = np.full(MLA_Q_LORA + MLA_KV_LORA + LANES, -1, np.int64)
    src_dn[:MLA_Q_LORA + MLA_KV_LORA] = np.arange(MLA_Q_LORA + MLA_KV_LORA)
    src_uq = np.full(H * LANES, -1, np.int64)
    src_k = np.full(H * LANES, -1, np.int64)
    src_v = np.zeros(H * MLA_V, np.int64)
    for i in range(MLA_ROPE):
        src_dn[MLA_Q_LORA + MLA_KV_LORA + lane_of[MLA_NOPE + i]] = MLA_Q_LORA + MLA_KV_LORA + i
    for h in range(H):
        for dd in range(MLA_QK):
            src_uq[h * LANES + lane_of[dd]] = h * MLA_QK + dd
        for dd in range(MLA_NOPE):
            src_k[h * LANES + lane_of[dd]] = h * (MLA_NOPE + MLA_V) + dd
        for dd in range(MLA_V):
            src_v[h * MLA_V + dd] = h * (MLA_NOPE + MLA_V) + MLA_NOPE + dd
    wdn = _take_cols(mla_w_down[0], src_dn).astype(BF16)
    wuq = _take_cols(mla_w_uq[0], src_uq).astype(BF16)
    wk = _take_cols(mla_w_ukv[0], src_k).astype(BF16)
    wv = _take_cols(mla_w_ukv[0], src_v).astype(BF16)
    half = MLA_ROPE // 2
    mla_freq, mla_sign = _rope_lane_tables(
        half, lambda l: l if l < half else (l - 64 if 64 <= l < 64 + half else None))

    q, k, v = _mla_proj(x, pos, norm_attn[0:1], wdn, mla_q_norm[0:1], mla_kv_norm[0:1], wuq, wk, wv,
                        mla_freq, mla_sign)
    o = _mla_attn(q, k, v)
    h = _attn_ffn(x.reshape(T, D), o.reshape(T, D), mla_w_o[0].astype(BF16), norm_ffn[0:1],
                  ffn_w_gate[0].astype(BF16), ffn_w_up[0].astype(BF16), ffn_w_down[0].astype(BF16))

    src_pair = np.zeros(DSW_WIDTH, np.int64)
    for j in range(DSW_HEADS // 2):
        for hh in range(2):
            for dd in range(DSW_HD):
                src_pair[j * LANES + _dsw_lane_of(hh, dd)] = (2 * j + hh) * DSW_HD + dd
    wq_b, wk_b, wv_b = [], [], []
    for g in range(N_BR):
        wq_b.append(_take_cols(dsw_w_q[0], g * DSW_WIDTH + src_pair))
        wk_b.append(_take_cols(dsw_w_kv, g * DSW_WIDTH + src_pair))
        wv_b.append(dsw_w_kv[:, (N_BR + g) * DSW_WIDTH:(N_BR + g + 1) * DSW_WIDTH])
    wq_b = jnp.stack(wq_b).astype(BF16)
    wk_b = jnp.stack(wk_b).astype(BF16)
    wv_b = jnp.stack(wv_b).astype(BF16)
    half = DSW_ROT // 2
    dsw_freq, dsw_sign = _rope_lane_tables(
        half, lambda l: l % half if l < 2 * half else ((l - 64) % half if 64 <= l < 64 + 2 * half else None))

    qkv = _dsw_proj(h.reshape(B, S, D), pos, norm_attn[1:2], dsw_kv_norm.reshape(1, D), wq_b, wk_b, wv_b,
                    dsw_freq, dsw_sign)
    o = _dsw_attn(qkv)
    wr = jnp.pad(moe_router[0], ((0, 0), (0, LANES - N_EXPERTS)))
    h3, hn, comb = _dsw_router(h, o.reshape(T, D), dsw_w_o[0].astype(BF16), norm_ffn[1:2], wr)
    out = _moe(hn, comb, h3, final_norm.reshape(1, D), moe_w_gate[0].astype(BF16), moe_w_up[0].astype(BF16),
               moe_w_down[0].astype(BF16))
    return out.reshape(B, S, D)
```

```python
import functools
import math

import numpy as np
import jax
import jax.numpy as jnp
from jax import lax
from jax.experimental import pallas as pl
from jax.experimental.pallas import tpu as pltpu

F32 = jnp.float32
BF16 = jnp.bfloat16

D_MODEL = 1024
RMS_EPS = 1e-6
ROPE_THETA = 500000.0
NEG_INF = -1e30
LANES = 128

MLA_HEADS = 16
MLA_NOPE = 64
MLA_ROPE = 32
MLA_V = 64
MLA_QK = MLA_NOPE + MLA_ROPE
MLA_Q_LORA = 384
MLA_KV_LORA = 256

DSW_BRANCHES = ((128, 1), (512, 4), (2048, 16))
N_BR = 3
DSW_HEADS = 16
DSW_HD = 64
DSW_ROT = 16
DSW_WIDTH = DSW_HEADS * DSW_HD
DSW_BLOCK = 128

FFN_DIM = 2816
N_EXPERTS = 8
EXPERT_DIM = 3584

VMEM_LIMIT = 56 * 1024 * 1024


def _cparams(sem):
    return pltpu.CompilerParams(dimension_semantics=sem, vmem_limit_bytes=VMEM_LIMIT)


def _resident(shape):
    nd = len(shape)
    return pl.BlockSpec(shape, lambda *_: (0,) * nd, pipeline_mode=pl.Buffered(1))


def _rms(x, g):
    return x * lax.rsqrt(jnp.mean(x * x, axis=-1, keepdims=True) + RMS_EPS) * g


def _dot(a, b):
    return jnp.dot(a, b, preferred_element_type=F32)


def _dot_nt(a, b):
    return lax.dot_general(a, b, (((1,), (1,)), ((), ())), preferred_element_type=F32)


def _rope(t, cos, sin_signed):
    return t * cos + pltpu.roll(t, 64, 1) * sin_signed


def _mla_proj_kernel(x_ref, pos_ref, g_ref, wdn_ref, qn_ref, kvn_ref, wuq_ref, wk_ref, wv_ref,
                     freq_ref, sign_ref, q_ref, k_ref, v_ref):
    xn = _rms(x_ref[...], g_ref[...]).astype(BF16)
    down = _dot(xn, wdn_ref[...])
    cq = _rms(down[:, :MLA_Q_LORA], qn_ref[...]).astype(BF16)
    ckv = _rms(down[:, MLA_Q_LORA:MLA_Q_LORA + MLA_KV_LORA], kvn_ref[...]).astype(BF16)
    kr = down[:, MLA_Q_LORA + MLA_KV_LORA:]
    ang = pos_ref[...].astype(F32) * freq_ref[...]
    cos = jnp.cos(ang)
    sin = jnp.sin(ang) * sign_ref[...]
    kr = _rope(kr, cos, sin)
    scale = 1.0 / math.sqrt(MLA_QK)
    cos_q = cos * scale
    sin_q = sin * scale
    q = _dot(cq, wuq_ref[...])
    for h in range(MLA_HEADS):
        q_ref[h] = _rope(q[:, h * LANES:(h + 1) * LANES], cos_q, sin_q).astype(BF16)
    kp = _dot(ckv, wk_ref[...])
    for h in range(MLA_HEADS):
        k_ref[h] = (kp[:, h * LANES:(h + 1) * LANES] + kr).astype(BF16)
    v = _dot(ckv, wv_ref[...])
    for j in range(MLA_HEADS // 2):
        v_ref[j] = v[:, j * LANES:(j + 1) * LANES].astype(BF16)


def _mla_proj(x, pos, g, wdn, qn, kvn, wuq, wk, wv, freq, sign, tm=512):
    B, S, _ = x.shape
    H = MLA_HEADS
    return pl.pallas_call(
        _mla_proj_kernel,
        grid=(B, S // tm),
        in_specs=[
            pl.BlockSpec((None, tm, D_MODEL), lambda b, i: (b, i, 0)),
            pl.BlockSpec((None, tm, 1), lambda b, i: (b, i, 0)),
            _resident(g.shape), _resident(wdn.shape), _resident(qn.shape), _resident(kvn.shape),
            _resident(wuq.shape), _resident(wk.shape), _resident(wv.shape),
            _resident(freq.shape), _resident(sign.shape),
        ],
        out_specs=[
            pl.BlockSpec((None, H, tm, LANES), lambda b, i: (b, 0, i, 0)),
            pl.BlockSpec((None, H, tm, LANES), lambda b, i: (b, 0, i, 0)),
            pl.BlockSpec((None, H // 2, tm, LANES), lambda b, i: (b, 0, i, 0)),
        ],
        out_shape=[
            jax.ShapeDtypeStruct((B, H, S, LANES), BF16),
            jax.ShapeDtypeStruct((B, H, S, LANES), BF16),
            jax.ShapeDtypeStruct((B, H // 2, S, LANES), BF16),
        ],
        compiler_params=_cparams(("parallel", "parallel")),
        name="mla_proj",
    )(x, pos, g, wdn, qn, kvn, wuq, wk, wv, freq, sign)


def _mla_attn_kernel(q_ref, k_ref, v_ref, o_ref, m_scr, l_scr, acc_scr, *, t):
    i = pl.program_id(2)
    m_scr[...] = jnp.full(m_scr.shape, NEG_INF, F32)
    l_scr[...] = jnp.zeros(l_scr.shape, F32)
    acc_scr[...] = jnp.zeros(acc_scr.shape, F32)

    def chunk(c, diagonal):
        off = pl.multiple_of(c * t, t)
        v = v_ref[pl.ds(off, t), :]
        for hh in range(2):
            s = _dot_nt(q_ref[hh], k_ref[hh, pl.ds(off, t), :])
            if diagonal:
                row = lax.broadcasted_iota(jnp.int32, (t, t), 0)
                col = lax.broadcasted_iota(jnp.int32, (t, t), 1)
                s = jnp.where(col <= row, s, NEG_INF)
            m_prev = m_scr[hh]
            m_new = jnp.maximum(m_prev, jnp.max(s, axis=-1, keepdims=True))
            alpha = jnp.exp(m_prev - m_new)
            ps = [jnp.exp(s[:, j * LANES:(j + 1) * LANES] - m_new) for j in range(t // LANES)]
            psum = ps[0]
            for pj in ps[1:]:
                psum = psum + pj
            l_scr[hh] = alpha * l_scr[hh] + jnp.sum(psum, axis=-1, keepdims=True)
            p = jnp.concatenate(ps, axis=1).astype(BF16)
            acc_scr[hh] = alpha * acc_scr[hh] + _dot(p, v)
            m_scr[hh] = m_new

    def body(c, carry):
        chunk(c, False)
        return carry

    lax.fori_loop(0, i, body, 0)
    chunk(i, True)
    lane = lax.broadcasted_iota(jnp.int32, (t, LANES), 1)
    o_ref[...] = jnp.where(lane < MLA_V, acc_scr[0] / l_scr[0], acc_scr[1] / l_scr[1]).astype(BF16)


def _mla_attn(q, k, v, t=512):
    B, H, S, _ = q.shape
    return pl.pallas_call(
        functools.partial(_mla_attn_kernel, t=t),
        grid=(B, H // 2, S // t),
        in_specs=[
            pl.BlockSpec((None, 2, t, LANES), lambda b, j, i: (b, j, i, 0)),
            pl.BlockSpec((None, 2, S, LANES), lambda b, j, i: (b, j, 0, 0)),
            pl.BlockSpec((None, None, S, LANES), lambda b, j, i: (b, j, 0, 0)),
        ],
        out_specs=pl.BlockSpec((None, t, LANES), lambda b, j, i: (b, i, j)),
        out_shape=jax.ShapeDtypeStruct((B, S, (H // 2) * LANES), BF16),
        scratch_shapes=[pltpu.VMEM((2, t, LANES), F32)] * 3,
        compiler_params=_cparams(("parallel", "parallel", "parallel")),
        name="mla_attn",
    )(q, k, v)


def _swiglu_acc(hn, wg_ref, wu_ref, wd_ref, chunk, scale=None):
    acc = None
    n = wg_ref.shape[-1]
    for c0 in range(0, n, chunk):
        g = _dot(hn, wg_ref[:, c0:c0 + chunk])
        u = _dot(hn, wu_ref[:, c0:c0 + chunk])
        a = g * jax.nn.sigmoid(g) * u
        if scale is not None:
            a = a * scale
        part = _dot(a.astype(BF16), wd_ref[c0:c0 + chunk, :])
        acc = part if acc is None else acc + part
    return acc


def _attn_ffn_kernel(x_ref, o_ref, wo_ref, g_ref, wg_ref, wu_ref, wd_ref, h_ref):
    h1 = x_ref[...] + _dot(o_ref[...], wo_ref[...])
    hn = _rms(h1, g_ref[...]).astype(BF16)
    h_ref[...] = h1 + _swiglu_acc(hn, wg_ref, wu_ref, wd_ref, FFN_DIM // 2)


def _attn_ffn(x, o, wo, g, wg, wu, wd, tm=512):
    T = x.shape[0]
    return pl.pallas_call(
        _attn_ffn_kernel,
        grid=(T // tm,),
        in_specs=[
            pl.BlockSpec((tm, D_MODEL), lambda i: (i, 0)),
            pl.BlockSpec((tm, D_MODEL), lambda i: (i, 0)),
            _resident(wo.shape), _resident(g.shape), _resident(wg.shape), _resident(wu.shape),
            _resident(wd.shape),
        ],
        out_specs=pl.BlockSpec((tm, D_MODEL), lambda i: (i, 0)),
        out_shape=jax.ShapeDtypeStruct((T, D_MODEL), F32),
        compiler_params=_cparams(("parallel",)),
        name="attn_ffn",
    )(x, o, wo, g, wg, wu, wd)


def _dsw_proj_kernel(h_ref, pos_ref, gq_ref, gkv_ref, wq_ref, wk_ref, wv_ref, freq_ref, sign_ref,
                     q1, k1, v1, q2, k2, v2, q3, k3, v3,
                     xq_scr, xkv_scr, cos_scr, sin_scr, pq_scr, pkv_scr, pcos_scr, psin_scr, *, tm):
    h = h_ref[...]
    r = lax.rsqrt(jnp.mean(h * h, axis=-1, keepdims=True) + RMS_EPS)
    hr = h * r
    nblk = D_MODEL // LANES
    xq_f = hr * gq_ref[...]
    xkv_f = hr * gkv_ref[...]
    for cb in range(nblk):
        xq_scr[cb] = xq_f[:, cb * LANES:(cb + 1) * LANES]
        xkv_scr[cb] = xkv_f[:, cb * LANES:(cb + 1) * LANES]
    ang = pos_ref[...].astype(F32) * freq_ref[...]
    cos_scr[...] = jnp.cos(ang)
    sin_scr[...] = jnp.sin(ang) * sign_ref[...]
    scale = 1.0 / math.sqrt(DSW_HD)
    outs = ((q1, k1, v1), (q2, k2, v2), (q3, k3, v3))
    for g, (_, d) in enumerate(DSW_BRANCHES):
        n = tm // d
        q_out, k_out, v_out = outs[g]
        if d == 1:
            xq = xq_f.astype(BF16)
            xkv = xkv_f.astype(BF16)
            cos = cos_scr[...]
            sin = sin_scr[...]
        else:
            for rr in range(d):
                rows = slice(rr * n, (rr + 1) * n)
                for cb in range(nblk):
                    cols = slice(cb * LANES, (cb + 1) * LANES)
                    pq_scr[rows, cols] = xq_scr[cb, pl.ds(rr, n, stride=d), :].astype(BF16)
                    pkv_scr[rows, cols] = xkv_scr[cb, pl.ds(rr, n, stride=d), :].astype(BF16)
                pcos_scr[rows, :] = cos_scr[pl.ds(rr, n, stride=d), :]
                psin_scr[rows, :] = sin_scr[pl.ds(rr, n, stride=d), :]
            xq = pq_scr[...]
            xkv = pkv_scr[...]
            cos = pcos_scr[...]
            sin = psin_scr[...]
        q = _dot(xq, wq_ref[g])
        k = _dot(xkv, wk_ref[g])
        v = _dot(xkv, wv_ref[g])
        cos_q = cos * scale
        sin_q = sin * scale
        for j in range(DSW_HEADS // 2):
            sl = slice(j * LANES, (j + 1) * LANES)
            qj = _rope(q[:, sl], cos_q, sin_q).astype(BF16)
            kj = _rope(k[:, sl], cos, sin).astype(BF16)
            vj = v[:, sl].astype(BF16)
            for rr in range(d):
                q_out[j, rr] = qj[rr * n:(rr + 1) * n]
                k_out[j, rr] = kj[rr * n:(rr + 1) * n]
                v_out[j, rr] = vj[rr * n:(rr + 1) * n]


def _dsw_proj(h, pos, gq, gkv, wq, wk, wv, freq, sign, tm=256):
    B, S, _ = h.shape
    P = DSW_HEADS // 2
    out_specs, out_shape = [], []
    for _, d in DSW_BRANCHES:
        for _ in range(3):
            out_specs.append(pl.BlockSpec((None, P, d, tm // d, LANES), lambda b, i: (b, 0, 0, i, 0)))
            out_shape.append(jax.ShapeDtypeStruct((B, P, d, S // d, LANES), BF16))
    return pl.pallas_call(
        functools.partial(_dsw_proj_kernel, tm=tm),
        grid=(B, S // tm),
        in_specs=[
            pl.BlockSpec((None, tm, D_MODEL), lambda b, i: (b, i, 0)),
            pl.BlockSpec((None, tm, 1), lambda b, i: (b, i, 0)),
            _resident(gq.shape), _resident(gkv.shape), _resident(wq.shape), _resident(wk.shape),
            _resident(wv.shape), _resident(freq.shape), _resident(sign.shape),
        ],
        out_specs=out_specs,
        out_shape=out_shape,
        scratch_shapes=[
            pltpu.VMEM((D_MODEL // LANES, tm, LANES), F32), pltpu.VMEM((D_MODEL // LANES, tm, LANES), F32),
            pltpu.VMEM((tm, LANES), F32), pltpu.VMEM((tm, LANES), F32),
            pltpu.VMEM((tm, D_MODEL), BF16), pltpu.VMEM((tm, D_MODEL), BF16),
            pltpu.VMEM((tm, LANES), F32), pltpu.VMEM((tm, LANES), F32),
        ],
        compiler_params=_cparams(("parallel", "parallel")),
        name="dsw_proj",
    )(h, pos, gq, gkv, wq, wk, wv, freq, sign)


def _unroll_of(n):
    for u in (4, 5, 3, 2):
        if n % u == 0:
            return u
    return 1


def _dsw_attn_kernel(q1, k1, v1, q2, k2, v2, q3, k3, v3, o_ref,
                     o1_scr, l1_scr, o2_scr, l2_scr, o3_scr, l3_scr, bo_scr, bl_scr, *, S):
    blk = DSW_BLOCK
    lane = lax.broadcasted_iota(jnp.int32, (blk, LANES), 1)
    mask_a = (lane < 8) | ((lane >= 16) & (lane < 72))
    first_half = lane < DSW_HD
    row1 = lax.broadcasted_iota(jnp.int32, (2 * blk, blk), 0)
    col1 = lax.broadcasted_iota(jnp.int32, (2 * blk, blk), 1)
    row1 = jnp.where(row1 >= blk, row1 - blk, row1)
    valid_first = col1 <= row1
    row2 = lax.broadcasted_iota(jnp.int32, (2 * blk, 2 * blk), 0)
    col2 = lax.broadcasted_iota(jnp.int32, (2 * blk, 2 * blk), 1)
    row2 = jnp.where(row2 >= blk, row2 - blk, row2)
    valid_next = ((col2 < blk) & (col2 >= row2)) | ((col2 >= blk) & (col2 - blk <= row2))

    def attend(qb, kk, vv, valid):
        qf = qb.astype(F32)
        q2 = jnp.concatenate([jnp.where(mask_a, qf, 0.0), jnp.where(mask_a, 0.0, qf)], axis=0).astype(BF16)
        s = jnp.where(valid, _dot_nt(q2, kk), NEG_INF)
        m = jnp.max(s, axis=-1, keepdims=True)
        p = jnp.exp(s - m)
        l = jnp.sum(p, axis=-1, keepdims=True)
        o2 = _dot(p.astype(BF16), vv) / l
        lse = m + jnp.log(l)
        return (jnp.where(first_half, o2[:blk], o2[blk:]),
                jnp.where(first_half, lse[:blk], lse[blk:]))

    def first_block(q_ref, k_ref, v_ref, off, o_dst, l_dst):
        sl = pl.ds(pl.multiple_of(off, blk), blk)
        o, l = attend(q_ref[sl, :], k_ref[sl, :], v_ref[sl, :], valid_first)
        o_dst[sl, :] = o
        l_dst[sl, :] = l

    def next_block(q_ref, k_ref, v_ref, off, o_dst, l_dst):
        sl = pl.ds(pl.multiple_of(off, blk), blk)
        win = pl.ds(pl.multiple_of(off - blk, blk), 2 * blk)
        o, l = attend(q_ref[sl, :], k_ref[win, :], v_ref[win, :], valid_next)
        o_dst[sl, :] = o
        l_dst[sl, :] = l

    branches = ((q1, k1, v1, o1_scr, l1_scr, o1_scr, l1_scr),
                (q2, k2, v2, bo_scr, bl_scr, o2_scr, l2_scr),
                (q3, k3, v3, bo_scr, bl_scr, o3_scr, l3_scr))
    for (_, d), (q_ref, k_ref, v_ref, o_dst, l_dst, o_tok, l_tok) in zip(DSW_BRANCHES, branches):
        L = S // d
        nb = L // blk

        def first_body(rr, c, q_ref=q_ref, k_ref=k_ref, v_ref=v_ref, o_dst=o_dst, l_dst=l_dst, L=L):
            first_block(q_ref, k_ref, v_ref, rr * L, o_dst, l_dst)
            return c

        lax.fori_loop(0, d, first_body, 0, unroll=_unroll_of(d))
        if nb > 1:
            def next_body(idx, c, q_ref=q_ref, k_ref=k_ref, v_ref=v_ref, o_dst=o_dst, l_dst=l_dst,
                          L=L, nb=nb):
                rr = idx // (nb - 1)
                ii = idx - rr * (nb - 1) + 1
                next_block(q_ref, k_ref, v_ref, rr * L + ii * blk, o_dst, l_dst)
                return c

            lax.fori_loop(0, d * (nb - 1), next_body, 0, unroll=_unroll_of(d * (nb - 1)))
        if d > 1:
            for rr in range(d):
                o_tok[pl.ds(rr, L, stride=d), :] = o_dst[rr * L:(rr + 1) * L, :]
                l_tok[pl.ds(rr, L, stride=d), :] = l_dst[rr * L:(rr + 1) * L, :]

    ch = 256
    for c0 in range(0, S, ch):
        sl = slice(c0, c0 + ch)
        la, lb, lc = l1_scr[sl, :], l2_scr[sl, :], l3_scr[sl, :]
        m = jnp.maximum(jnp.maximum(la, lb), lc)
        wa, wb, wc = jnp.exp(la - m), jnp.exp(lb - m), jnp.exp(lc - m)
        num = wa * o1_scr[sl, :] + wb * o2_scr[sl, :] + wc * o3_scr[sl, :]
        o_ref[sl, :] = (num / (wa + wb + wc)).astype(BF16)


def _dsw_attn(qkv):
    B, P = qkv[0].shape[0], qkv[0].shape[1]
    S = qkv[0].shape[2] * qkv[0].shape[3]
    flat = [a.reshape(B, P, S, LANES) for a in qkv]
    spec = pl.BlockSpec((None, None, S, LANES), lambda b, j: (b, j, 0, 0))
    return pl.pallas_call(
        functools.partial(_dsw_attn_kernel, S=S),
        grid=(B, P),
        in_specs=[spec] * 9,
        out_specs=pl.BlockSpec((None, S, LANES), lambda b, j: (b, 0, j)),
        out_shape=jax.ShapeDtypeStruct((B, S, P * LANES), BF16),
        scratch_shapes=[pltpu.VMEM((S, LANES), F32)] * 8,
        compiler_params=_cparams(("parallel", "parallel")),
        name="dsw_attn",
    )(*flat)


META_E1, META_E2, META_R1, META_R2, META_G1, META_G2 = range(6)


def _dsw_router_kernel(h_ref, o_ref, wo_ref, g_ref, wr_ref, h3_ref, hn_ref, meta_ref, cnt_ref, base_scr):
    @pl.when(pl.program_id(0) == 0)
    def _():
        base_scr[...] = jnp.zeros_like(base_scr)

    h3 = h_ref[...] + _dot(o_ref[...], wo_ref[...])
    h3_ref[...] = h3
    hn = _rms(h3, g_ref[...])
    hn_ref[...] = hn
    logits = jnp.dot(hn, wr_ref[...], preferred_element_type=F32, precision=lax.Precision.HIGHEST)
    tm = logits.shape[0]
    lane = lax.broadcasted_iota(jnp.int32, logits.shape, 1).astype(F32)
    logits = jnp.where(lane < N_EXPERTS, logits, -jnp.inf)
    v1 = jnp.max(logits, axis=-1, keepdims=True)
    i1 = jnp.min(jnp.where(logits == v1, lane, float(LANES)), axis=-1, keepdims=True)
    rest = jnp.where(lane == i1, -jnp.inf, logits)
    v2 = jnp.max(rest, axis=-1, keepdims=True)
    i2 = jnp.min(jnp.where(rest == v2, lane, float(LANES)), axis=-1, keepdims=True)
    e2 = jnp.exp(v2 - v1)
    den = 1.0 + e2
    onehot = jnp.where((lane == i1) | (lane == i2), 1.0, 0.0)
    row = lax.broadcasted_iota(jnp.int32, (tm, tm), 0)
    col = lax.broadcasted_iota(jnp.int32, (tm, tm), 1)
    earlier = jnp.where(col < row, 1.0, 0.0).astype(BF16)
    before = _dot(earlier, onehot.astype(BF16)) + base_scr[0:1, :]
    r1 = jnp.sum(jnp.where(lane == i1, before, 0.0), axis=-1, keepdims=True)
    r2 = jnp.sum(jnp.where(lane == i2, before, 0.0), axis=-1, keepdims=True)
    total = base_scr[0:1, :] + jnp.sum(onehot, axis=0, keepdims=True)
    base_scr[...] = jnp.broadcast_to(total, base_scr.shape)
    cnt_ref[...] = jnp.broadcast_to(total, cnt_ref.shape)
    meta = jnp.zeros_like(logits)
    for k, val in ((META_E1, i1), (META_E2, i2), (META_R1, r1), (META_R2, r2),
                   (META_G1, 1.0 / den), (META_G2, e2 / den)):
        meta = jnp.where(lane == float(k), val, meta)
    meta_ref[...] = meta


def _dsw_router(h, o, wo, g, wr, tm=512):
    T = h.shape[0]
    return pl.pallas_call(
        _dsw_router_kernel,
        grid=(T // tm,),
        in_specs=[
            pl.BlockSpec((tm, D_MODEL), lambda i: (i, 0)),
            pl.BlockSpec((tm, D_MODEL), lambda i: (i, 0)),
            _resident(wo.shape), _resident(g.shape), _resident(wr.shape),
        ],
        out_specs=[
            pl.BlockSpec((tm, D_MODEL), lambda i: (i, 0)),
            pl.BlockSpec((tm, D_MODEL), lambda i: (i, 0)),
            pl.BlockSpec((tm, LANES), lambda i: (i, 0)),
            pl.BlockSpec((8, LANES), lambda i: (0, 0)),
        ],
        out_shape=[
            jax.ShapeDtypeStruct((T, D_MODEL), F32),
            jax.ShapeDtypeStruct((T, D_MODEL), F32),
            jax.ShapeDtypeStruct((T, LANES), F32),
            jax.ShapeDtypeStruct((8, LANES), F32),
        ],
        scratch_shapes=[pltpu.VMEM((8, LANES), F32)],
        compiler_params=_cparams(("arbitrary",)),
        name="dsw_router",
    )(h, o, wo, g, wr)


MOE_ROWS = 512


def _row_copy(src, src_row, dst, dst_row, sem):
    return pltpu.make_async_copy(src.at[pl.ds(src_row, 1)], dst.at[pl.ds(dst_row, 1)], sem)


def _moe_dispatch_kernel(d1_ref, d2_ref, hn_hbm, xs_in_hbm, xs_hbm, sem, *, tm):
    del xs_in_hbm
    base = pl.program_id(0) * tm

    def body(r, c):
        _row_copy(hn_hbm, base + r, xs_hbm, d1_ref[0, r], sem).start()
        _row_copy(hn_hbm, base + r, xs_hbm, d2_ref[0, r], sem).start()
        return c

    lax.fori_loop(0, tm, body, 0, unroll=8)
    pltpu.make_async_copy(hn_hbm.at[pl.ds(0, 2 * tm)], xs_hbm.at[pl.ds(0, 2 * tm)], sem).wait()


def _moe_dispatch(d1, d2, hn, n_rows, tm=512):
    T = hn.shape[0]
    smem = lambda: pl.BlockSpec((None, 1, tm), lambda i: (i, 0, 0), memory_space=pltpu.SMEM)
    return pl.pallas_call(
        functools.partial(_moe_dispatch_kernel, tm=tm),
        grid=(T // tm,),
        in_specs=[smem(), smem(), pl.BlockSpec(memory_space=pl.ANY), pl.BlockSpec(memory_space=pl.ANY)],
        out_specs=pl.BlockSpec(memory_space=pl.ANY),
        out_shape=jax.ShapeDtypeStruct((n_rows, D_MODEL), F32),
        scratch_shapes=[pltpu.SemaphoreType.DMA],
        input_output_aliases={3: 0},
        compiler_params=_cparams(("arbitrary",)),
        name="moe_dispatch",
    )(d1.reshape(T // tm, 1, tm), d2.reshape(T // tm, 1, tm), hn, jnp.zeros((n_rows, D_MODEL), F32))


def _moe_group_kernel(te_ref, nu_ref, xs_ref, wg_ref, wu_ref, wd_ref, ys_ref, xb_scr, acc_scr):
    j = pl.program_id(0)
    f = pl.program_id(1)
    used = j < nu_ref[0]

    @pl.when(used & (f == 0))
    def _():
        xb_scr[...] = xs_ref[...].astype(BF16)
        acc_scr[...] = jnp.zeros_like(acc_scr)

    @pl.when(used)
    def _():
        acc_scr[...] += _swiglu_acc(xb_scr[...], wg_ref, wu_ref, wd_ref, wg_ref.shape[-1])

    @pl.when(f == pl.num_programs(1) - 1)
    def _():
        ys_ref[...] = jnp.where(used, acc_scr[...], 0.0)


def _moe_group(tile_expert, n_used, xs, wg, wu, wd, tf=512):
    n_rows = xs.shape[0]
    R = MOE_ROWS
    E, _, Fe = wg.shape
    nf = Fe // tf

    def w_col(j, f, te, nu):
        return jnp.where(j < nu[0], f, nf - 1)

    return pl.pallas_call(
        _moe_group_kernel,
        grid_spec=pltpu.PrefetchScalarGridSpec(
            num_scalar_prefetch=2,
            grid=(n_rows // R, nf),
            in_specs=[
                pl.BlockSpec((R, D_MODEL), lambda j, f, te, nu: (jnp.minimum(j, nu[0] - 1), 0)),
                pl.BlockSpec((None, D_MODEL, tf), lambda j, f, te, nu: (te[j], 0, w_col(j, f, te, nu))),
                pl.BlockSpec((None, D_MODEL, tf), lambda j, f, te, nu: (te[j], 0, w_col(j, f, te, nu))),
                pl.BlockSpec((None, tf, D_MODEL), lambda j, f, te, nu: (te[j], w_col(j, f, te, nu), 0)),
            ],
            out_specs=pl.BlockSpec((R, D_MODEL), lambda j, f, te, nu: (j, 0)),
            scratch_shapes=[pltpu.VMEM((R, D_MODEL), BF16), pltpu.VMEM((R, D_MODEL), F32)],
        ),
        out_shape=jax.ShapeDtypeStruct((n_rows, D_MODEL), F32),
        compiler_params=_cparams(("arbitrary", "arbitrary")),
        name="moe_group",
    )(tile_expert, n_used, xs, wg, wu, wd)


def _moe_combine_kernel(d1_ref, d2_ref, meta_ref, h3_ref, gf_ref, ys_hbm, out_ref, y1_scr, y2_scr, sem, *, tm):
    def body(r, c):
        _row_copy(ys_hbm, d1_ref[0, r], y1_scr, r, sem).start()
        _row_copy(ys_hbm, d2_ref[0, r], y2_scr, r, sem).start()
        return c

    lax.fori_loop(0, tm, body, 0, unroll=8)
    pltpu.make_async_copy(ys_hbm.at[pl.ds(0, tm)], y1_scr, sem).wait()
    pltpu.make_async_copy(ys_hbm.at[pl.ds(0, tm)], y2_scr, sem).wait()
    meta = meta_ref[...]
    lane = lax.broadcasted_iota(jnp.int32, meta.shape, 1)
    g1 = jnp.sum(jnp.where(lane == META_G1, meta, 0.0), axis=-1, keepdims=True)
    g2 = jnp.sum(jnp.where(lane == META_G2, meta, 0.0), axis=-1, keepdims=True)
    out_ref[...] = _rms(h3_ref[...] + g1 * y1_scr[...] + g2 * y2_scr[...], gf_ref[...])


def _moe_combine(d1, d2, meta, h3, gf, ys, tm=256):
    T = h3.shape[0]
    smem = lambda: pl.BlockSpec((None, 1, tm), lambda i: (i, 0, 0), memory_space=pltpu.SMEM)
    return pl.pallas_call(
        functools.partial(_moe_combine_kernel, tm=tm),
        grid=(T // tm,),
        in_specs=[
            smem(), smem(),
            pl.BlockSpec((tm, LANES), lambda i: (i, 0)),
            pl.BlockSpec((tm, D_MODEL), lambda i: (i, 0)),
            pl.BlockSpec((1, D_MODEL), lambda i: (0, 0)),
            pl.BlockSpec(memory_space=pl.ANY),
        ],
        out_specs=pl.BlockSpec((tm, D_MODEL), lambda i: (i, 0)),
        out_shape=jax.ShapeDtypeStruct((T, D_MODEL), F32),
        scratch_shapes=[pltpu.VMEM((tm, D_MODEL), F32), pltpu.VMEM((tm, D_MODEL), F32),
                        pltpu.SemaphoreType.DMA],
        compiler_params=_cparams(("arbitrary",)),
        name="moe_combine",
    )(d1.reshape(T // tm, 1, tm), d2.reshape(T // tm, 1, tm), meta, h3, gf, ys)


def _moe_plan(meta, counts, n_tiles):
    R = MOE_ROWS
    cnt = counts[0, :N_EXPERTS].astype(jnp.int32)
    tiles = (cnt + R - 1) // R
    tile_end = jnp.cumsum(tiles)
    row_start = (tile_end - tiles) * R
    e1 = meta[:, META_E1].astype(jnp.int32)
    e2 = meta[:, META_E2].astype(jnp.int32)
    d1 = row_start[e1] + meta[:, META_R1].astype(jnp.int32)
    d2 = row_start[e2] + meta[:, META_R2].astype(jnp.int32)
    n_used = tile_end[-1:]
    j = jnp.minimum(jnp.arange(n_tiles, dtype=jnp.int32), n_used - 1)
    tile_expert = jnp.sum(j[:, None] >= tile_end[None, :], axis=1).astype(jnp.int32)
    return d1, d2, tile_expert, n_used.astype(jnp.int32)


def _take_cols(w, src):
    src = np.asarray(src)
    cols = jnp.take(w, jnp.asarray(np.maximum(src, 0)), axis=1)
    return jnp.where(jnp.asarray(src >= 0)[None, :], cols, 0.0)


def _mla_lane_of_dim():
    lane = np.zeros(MLA_QK, np.int64)
    for dd in range(MLA_NOPE):
        lane[dd] = 16 + dd if dd < 48 else 80 + (dd - 48)
    half = MLA_ROPE // 2
    for i in range(MLA_ROPE):
        lane[MLA_NOPE + i] = i if i < half else 64 + (i - half)
    return lane


def _dsw_lane_of(hh, dd):
    half = DSW_ROT // 2
    if dd < half:
        return hh * half + dd
    if dd < DSW_ROT:
        return 64 + hh * half + (dd - half)
    return (16 if hh == 0 else 80) + (dd - DSW_ROT)


def _rope_lane_tables(half, per_lane_index):
    inv_freq = jnp.power(jnp.float32(ROPE_THETA), -jnp.arange(half, dtype=F32) * (2.0 / (2 * half)))
    idx = np.zeros(LANES, np.int64)
    on = np.zeros(LANES, bool)
    sign = np.zeros(LANES, np.float32)
    for l in range(LANES):
        i = per_lane_index(l)
        if i is not None:
            idx[l], on[l] = i, True
            sign[l] = -1.0 if l < 64 else 1.0
    freq = jnp.where(jnp.asarray(on), jnp.take(inv_freq, jnp.asarray(idx)), 0.0)
    return freq.reshape(1, LANES), jnp.asarray(sign).reshape(1, LANES)


def kernel(x, positions, norm_attn, norm_ffn, mla_w_down, mla_q_norm, mla_w_uq, mla_kv_norm, mla_w_ukv,
           mla_w_o, dsw_kv_norm, dsw_w_kv, dsw_w_q, dsw_w_o, ffn_w_gate, ffn_w_up, ffn_w_down, moe_router,
           moe_w_gate, moe_w_up, moe_w_down, final_norm):
    B, S, D = x.shape
    T = B * S
    pos = positions.reshape(B, S, 1)

    lane_of = _mla_lane_of_dim()
    H = MLA_HEADS
    src_dn = np.full(MLA_Q_LORA + MLA_KV_LORA + LANES, -1, np.int64)
    src_dn[:MLA_Q_LORA + MLA_KV_LORA] = np.arange(MLA_Q_LORA + MLA_KV_LORA)
    src_uq = np.full(H * LANES, -1, np.int64)
    src_k = np.full(H * LANES, -1, np.int64)
    src_v = np.zeros(H * MLA_V, np.int64)
    for i in range(MLA_ROPE):
        src_dn[MLA_Q_LORA + MLA_KV_LORA + lane_of[MLA_NOPE + i]] = MLA_Q_LORA + MLA_KV_LORA + i
    for h in range(H):
        for dd in range(MLA_QK):
            src_uq[h * LANES + lane_of[dd]] = h * MLA_QK + dd
        for dd in range(MLA_NOPE):
            src_k[h * LANES + lane_of[dd]] = h * (MLA_NOPE + MLA_V) + dd
        for dd in range(MLA_V):
            src_v[h * MLA_V + dd] = h * (MLA_NOPE + MLA_V) + MLA_NOPE + dd
    wdn = _take_cols(mla_w_down[0], src_dn).astype(BF16)
    wuq = _take_cols(mla_w_uq[0], src_uq).astype(BF16)
    wk = _take_cols(mla_w_ukv[0], src_k).astype(BF16)
    wv = _take_cols(mla_w_ukv[0], src_v).astype(BF16)
    half = MLA_ROPE // 2
    mla_freq, mla_sign = _rope_lane_tables(
        half, lambda l: l if l < half else (l - 64 if 64 <= l < 64 + half else None))

    q, k, v = _mla_proj(x, pos, norm_attn[0:1], wdn, mla_q_norm[0:1], mla_kv_norm[0:1], wuq, wk, wv,
                        mla_freq, mla_sign)
    o = _mla_attn(q, k, v)
    h = _attn_ffn(x.reshape(T, D), o.reshape(T, D), mla_w_o[0].astype(BF16), norm_ffn[0:1],
                  ffn_w_gate[0].astype(BF16), ffn_w_up[0].astype(BF16), ffn_w_down[0].astype(BF16))

    src_pair = np.zeros(DSW_WIDTH, np.int64)
    for j in range(DSW_HEADS // 2):
        for hh in range(2):
            for dd in range(DSW_HD):
                src_pair[j * LANES + _dsw_lane_of(hh, dd)] = (2 * j + hh) * DSW_HD + dd
    wq_b, wk_b, wv_b = [], [], []
    for g in range(N_BR):
        wq_b.append(_take_cols(dsw_w_q[0], g * DSW_WIDTH + src_pair))
        wk_b.append(_take_cols(dsw_w_kv, g * DSW_WIDTH + src_pair))
        wv_b.append(dsw_w_kv[:, (N_BR + g) * DSW_WIDTH:(N_BR + g + 1) * DSW_WIDTH])
    wq_b = jnp.stack(wq_b).astype(BF16)
    wk_b = jnp.stack(wk_b).astype(BF16)
    wv_b = jnp.stack(wv_b).astype(BF16)
    half = DSW_ROT // 2
    dsw_freq, dsw_sign = _rope_lane_tables(
        half, lambda l: l % half if l < 2 * half else ((l - 64) % half if 64 <= l < 64 + 2 * half else None))

    qkv = _dsw_proj(h.reshape(B, S, D), pos, norm_attn[1:2], dsw_kv_norm.reshape(1, D), wq_b, wk_b, wv_b,
                    dsw_freq, dsw_sign)
    o = _dsw_attn(qkv)
    wr = jnp.pad(moe_router[0], ((0, 0), (0, LANES - N_EXPERTS)))
    h3, hn, meta, counts = _dsw_router(h, o.reshape(T, D), dsw_w_o[0].astype(BF16), norm_ffn[1:2], wr)

    n_tiles = (2 * T) // MOE_ROWS + N_EXPERTS
    d1, d2, tile_expert, n_used = _moe_plan(meta, counts, n_tiles)
    xs = _moe_dispatch(d1, d2, hn, n_tiles * MOE_ROWS)
    ys = _moe_group(tile_expert, n_used, xs, moe_w_gate[0].astype(BF16), moe_w_up[0].astype(BF16),
                    moe_w_down[0].astype(BF16))
    out = _moe_combine(d1, d2, meta, h3, final_norm.reshape(1, D), ys)
    return out.reshape(B, S, D)
```

```python
import functools
import math

import numpy as np
import jax
import jax.numpy as jnp
from jax import lax
from jax.experimental import pallas as pl
from jax.experimental.pallas import tpu as pltpu

F32 = jnp.float32
BF16 = jnp.bfloat16

D_MODEL = 1024
RMS_EPS = 1e-6
ROPE_THETA = 500000.0
NEG_INF = -1e30
LANES = 128

MLA_HEADS = 16
MLA_NOPE = 64
MLA_ROPE = 32
MLA_V = 64
MLA_QK = MLA_NOPE + MLA_ROPE
MLA_Q_LORA = 384
MLA_KV_LORA = 256

DSW_BRANCHES = ((128, 1), (512, 4), (2048, 16))
N_BR = 3
DSW_HEADS = 16
DSW_HD = 64
DSW_ROT = 16
DSW_WIDTH = DSW_HEADS * DSW_HD
DSW_BLOCK = 128

FFN_DIM = 2816
N_EXPERTS = 8
EXPERT_DIM = 3584

VMEM_LIMIT = 56 * 1024 * 1024


def _cparams(sem):
    return pltpu.CompilerParams(dimension_semantics=sem, vmem_limit_bytes=VMEM_LIMIT)


def _resident(shape):
    nd = len(shape)
    return pl.BlockSpec(shape, lambda *_: (0,) * nd, pipeline_mode=pl.Buffered(1))


def _rms(x, g):
    return x * lax.rsqrt(jnp.mean(x * x, axis=-1, keepdims=True) + RMS_EPS) * g


def _dot(a, b):
    return jnp.dot(a, b, preferred_element_type=F32)


def _dot_nt(a, b):
    return lax.dot_general(a, b, (((1,), (1,)), ((), ())), preferred_element_type=F32)


def _rope(t, cos, sin_signed):
    return t * cos + pltpu.roll(t, 64, 1) * sin_signed


def _mla_proj_kernel(x_ref, pos_ref, g_ref, wdn_ref, qn_ref, kvn_ref, wuq_ref, wk_ref, wv_ref,
                     freq_ref, sign_ref, q_ref, k_ref, v_ref):
    xn = _rms(x_ref[...], g_ref[...]).astype(BF16)
    down = _dot(xn, wdn_ref[...])
    cq = _rms(down[:, :MLA_Q_LORA], qn_ref[...]).astype(BF16)
    ckv = _rms(down[:, MLA_Q_LORA:MLA_Q_LORA + MLA_KV_LORA], kvn_ref[...]).astype(BF16)
    kr = down[:, MLA_Q_LORA + MLA_KV_LORA:]
    ang = pos_ref[...].astype(F32) * freq_ref[...]
    cos = jnp.cos(ang)
    sin = jnp.sin(ang) * sign_ref[...]
    kr = _rope(kr, cos, sin)
    scale = 1.0 / math.sqrt(MLA_QK)
    cos_q = cos * scale
    sin_q = sin * scale
    q = _dot(cq, wuq_ref[...])
    for h in range(MLA_HEADS):
        q_ref[h] = _rope(q[:, h * LANES:(h + 1) * LANES], cos_q, sin_q).astype(BF16)
    kp = _dot(ckv, wk_ref[...])
    for h in range(MLA_HEADS):
        k_ref[h] = (kp[:, h * LANES:(h + 1) * LANES] + kr).astype(BF16)
    v = _dot(ckv, wv_ref[...])
    for j in range(MLA_HEADS // 2):
        v_ref[j] = v[:, j * LANES:(j + 1) * LANES].astype(BF16)


def _mla_proj(x, pos, g, wdn, qn, kvn, wuq, wk, wv, freq, sign, tm=512):
    B, S, _ = x.shape
    H = MLA_HEADS
    return pl.pallas_call(
        _mla_proj_kernel,
        grid=(B, S // tm),
        in_specs=[
            pl.BlockSpec((None, tm, D_MODEL), lambda b, i: (b, i, 0)),
            pl.BlockSpec((None, tm, 1), lambda b, i: (b, i, 0)),
            _resident(g.shape), _resident(wdn.shape), _resident(qn.shape), _resident(kvn.shape),
            _resident(wuq.shape), _resident(wk.shape), _resident(wv.shape),
            _resident(freq.shape), _resident(sign.shape),
        ],
        out_specs=[
            pl.BlockSpec((None, H, tm, LANES), lambda b, i: (b, 0, i, 0)),
            pl.BlockSpec((None, H, tm, LANES), lambda b, i: (b, 0, i, 0)),
            pl.BlockSpec((None, H // 2, tm, LANES), lambda b, i: (b, 0, i, 0)),
        ],
        out_shape=[
            jax.ShapeDtypeStruct((B, H, S, LANES), BF16),
            jax.ShapeDtypeStruct((B, H, S, LANES), BF16),
            jax.ShapeDtypeStruct((B, H // 2, S, LANES), BF16),
        ],
        compiler_params=_cparams(("parallel", "parallel")),
        name="mla_proj",
    )(x, pos, g, wdn, qn, kvn, wuq, wk, wv, freq, sign)


def _mla_attn_kernel(q_ref, k_ref, v_ref, o_ref, m_scr, l_scr, acc_scr, *, t):
    i = pl.program_id(2)
    m_scr[...] = jnp.full(m_scr.shape, NEG_INF, F32)
    l_scr[...] = jnp.zeros(l_scr.shape, F32)
    acc_scr[...] = jnp.zeros(acc_scr.shape, F32)

    def chunk(c, diagonal):
        off = pl.multiple_of(c * t, t)
        v = v_ref[pl.ds(off, t), :]
        for hh in range(2):
            s = _dot_nt(q_ref[hh], k_ref[hh, pl.ds(off, t), :])
            if diagonal:
                row = lax.broadcasted_iota(jnp.int32, (t, t), 0)
                col = lax.broadcasted_iota(jnp.int32, (t, t), 1)
                s = jnp.where(col <= row, s, NEG_INF)
            m_prev = m_scr[hh]
            m_new = jnp.maximum(m_prev, jnp.max(s, axis=-1, keepdims=True))
            alpha = jnp.exp(m_prev - m_new)
            ps = [jnp.exp(s[:, j * LANES:(j + 1) * LANES] - m_new) for j in range(t // LANES)]
            psum = ps[0]
            for pj in ps[1:]:
                psum = psum + pj
            l_scr[hh] = alpha * l_scr[hh] + jnp.sum(psum, axis=-1, keepdims=True)
            p = jnp.concatenate(ps, axis=1).astype(BF16)
            acc_scr[hh] = alpha * acc_scr[hh] + _dot(p, v)
            m_scr[hh] = m_new

    def body(c, carry):
        chunk(c, False)
        return carry

    lax.fori_loop(0, i, body, 0)
    chunk(i, True)
    lane = lax.broadcasted_iota(jnp.int32, (t, LANES), 1)
    o_ref[...] = jnp.where(lane < MLA_V, acc_scr[0] / l_scr[0], acc_scr[1] / l_scr[1]).astype(BF16)


def _mla_attn(q, k, v, t=512):
    B, H, S, _ = q.shape
    return pl.pallas_call(
        functools.partial(_mla_attn_kernel, t=t),
        grid=(B, H // 2, S // t),
        in_specs=[
            pl.BlockSpec((None, 2, t, LANES), lambda b, j, i: (b, j, i, 0)),
            pl.BlockSpec((None, 2, S, LANES), lambda b, j, i: (b, j, 0, 0)),
            pl.BlockSpec((None, None, S, LANES), lambda b, j, i: (b, j, 0, 0)),
        ],
        out_specs=pl.BlockSpec((None, t, LANES), lambda b, j, i: (b, i, j)),
        out_shape=jax.ShapeDtypeStruct((B, S, (H // 2) * LANES), BF16),
        scratch_shapes=[pltpu.VMEM((2, t, LANES), F32)] * 3,
        compiler_params=_cparams(("parallel", "parallel", "parallel")),
        name="mla_attn",
    )(q, k, v)


def _swiglu_acc(hn, wg_ref, wu_ref, wd_ref, chunk, scale=None):
    acc = None
    n = wg_ref.shape[-1]
    for c0 in range(0, n, chunk):
        g = _dot(hn, wg_ref[:, c0:c0 + chunk])
        u = _dot(hn, wu_ref[:, c0:c0 + chunk])
        a = g * jax.nn.sigmoid(g) * u
        if scale is not None:
            a = a * scale
        part = _dot(a.astype(BF16), wd_ref[c0:c0 + chunk, :])
        acc = part if acc is None else acc + part
    return acc


def _attn_ffn_kernel(x_ref, o_ref, wo_ref, g_ref, wg_ref, wu_ref, wd_ref, h_ref):
    h1 = x_ref[...] + _dot(o_ref[...], wo_ref[...])
    hn = _rms(h1, g_ref[...]).astype(BF16)
    h_ref[...] = h1 + _swiglu_acc(hn, wg_ref, wu_ref, wd_ref, FFN_DIM // 2)


def _attn_ffn(x, o, wo, g, wg, wu, wd, tm=512):
    T = x.shape[0]
    return pl.pallas_call(
        _attn_ffn_kernel,
        grid=(T // tm,),
        in_specs=[
            pl.BlockSpec((tm, D_MODEL), lambda i: (i, 0)),
            pl.BlockSpec((tm, D_MODEL), lambda i: (i, 0)),
            _resident(wo.shape), _resident(g.shape), _resident(wg.shape), _resident(wu.shape),
            _resident(wd.shape),
        ],
        out_specs=pl.BlockSpec((tm, D_MODEL), lambda i: (i, 0)),
        out_shape=jax.ShapeDtypeStruct((T, D_MODEL), F32),
        compiler_params=_cparams(("parallel",)),
        name="attn_ffn",
    )(x, o, wo, g, wg, wu, wd)


def _dsw_proj_kernel(h_ref, pos_ref, gq_ref, gkv_ref, wq_ref, wk_ref, wv_ref, freq_ref, sign_ref,
                     q1, k1, v1, q2, k2, v2, q3, k3, v3,
                     xq_scr, xkv_scr, cos_scr, sin_scr, pq_scr, pkv_scr, pcos_scr, psin_scr, *, tm):
    h = h_ref[...]
    r = lax.rsqrt(jnp.mean(h * h, axis=-1, keepdims=True) + RMS_EPS)
    hr = h * r
    nblk = D_MODEL // LANES
    xq_f = hr * gq_ref[...]
    xkv_f = hr * gkv_ref[...]
    for cb in range(nblk):
        xq_scr[cb] = xq_f[:, cb * LANES:(cb + 1) * LANES]
        xkv_scr[cb] = xkv_f[:, cb * LANES:(cb + 1) * LANES]
    ang = pos_ref[...].astype(F32) * freq_ref[...]
    cos_scr[...] = jnp.cos(ang)
    sin_scr[...] = jnp.sin(ang) * sign_ref[...]
    scale = 1.0 / math.sqrt(DSW_HD)
    outs = ((q1, k1, v1), (q2, k2, v2), (q3, k3, v3))
    for g, (_, d) in enumerate(DSW_BRANCHES):
        n = tm // d
        q_out, k_out, v_out = outs[g]
        if d == 1:
            xq = xq_f.astype(BF16)
            xkv = xkv_f.astype(BF16)
            cos = cos_scr[...]
            sin = sin_scr[...]
        else:
            for rr in range(d):
                rows = slice(rr * n, (rr + 1) * n)
                for cb in range(nblk):
                    cols = slice(cb * LANES, (cb + 1) * LANES)
                    pq_scr[rows, cols] = xq_scr[cb, pl.ds(rr, n, stride=d), :].astype(BF16)
                    pkv_scr[rows, cols] = xkv_scr[cb, pl.ds(rr, n, stride=d), :].astype(BF16)
                pcos_scr[rows, :] = cos_scr[pl.ds(rr, n, stride=d), :]
                psin_scr[rows, :] = sin_scr[pl.ds(rr, n, stride=d), :]
            xq = pq_scr[...]
            xkv = pkv_scr[...]
            cos = pcos_scr[...]
            sin = psin_scr[...]
        q = _dot(xq, wq_ref[g])
        k = _dot(xkv, wk_ref[g])
        v = _dot(xkv, wv_ref[g])
        cos_q = cos * scale
        sin_q = sin * scale
        for j in range(DSW_HEADS // 2):
            sl = slice(j * LANES, (j + 1) * LANES)
            qj = _rope(q[:, sl], cos_q, sin_q).astype(BF16)
            kj = _rope(k[:, sl], cos, sin).astype(BF16)
            vj = v[:, sl].astype(BF16)
            for rr in range(d):
                q_out[j, rr] = qj[rr * n:(rr + 1) * n]
                k_out[j, rr] = kj[rr * n:(rr + 1) * n]
                v_out[j, rr] = vj[rr * n:(rr + 1) * n]


def _dsw_proj(h, pos, gq, gkv, wq, wk, wv, freq, sign, tm=256):
    B, S, _ = h.shape
    P = DSW_HEADS // 2
    out_specs, out_shape = [], []
    for _, d in DSW_BRANCHES:
        for _ in range(3):
            out_specs.append(pl.BlockSpec((None, P, d, tm // d, LANES), lambda b, i: (b, 0, 0, i, 0)))
            out_shape.append(jax.ShapeDtypeStruct((B, P, d, S // d, LANES), BF16))
    return pl.pallas_call(
        functools.partial(_dsw_proj_kernel, tm=tm),
        grid=(B, S // tm),
        in_specs=[
            pl.BlockSpec((None, tm, D_MODEL), lambda b, i: (b, i, 0)),
            pl.BlockSpec((None, tm, 1), lambda b, i: (b, i, 0)),
            _resident(gq.shape), _resident(gkv.shape), _resident(wq.shape), _resident(wk.shape),
            _resident(wv.shape), _resident(freq.shape), _resident(sign.shape),
        ],
        out_specs=out_specs,
        out_shape=out_shape,
        scratch_shapes=[
            pltpu.VMEM((D_MODEL // LANES, tm, LANES), F32), pltpu.VMEM((D_MODEL // LANES, tm, LANES), F32),
            pltpu.VMEM((tm, LANES), F32), pltpu.VMEM((tm, LANES), F32),
            pltpu.VMEM((tm, D_MODEL), BF16), pltpu.VMEM((tm, D_MODEL), BF16),
            pltpu.VMEM((tm, LANES), F32), pltpu.VMEM((tm, LANES), F32),
        ],
        compiler_params=_cparams(("parallel", "parallel")),
        name="dsw_proj",
    )(h, pos, gq, gkv, wq, wk, wv, freq, sign)


def _unroll_of(n):
    for u in (4, 5, 3, 2):
        if n % u == 0:
            return u
    return 1


def _dsw_attn_kernel(q1, k1, v1, q2, k2, v2, q3, k3, v3, o_ref,
                     o1_scr, l1_scr, o2_scr, l2_scr, o3_scr, l3_scr, bo_scr, bl_scr, *, S):
    blk = DSW_BLOCK
    lane = lax.broadcasted_iota(jnp.int32, (blk, LANES), 1)
    mask_a = (lane < 8) | ((lane >= 16) & (lane < 72))
    first_half = lane < DSW_HD
    row1 = lax.broadcasted_iota(jnp.int32, (2 * blk, blk), 0)
    col1 = lax.broadcasted_iota(jnp.int32, (2 * blk, blk), 1)
    row1 = jnp.where(row1 >= blk, row1 - blk, row1)
    valid_first = col1 <= row1
    row2 = lax.broadcasted_iota(jnp.int32, (2 * blk, 2 * blk), 0)
    col2 = lax.broadcasted_iota(jnp.int32, (2 * blk, 2 * blk), 1)
    row2 = jnp.where(row2 >= blk, row2 - blk, row2)
    valid_next = ((col2 < blk) & (col2 >= row2)) | ((col2 >= blk) & (col2 - blk <= row2))

    def attend(qb, kk, vv, valid):
        qf = qb.astype(F32)
        q2 = jnp.concatenate([jnp.where(mask_a, qf, 0.0), jnp.where(mask_a, 0.0, qf)], axis=0).astype(BF16)
        s = jnp.where(valid, _dot_nt(q2, kk), NEG_INF)
        m = jnp.max(s, axis=-1, keepdims=True)
        p = jnp.exp(s - m)
        l = jnp.sum(p, axis=-1, keepdims=True)
        o2 = _dot(p.astype(BF16), vv) / l
        lse = m + jnp.log(l)
        return (jnp.where(first_half, o2[:blk], o2[blk:]),
                jnp.where(first_half, lse[:blk], lse[blk:]))

    def first_block(q_ref, k_ref, v_ref, off, o_dst, l_dst):
        sl = pl.ds(pl.multiple_of(off, blk), blk)
        o, l = attend(q_ref[sl, :], k_ref[sl, :], v_ref[sl, :], valid_first)
        o_dst[sl, :] = o
        l_dst[sl, :] = l

    def next_block(q_ref, k_ref, v_ref, off, o_dst, l_dst):
        sl = pl.ds(pl.multiple_of(off, blk), blk)
        win = pl.ds(pl.multiple_of(off - blk, blk), 2 * blk)
        o, l = attend(q_ref[sl, :], k_ref[win, :], v_ref[win, :], valid_next)
        o_dst[sl, :] = o
        l_dst[sl, :] = l

    branches = ((q1, k1, v1, o1_scr, l1_scr, o1_scr, l1_scr),
                (q2, k2, v2, bo_scr, bl_scr, o2_scr, l2_scr),
                (q3, k3, v3, bo_scr, bl_scr, o3_scr, l3_scr))
    for (_, d), (q_ref, k_ref, v_ref, o_dst, l_dst, o_tok, l_tok) in zip(DSW_BRANCHES, branches):
        L = S // d
        nb = L // blk

        def first_body(rr, c, q_ref=q_ref, k_ref=k_ref, v_ref=v_ref, o_dst=o_dst, l_dst=l_dst, L=L):
            first_block(q_ref, k_ref, v_ref, rr * L, o_dst, l_dst)
            return c

        lax.fori_loop(0, d, first_body, 0, unroll=_unroll_of(d))
        if nb > 1:
            def next_body(idx, c, q_ref=q_ref, k_ref=k_ref, v_ref=v_ref, o_dst=o_dst, l_dst=l_dst,
                          L=L, nb=nb):
                rr = idx // (nb - 1)
                ii = idx - rr * (nb - 1) + 1
                next_block(q_ref, k_ref, v_ref, rr * L + ii * blk, o_dst, l_dst)
                return c

            lax.fori_loop(0, d * (nb - 1), next_body, 0, unroll=_unroll_of(d * (nb - 1)))
        if d > 1:
            for rr in range(d):
                o_tok[pl.ds(rr, L, stride=d), :] = o_dst[rr * L:(rr + 1) * L, :]
                l_tok[pl.ds(rr, L, stride=d), :] = l_dst[rr * L:(rr + 1) * L, :]

    ch = 256
    for c0 in range(0, S, ch):
        sl = slice(c0, c0 + ch)
        la, lb, lc = l1_scr[sl, :], l2_scr[sl, :], l3_scr[sl, :]
        m = jnp.maximum(jnp.maximum(la, lb), lc)
        wa, wb, wc = jnp.exp(la - m), jnp.exp(lb - m), jnp.exp(lc - m)
        num = wa * o1_scr[sl, :] + wb * o2_scr[sl, :] + wc * o3_scr[sl, :]
        o_ref[sl, :] = (num / (wa + wb + wc)).astype(BF16)


def _dsw_attn(qkv):
    B, P = qkv[0].shape[0], qkv[0].shape[1]
    S = qkv[0].shape[2] * qkv[0].shape[3]
    flat = [a.reshape(B, P, S, LANES) for a in qkv]
    spec = pl.BlockSpec((None, None, S, LANES), lambda b, j: (b, j, 0, 0))
    return pl.pallas_call(
        functools.partial(_dsw_attn_kernel, S=S),
        grid=(B, P),
        in_specs=[spec] * 9,
        out_specs=pl.BlockSpec((None, S, LANES), lambda b, j: (b, 0, j)),
        out_shape=jax.ShapeDtypeStruct((B, S, P * LANES), BF16),
        scratch_shapes=[pltpu.VMEM((S, LANES), F32)] * 8,
        compiler_params=_cparams(("parallel", "parallel")),
        name="dsw_attn",
    )(*flat)


META_E1, META_E2, META_R1, META_R2, META_G1, META_G2 = range(6)


def _dsw_router_kernel(h_ref, o_ref, wo_ref, g_ref, wr_ref, h3_ref, hn_ref, meta_ref, cnt_ref, base_scr):
    @pl.when(pl.program_id(0) == 0)
    def _():
        base_scr[...] = jnp.zeros_like(base_scr)

    h3 = h_ref[...] + _dot(o_ref[...], wo_ref[...])
    h3_ref[...] = h3
    hn = _rms(h3, g_ref[...])
    hn_ref[...] = hn
    logits = jnp.dot(hn, wr_ref[...], preferred_element_type=F32, precision=lax.Precision.HIGHEST)
    tm = logits.shape[0]
    lane = lax.broadcasted_iota(jnp.int32, logits.shape, 1).astype(F32)
    logits = jnp.where(lane < N_EXPERTS, logits, -jnp.inf)
    v1 = jnp.max(logits, axis=-1, keepdims=True)
    i1 = jnp.min(jnp.where(logits == v1, lane, float(LANES)), axis=-1, keepdims=True)
    rest = jnp.where(lane == i1, -jnp.inf, logits)
    v2 = jnp.max(rest, axis=-1, keepdims=True)
    i2 = jnp.min(jnp.where(rest == v2, lane, float(LANES)), axis=-1, keepdims=True)
    e2 = jnp.exp(v2 - v1)
    den = 1.0 + e2
    onehot = jnp.where((lane == i1) | (lane == i2), 1.0, 0.0)
    row = lax.broadcasted_iota(jnp.int32, (tm, tm), 0)
    col = lax.broadcasted_iota(jnp.int32, (tm, tm), 1)
    earlier = jnp.where(col < row, 1.0, 0.0).astype(BF16)
    before = _dot(earlier, onehot.astype(BF16)) + base_scr[0:1, :]
    r1 = jnp.sum(jnp.where(lane == i1, before, 0.0), axis=-1, keepdims=True)
    r2 = jnp.sum(jnp.where(lane == i2, before, 0.0), axis=-1, keepdims=True)
    total = base_scr[0:1, :] + jnp.sum(onehot, axis=0, keepdims=True)
    base_scr[...] = jnp.broadcast_to(total, base_scr.shape)
    cnt_ref[...] = jnp.broadcast_to(total, cnt_ref.shape)
    meta = jnp.zeros_like(logits)
    for k, val in ((META_E1, i1), (META_E2, i2), (META_R1, r1), (META_R2, r2),
                   (META_G1, 1.0 / den), (META_G2, e2 / den)):
        meta = jnp.where(lane == float(k), val, meta)
    meta_ref[...] = meta


def _dsw_router(h, o, wo, g, wr, tm=512):
    T = h.shape[0]
    return pl.pallas_call(
        _dsw_router_kernel,
        grid=(T // tm,),
        in_specs=[
            pl.BlockSpec((tm, D_MODEL), lambda i: (i, 0)),
            pl.BlockSpec((tm, D_MODEL), lambda i: (i, 0)),
            _resident(wo.shape), _resident(g.shape), _resident(wr.shape),
        ],
        out_specs=[
            pl.BlockSpec((tm, D_MODEL), lambda i: (i, 0)),
            pl.BlockSpec((tm, D_MODEL), lambda i: (i, 0)),
            pl.BlockSpec((tm, LANES), lambda i: (i, 0)),
            pl.BlockSpec((8, LANES), lambda i: (0, 0)),
        ],
        out_shape=[
            jax.ShapeDtypeStruct((T, D_MODEL), F32),
            jax.ShapeDtypeStruct((T, D_MODEL), F32),
            jax.ShapeDtypeStruct((T, LANES), F32),
            jax.ShapeDtypeStruct((8, LANES), F32),
        ],
        scratch_shapes=[pltpu.VMEM((8, LANES), F32)],
        compiler_params=_cparams(("arbitrary",)),
        name="dsw_router",
    )(h, o, wo, g, wr)


MOE_ROWS = 512


def _row_copy(src, src_row, dst, dst_row, sem):
    return pltpu.make_async_copy(src.at[pl.ds(src_row, 1)], dst.at[pl.ds(dst_row, 1)], sem)


def _moe_dispatch_kernel(d1_ref, d2_ref, hn_ref, xs_in_hbm, xs_hbm, sem, *, tm):
    del xs_in_hbm

    def body(r, c):
        _row_copy(hn_ref, r, xs_hbm, d1_ref[0, r], sem).start()
        _row_copy(hn_ref, r, xs_hbm, d2_ref[0, r], sem).start()
        return c

    lax.fori_loop(0, tm, body, 0, unroll=8)
    for _ in range(2):
        pltpu.make_async_copy(hn_ref, xs_hbm.at[pl.ds(0, tm)], sem).wait()


def _moe_dispatch(d1, d2, hn, n_rows, tm=512):
    T = hn.shape[0]
    smem = lambda: pl.BlockSpec((None, 1, tm), lambda i: (i, 0, 0), memory_space=pltpu.SMEM)
    return pl.pallas_call(
        functools.partial(_moe_dispatch_kernel, tm=tm),
        grid=(T // tm,),
        in_specs=[smem(), smem(), pl.BlockSpec((tm, D_MODEL), lambda i: (i, 0)),
                  pl.BlockSpec(memory_space=pl.ANY)],
        out_specs=pl.BlockSpec(memory_space=pl.ANY),
        out_shape=jax.ShapeDtypeStruct((n_rows, D_MODEL), F32),
        scratch_shapes=[pltpu.SemaphoreType.DMA],
        input_output_aliases={3: 0},
        compiler_params=_cparams(("arbitrary",)),
        name="moe_dispatch",
    )(d1.reshape(T // tm, 1, tm), d2.reshape(T // tm, 1, tm), hn, jnp.zeros((n_rows, D_MODEL), F32))


def _moe_group_kernel(te_ref, nu_ref, xs_ref, wg_ref, wu_ref, wd_ref, ys_ref, xb_scr, acc_scr):
    j = pl.program_id(0)
    f = pl.program_id(1)
    used = j < nu_ref[0]

    @pl.when(used & (f == 0))
    def _():
        xb_scr[...] = xs_ref[...].astype(BF16)
        acc_scr[...] = jnp.zeros_like(acc_scr)

    @pl.when(used)
    def _():
        acc_scr[...] += _swiglu_acc(xb_scr[...], wg_ref, wu_ref, wd_ref, wg_ref.shape[-1])

    @pl.when(f == pl.num_programs(1) - 1)
    def _():
        ys_ref[...] = jnp.where(used, acc_scr[...], 0.0)


def _moe_group(tile_expert, n_used, xs, wg, wu, wd, tf=512):
    n_rows = xs.shape[0]
    R = MOE_ROWS
    E, _, Fe = wg.shape
    nf = Fe // tf

    def w_col(j, f, te, nu):
        return jnp.where(j < nu[0], f, nf - 1)

    return pl.pallas_call(
        _moe_group_kernel,
        grid_spec=pltpu.PrefetchScalarGridSpec(
            num_scalar_prefetch=2,
            grid=(n_rows // R, nf),
            in_specs=[
                pl.BlockSpec((R, D_MODEL), lambda j, f, te, nu: (jnp.minimum(j, nu[0] - 1), 0)),
                pl.BlockSpec((None, D_MODEL, tf), lambda j, f, te, nu: (te[j], 0, w_col(j, f, te, nu))),
                pl.BlockSpec((None, D_MODEL, tf), lambda j, f, te, nu: (te[j], 0, w_col(j, f, te, nu))),
                pl.BlockSpec((None, tf, D_MODEL), lambda j, f, te, nu: (te[j], w_col(j, f, te, nu), 0)),
            ],
            out_specs=pl.BlockSpec((R, D_MODEL), lambda j, f, te, nu: (j, 0)),
            scratch_shapes=[pltpu.VMEM((R, D_MODEL), BF16), pltpu.VMEM((R, D_MODEL), F32)],
        ),
        out_shape=jax.ShapeDtypeStruct((n_rows, D_MODEL), F32),
        compiler_params=_cparams(("arbitrary", "arbitrary")),
        name="moe_group",
    )(tile_expert, n_used, xs, wg, wu, wd)


def _moe_combine_kernel(d1_ref, d2_ref, meta_ref, h3_ref, gf_ref, ys_hbm, out_ref, y1_scr, y2_scr, sem, *, tm):
    def body(r, c):
        _row_copy(ys_hbm, d1_ref[0, r], y1_scr, r, sem).start()
        _row_copy(ys_hbm, d2_ref[0, r], y2_scr, r, sem).start()
        return c

    lax.fori_loop(0, tm, body, 0, unroll=8)
    pltpu.make_async_copy(ys_hbm.at[pl.ds(0, tm)], y1_scr, sem).wait()
    pltpu.make_async_copy(ys_hbm.at[pl.ds(0, tm)], y2_scr, sem).wait()
    meta = meta_ref[...]
    lane = lax.broadcasted_iota(jnp.int32, meta.shape, 1)
    g1 = jnp.sum(jnp.where(lane == META_G1, meta, 0.0), axis=-1, keepdims=True)
    g2 = jnp.sum(jnp.where(lane == META_G2, meta, 0.0), axis=-1, keepdims=True)
    out_ref[...] = _rms(h3_ref[...] + g1 * y1_scr[...] + g2 * y2_scr[...], gf_ref[...])


def _moe_combine(d1, d2, meta, h3, gf, ys, tm=256):
    T = h3.shape[0]
    smem = lambda: pl.BlockSpec((None, 1, tm), lambda i: (i, 0, 0), memory_space=pltpu.SMEM)
    return pl.pallas_call(
        functools.partial(_moe_combine_kernel, tm=tm),
        grid=(T // tm,),
        in_specs=[
            smem(), smem(),
            pl.BlockSpec((tm, LANES), lambda i: (i, 0)),
            pl.BlockSpec((tm, D_MODEL), lambda i: (i, 0)),
            pl.BlockSpec((1, D_MODEL), lambda i: (0, 0)),
            pl.BlockSpec(memory_space=pl.ANY),
        ],
        out_specs=pl.BlockSpec((tm, D_MODEL), lambda i: (i, 0)),
        out_shape=jax.ShapeDtypeStruct((T, D_MODEL), F32),
        scratch_shapes=[pltpu.VMEM((tm, D_MODEL), F32), pltpu.VMEM((tm, D_MODEL), F32),
                        pltpu.SemaphoreType.DMA],
        compiler_params=_cparams(("arbitrary",)),
        name="moe_combine",
    )(d1.reshape(T // tm, 1, tm), d2.reshape(T // tm, 1, tm), meta, h3, gf, ys)


def _moe_plan(meta, counts, n_tiles):
    R = MOE_ROWS
    cnt = counts[0, :N_EXPERTS].astype(jnp.int32)
    tiles = (cnt + R - 1) // R
    tile_end = jnp.cumsum(tiles)
    row_start = (tile_end - tiles) * R
    e1 = meta[:, META_E1].astype(jnp.int32)
    e2 = meta[:, META_E2].astype(jnp.int32)
    d1 = row_start[e1] + meta[:, META_R1].astype(jnp.int32)
    d2 = row_start[e2] + meta[:, META_R2].astype(jnp.int32)
    n_used = tile_end[-1:]
    j = jnp.minimum(jnp.arange(n_tiles, dtype=jnp.int32), n_used - 1)
    tile_expert = jnp.sum(j[:, None] >= tile_end[None, :], axis=1).astype(jnp.int32)
    return d1, d2, tile_expert, n_used.astype(jnp.int32)


def _take_cols(w, src):
    src = np.asarray(src)
    cols = jnp.take(w, jnp.asarray(np.maximum(src, 0)), axis=1)
    return jnp.where(jnp.asarray(src >= 0)[None, :], cols, 0.0)


def _mla_lane_of_dim():
    lane = np.zeros(MLA_QK, np.int64)
    for dd in range(MLA_NOPE):
        lane[dd] = 16 + dd if dd < 48 else 80 + (dd - 48)
    half = MLA_ROPE // 2
    for i in range(MLA_ROPE):
        lane[MLA_NOPE + i] = i if i < half else 64 + (i - half)
    return lane


def _dsw_lane_of(hh, dd):
    half = DSW_ROT // 2
    if dd < half:
        return hh * half + dd
    if dd < DSW_ROT:
        return 64 + hh * half + (dd - half)
    return (16 if hh == 0 else 80) + (dd - DSW_ROT)


def _rope_lane_tables(half, per_lane_index):
    inv_freq = jnp.power(jnp.float32(ROPE_THETA), -jnp.arange(half, dtype=F32) * (2.0 / (2 * half)))
    idx = np.zeros(LANES, np.int64)
    on = np.zeros(LANES, bool)
    sign = np.zeros(LANES, np.float32)
    for l in range(LANES):
        i = per_lane_index(l)
        if i is not None:
            idx[l], on[l] = i, True
            sign[l] = -1.0 if l < 64 else 1.0
    freq = jnp.where(jnp.asarray(on), jnp.take(inv_freq, jnp.asarray(idx)), 0.0)
    return freq.reshape(1, LANES), jnp.asarray(sign).reshape(1, LANES)


def kernel(x, positions, norm_attn, norm_ffn, mla_w_down, mla_q_norm, mla_w_uq, mla_kv_norm, mla_w_ukv,
           mla_w_o, dsw_kv_norm, dsw_w_kv, dsw_w_q, dsw_w_o, ffn_w_gate, ffn_w_up, ffn_w_down, moe_router,
           moe_w_gate, moe_w_up, moe_w_down, final_norm):
    B, S, D = x.shape
    T = B * S
    pos = positions.reshape(B, S, 1)

    lane_of = _mla_lane_of_dim()
    H = MLA_HEADS
    src_dn = np.full(MLA_Q_LORA + MLA_KV_LORA + LANES, -1, np.int64)
    src_dn[:MLA_Q_LORA + MLA_KV_LORA] = np.arange(MLA_Q_LORA + MLA_KV_LORA)
    src_uq = np.full(H * LANES, -1, np.int64)
    src_k = np.full(H * LANES, -1, np.int64)
    src_v = np.zeros(H * MLA_V, np.int64)
    for i in range(MLA_ROPE):
        src_dn[MLA_Q_LORA + MLA_KV_LORA + lane_of[MLA_NOPE + i]] = MLA_Q_LORA + MLA_KV_LORA + i
    for h in range(H):
        for dd in range(MLA_QK):
            src_uq[h * LANES + lane_of[dd]] = h * MLA_QK + dd
        for dd in range(MLA_NOPE):
            src_k[h * LANES + lane_of[dd]] = h * (MLA_NOPE + MLA_V) + dd
        for dd in range(MLA_V):
            src_v[h * MLA_V + dd] = h * (MLA_NOPE + MLA_V) + MLA_NOPE + dd
    wdn = _take_cols(mla_w_down[0], src_dn).astype(BF16)
    wuq = _take_cols(mla_w_uq[0], src_uq).astype(BF16)
    wk = _take_cols(mla_w_ukv[0], src_k).astype(BF16)
    wv = _take_cols(mla_w_ukv[0], src_v).astype(BF16)
    half = MLA_ROPE // 2
    mla_freq, mla_sign = _rope_lane_tables(
        half, lambda l: l if l < half else (l - 64 if 64 <= l < 64 + half else None))

    q, k, v = _mla_proj(x, pos, norm_attn[0:1], wdn, mla_q_norm[0:1], mla_kv_norm[0:1], wuq, wk, wv,
                        mla_freq, mla_sign)
    o = _mla_attn(q, k, v)
    h = _attn_ffn(x.reshape(T, D), o.reshape(T, D), mla_w_o[0].astype(BF16), norm_ffn[0:1],
                  ffn_w_gate[0].astype(BF16), ffn_w_up[0].astype(BF16), ffn_w_down[0].astype(BF16))

    src_pair = np.zeros(DSW_WIDTH, np.int64)
    for j in range(DSW_HEADS // 2):
        for hh in range(2):
            for dd in range(DSW_HD):
                src_pair[j * LANES + _dsw_lane_of(hh, dd)] = (2 * j + hh) * DSW_HD + dd
    wq_b, wk_b, wv_b = [], [], []
    for g in range(N_BR):
        wq_b.append(_take_cols(dsw_w_q[0], g * DSW_WIDTH + src_pair))
        wk_b.append(_take_cols(dsw_w_kv, g * DSW_WIDTH + src_pair))
        wv_b.append(dsw_w_kv[:, (N_BR + g) * DSW_WIDTH:(N_BR + g + 1) * DSW_WIDTH])
    wq_b = jnp.stack(wq_b).astype(BF16)
    wk_b = jnp.stack(wk_b).astype(BF16)
    wv_b = jnp.stack(wv_b).astype(BF16)
    half = DSW_ROT // 2
    dsw_freq, dsw_sign = _rope_lane_tables(
        half, lambda l: l % half if l < 2 * half else ((l - 64) % half if 64 <= l < 64 + 2 * half else None))

    qkv = _dsw_proj(h.reshape(B, S, D), pos, norm_attn[1:2], dsw_kv_norm.reshape(1, D), wq_b, wk_b, wv_b,
                    dsw_freq, dsw_sign)
    o = _dsw_attn(qkv)
    wr = jnp.pad(moe_router[0], ((0, 0), (0, LANES - N_EXPERTS)))
    h3, hn, meta, counts = _dsw_router(h, o.reshape(T, D), dsw_w_o[0].astype(BF16), norm_ffn[1:2], wr)

    n_tiles = (2 * T) // MOE_ROWS + N_EXPERTS
    d1, d2, tile_expert, n_used = _moe_plan(meta, counts, n_tiles)
    xs = _moe_dispatch(d1, d2, hn, n_tiles * MOE_ROWS)
    ys = _moe_group(tile_expert, n_used, xs, moe_w_gate[0].astype(BF16), moe_w_up[0].astype(BF16),
                    moe_w_down[0].astype(BF16))
    out = _moe_combine(d1, d2, meta, h3, final_norm.reshape(1, D), ys)
    return out.reshape(B, S, D)
```

```python
import functools
import math

import numpy as np
import jax
import jax.numpy as jnp
from jax import lax
from jax.experimental import pallas as pl
from jax.experimental.pallas import tpu as pltpu

F32 = jnp.float32
BF16 = jnp.bfloat16

D_MODEL = 1024
RMS_EPS = 1e-6
ROPE_THETA = 500000.0
NEG_INF = -1e30
LANES = 128

MLA_HEADS = 16
MLA_NOPE = 64
MLA_ROPE = 32
MLA_V = 64
MLA_QK = MLA_NOPE + MLA_ROPE
MLA_Q_LORA = 384
MLA_KV_LORA = 256

DSW_BRANCHES = ((128, 1), (512, 4), (2048, 16))
N_BR = 3
DSW_HEADS = 16
DSW_HD = 64
DSW_ROT = 16
DSW_WIDTH = DSW_HEADS * DSW_HD
DSW_BLOCK = 128

FFN_DIM = 2816
N_EXPERTS = 8
EXPERT_DIM = 3584

VMEM_LIMIT = 56 * 1024 * 1024


def _cparams(sem):
    return pltpu.CompilerParams(dimension_semantics=sem, vmem_limit_bytes=VMEM_LIMIT)


def _resident(shape):
    nd = len(shape)
    return pl.BlockSpec(shape, lambda *_: (0,) * nd, pipeline_mode=pl.Buffered(1))


def _rms(x, g):
    return x * lax.rsqrt(jnp.mean(x * x, axis=-1, keepdims=True) + RMS_EPS) * g


def _dot(a, b):
    return jnp.dot(a, b, preferred_element_type=F32)


def _dot_nt(a, b):
    return lax.dot_general(a, b, (((1,), (1,)), ((), ())), preferred_element_type=F32)


def _rope(t, cos, sin_signed):
    return t * cos + pltpu.roll(t, 64, 1) * sin_signed


def _mla_proj_kernel(x_ref, pos_ref, g_ref, wdn_ref, qn_ref, kvn_ref, wuq_ref, wk_ref, wv_ref,
                     freq_ref, sign_ref, q_ref, k_ref, v_ref):
    xn = _rms(x_ref[...], g_ref[...]).astype(BF16)
    down = _dot(xn, wdn_ref[...])
    cq = _rms(down[:, :MLA_Q_LORA], qn_ref[...]).astype(BF16)
    ckv = _rms(down[:, MLA_Q_LORA:MLA_Q_LORA + MLA_KV_LORA], kvn_ref[...]).astype(BF16)
    kr = down[:, MLA_Q_LORA + MLA_KV_LORA:]
    ang = pos_ref[...].astype(F32) * freq_ref[...]
    cos = jnp.cos(ang)
    sin = jnp.sin(ang) * sign_ref[...]
    kr = _rope(kr, cos, sin)
    scale = 1.0 / math.sqrt(MLA_QK)
    cos_q = cos * scale
    sin_q = sin * scale
    q = _dot(cq, wuq_ref[...])
    for h in range(MLA_HEADS):
        q_ref[h] = _rope(q[:, h * LANES:(h + 1) * LANES], cos_q, sin_q).T.astype(BF16)
    kp = _dot(ckv, wk_ref[...])
    for h in range(MLA_HEADS):
        k_ref[h] = (kp[:, h * LANES:(h + 1) * LANES] + kr).astype(BF16)
    v = _dot(ckv, wv_ref[...])
    for j in range(MLA_HEADS // 2):
        v_ref[j] = v[:, j * LANES:(j + 1) * LANES].T.astype(BF16)


def _mla_proj(x, pos, g, wdn, qn, kvn, wuq, wk, wv, freq, sign, tm):
    B, S, _ = x.shape
    H = MLA_HEADS
    return pl.pallas_call(
        _mla_proj_kernel,
        grid=(B, S // tm),
        in_specs=[
            pl.BlockSpec((None, tm, D_MODEL), lambda b, i: (b, i, 0)),
            pl.BlockSpec((None, tm, 1), lambda b, i: (b, i, 0)),
            _resident(g.shape), _resident(wdn.shape), _resident(qn.shape), _resident(kvn.shape),
            _resident(wuq.shape), _resident(wk.shape), _resident(wv.shape),
            _resident(freq.shape), _resident(sign.shape),
        ],
        out_specs=[
            pl.BlockSpec((None, H, None, LANES, tm), lambda b, i: (b, 0, i, 0, 0)),
            pl.BlockSpec((None, H, tm, LANES), lambda b, i: (b, 0, i, 0)),
            pl.BlockSpec((None, H // 2, None, LANES, tm), lambda b, i: (b, 0, i, 0, 0)),
        ],
        out_shape=[
            jax.ShapeDtypeStruct((B, H, S // tm, LANES, tm), BF16),
            jax.ShapeDtypeStruct((B, H, S, LANES), BF16),
            jax.ShapeDtypeStruct((B, H // 2, S // tm, LANES, tm), BF16),
        ],
        compiler_params=_cparams(("parallel", "parallel")),
        name="mla_proj",
    )(x, pos, g, wdn, qn, kvn, wuq, wk, wv, freq, sign)


MLA_TILE = 512

def _mla_attn_kernel(qT_ref, k_ref, vT_ref, o_ref, m_scr, l_scr, acc_scr, s_scr, *, t):
    i = pl.program_id(2)
    m_scr[...] = jnp.full(m_scr.shape, NEG_INF, F32)
    l_scr[...] = jnp.zeros(l_scr.shape, F32)
    acc_scr[...] = jnp.zeros(acc_scr.shape, F32)

    def scores(c, hh):
        return _dot(k_ref[hh, pl.ds(pl.multiple_of(c * t, t), t), :], qT_ref[hh])

    def update(s, c, hh):
        rows = slice(hh * MLA_V, (hh + 1) * MLA_V)
        m_prev = m_scr[hh]
        m_new = jnp.maximum(m_prev, jnp.max(s, axis=0, keepdims=True))
        alpha = jnp.exp(m_prev - m_new)
        p = jnp.exp(s - m_new)
        l_scr[hh] = alpha * l_scr[hh] + jnp.sum(p, axis=0, keepdims=True)
        acc_scr[rows, :] = alpha * acc_scr[rows, :] + _dot(vT_ref[c, rows, :], p.astype(BF16))
        m_scr[hh] = m_new

    for hh in range(2):
        s_scr[hh] = scores(0, hh)

    def body(c, carry):
        for hh in range(2):
            s = s_scr[hh]
            s_scr[hh] = scores(c + 1, hh)
            update(s, c, hh)
        return carry

    lax.fori_loop(0, i, body, 0)
    key = lax.broadcasted_iota(jnp.int32, (t, t), 0)
    qry = lax.broadcasted_iota(jnp.int32, (t, t), 1)
    for hh in range(2):
        update(jnp.where(key <= qry, s_scr[hh], NEG_INF), i, hh)
    inv = jnp.concatenate([jnp.broadcast_to(1.0 / l_scr[hh], (MLA_V, t)) for hh in range(2)], axis=0)
    o_ref[...] = (acc_scr[...] * inv).T.astype(BF16)


def _mla_attn(qT, k, vT):
    B, H, n, _, t = qT.shape
    S = n * t
    return pl.pallas_call(
        functools.partial(_mla_attn_kernel, t=t),
        grid=(B, H // 2, n),
        in_specs=[
            pl.BlockSpec((None, 2, None, LANES, t), lambda b, j, i: (b, j, i, 0, 0)),
            pl.BlockSpec((None, 2, S, LANES), lambda b, j, i: (b, j, 0, 0)),
            pl.BlockSpec((None, None, n, LANES, t), lambda b, j, i: (b, j, 0, 0, 0)),
        ],
        out_specs=pl.BlockSpec((None, t, LANES), lambda b, j, i: (b, i, j)),
        out_shape=jax.ShapeDtypeStruct((B, S, (H // 2) * LANES), BF16),
        scratch_shapes=[pltpu.VMEM((2, 1, t), F32), pltpu.VMEM((2, 1, t), F32), pltpu.VMEM((LANES, t), F32),
                        pltpu.VMEM((2, t, t), F32)],
        compiler_params=_cparams(("parallel", "parallel", "parallel")),
        name="mla_attn",
    )(qT, k, vT)


def _swiglu_acc(hn, wg_ref, wu_ref, wd_ref, chunk, scale=None):
    acc = None
    n = wg_ref.shape[-1]
    for c0 in range(0, n, chunk):
        g = _dot(hn, wg_ref[:, c0:c0 + chunk])
        u = _dot(hn, wu_ref[:, c0:c0 + chunk])
        a = g * jax.nn.sigmoid(g) * u
        if scale is not None:
            a = a * scale
        part = _dot(a.astype(BF16), wd_ref[c0:c0 + chunk, :])
        acc = part if acc is None else acc + part
    return acc


def _attn_ffn_kernel(x_ref, o_ref, wo_ref, g_ref, wg_ref, wu_ref, wd_ref, h_ref):
    h1 = x_ref[...] + _dot(o_ref[...], wo_ref[...])
    hn = _rms(h1, g_ref[...]).astype(BF16)
    h_ref[...] = h1 + _swiglu_acc(hn, wg_ref, wu_ref, wd_ref, FFN_DIM // 2)


def _attn_ffn(x, o, wo, g, wg, wu, wd, tm=512):
    T = x.shape[0]
    return pl.pallas_call(
        _attn_ffn_kernel,
        grid=(T // tm,),
        in_specs=[
            pl.BlockSpec((tm, D_MODEL), lambda i: (i, 0)),
            pl.BlockSpec((tm, D_MODEL), lambda i: (i, 0)),
            _resident(wo.shape), _resident(g.shape), _resident(wg.shape), _resident(wu.shape),
            _resident(wd.shape),
        ],
        out_specs=pl.BlockSpec((tm, D_MODEL), lambda i: (i, 0)),
        out_shape=jax.ShapeDtypeStruct((T, D_MODEL), F32),
        compiler_params=_cparams(("parallel",)),
        name="attn_ffn",
    )(x, o, wo, g, wg, wu, wd)


def _dsw_proj_kernel(h_ref, pos_ref, gq_ref, gkv_ref, wq_ref, wk_ref, wv_ref, freq_ref, sign_ref,
                     q1, k1, v1, q2, k2, v2, q3, k3, v3,
                     xq_scr, xkv_scr, cos_scr, sin_scr, pq_scr, pkv_scr, pcos_scr, psin_scr, *, tm):
    h = h_ref[...]
    r = lax.rsqrt(jnp.mean(h * h, axis=-1, keepdims=True) + RMS_EPS)
    hr = h * r
    nblk = D_MODEL // LANES
    xq_f = hr * gq_ref[...]
    xkv_f = hr * gkv_ref[...]
    for cb in range(nblk):
        xq_scr[cb] = xq_f[:, cb * LANES:(cb + 1) * LANES]
        xkv_scr[cb] = xkv_f[:, cb * LANES:(cb + 1) * LANES]
    ang = pos_ref[...].astype(F32) * freq_ref[...]
    cos_scr[...] = jnp.cos(ang)
    sin_scr[...] = jnp.sin(ang) * sign_ref[...]
    scale = 1.0 / math.sqrt(DSW_HD)
    outs = ((q1, k1, v1), (q2, k2, v2), (q3, k3, v3))
    for g, (_, d) in enumerate(DSW_BRANCHES):
        n = tm // d
        q_out, k_out, v_out = outs[g]
        if d == 1:
            xq = xq_f.astype(BF16)
            xkv = xkv_f.astype(BF16)
            cos = cos_scr[...]
            sin = sin_scr[...]
        else:
            for rr in range(d):
                rows = slice(rr * n, (rr + 1) * n)
                for cb in range(nblk):
                    cols = slice(cb * LANES, (cb + 1) * LANES)
                    pq_scr[rows, cols] = xq_scr[cb, pl.ds(rr, n, stride=d), :].astype(BF16)
                    pkv_scr[rows, cols] = xkv_scr[cb, pl.ds(rr, n, stride=d), :].astype(BF16)
                pcos_scr[rows, :] = cos_scr[pl.ds(rr, n, stride=d), :]
                psin_scr[rows, :] = sin_scr[pl.ds(rr, n, stride=d), :]
            xq = pq_scr[...]
            xkv = pkv_scr[...]
            cos = pcos_scr[...]
            sin = psin_scr[...]
        q = _dot(xq, wq_ref[g])
        k = _dot(xkv, wk_ref[g])
        v = _dot(xkv, wv_ref[g])
        cos_q = cos * scale
        sin_q = sin * scale
        for j in range(DSW_HEADS // 2):
            sl = slice(j * LANES, (j + 1) * LANES)
            qj = _rope(q[:, sl], cos_q, sin_q).astype(BF16)
            kj = _rope(k[:, sl], cos, sin).astype(BF16)
            vj = v[:, sl].astype(BF16)
            for rr in range(d):
                q_out[j, rr] = qj[rr * n:(rr + 1) * n]
                k_out[j, rr] = kj[rr * n:(rr + 1) * n]
                v_out[j, rr] = vj[rr * n:(rr + 1) * n]


def _dsw_proj(h, pos, gq, gkv, wq, wk, wv, freq, sign, tm=256):
    B, S, _ = h.shape
    P = DSW_HEADS // 2
    out_specs, out_shape = [], []
    for _, d in DSW_BRANCHES:
        for _ in range(3):
            out_specs.append(pl.BlockSpec((None, P, d, tm // d, LANES), lambda b, i: (b, 0, 0, i, 0)))
            out_shape.append(jax.ShapeDtypeStruct((B, P, d, S // d, LANES), BF16))
    return pl.pallas_call(
        functools.partial(_dsw_proj_kernel, tm=tm),
        grid=(B, S // tm),
        in_specs=[
            pl.BlockSpec((None, tm, D_MODEL), lambda b, i: (b, i, 0)),
            pl.BlockSpec((None, tm, 1), lambda b, i: (b, i, 0)),
            _resident(gq.shape), _resident(gkv.shape), _resident(wq.shape), _resident(wk.shape),
            _resident(wv.shape), _resident(freq.shape), _resident(sign.shape),
        ],
        out_specs=out_specs,
        out_shape=out_shape,
        scratch_shapes=[
            pltpu.VMEM((D_MODEL // LANES, tm, LANES), F32), pltpu.VMEM((D_MODEL // LANES, tm, LANES), F32),
            pltpu.VMEM((tm, LANES), F32), pltpu.VMEM((tm, LANES), F32),
            pltpu.VMEM((tm, D_MODEL), BF16), pltpu.VMEM((tm, D_MODEL), BF16),
            pltpu.VMEM((tm, LANES), F32), pltpu.VMEM((tm, LANES), F32),
        ],
        compiler_params=_cparams(("parallel", "parallel")),
        name="dsw_proj",
    )(h, pos, gq, gkv, wq, wk, wv, freq, sign)


def _unroll_of(n):
    for u in (4, 5, 3, 2):
        if n % u == 0:
            return u
    return 1


def _dsw_attn_kernel(q1, k1, v1, q2, k2, v2, q3, k3, v3, o_ref,
                     o1_scr, l1_scr, o2_scr, l2_scr, o3_scr, l3_scr, bo_scr, bl_scr, *, S):
    blk = DSW_BLOCK
    lane = lax.broadcasted_iota(jnp.int32, (blk, LANES), 1)
    mask_a = (lane < 8) | ((lane >= 16) & (lane < 72))
    first_half = lane < DSW_HD
    row1 = lax.broadcasted_iota(jnp.int32, (2 * blk, blk), 0)
    col1 = lax.broadcasted_iota(jnp.int32, (2 * blk, blk), 1)
    row1 = jnp.where(row1 >= blk, row1 - blk, row1)
    valid_first = col1 <= row1
    row2 = lax.broadcasted_iota(jnp.int32, (2 * blk, 2 * blk), 0)
    col2 = lax.broadcasted_iota(jnp.int32, (2 * blk, 2 * blk), 1)
    row2 = jnp.where(row2 >= blk, row2 - blk, row2)
    valid_next = ((col2 < blk) & (col2 >= row2)) | ((col2 >= blk) & (col2 - blk <= row2))

    def scores(q_ref, k_ref, off, win):
        qf = q_ref[pl.ds(pl.multiple_of(off, blk), blk), :].astype(F32)
        q2 = jnp.concatenate([jnp.where(mask_a, qf, 0.0), jnp.where(mask_a, 0.0, qf)], axis=0).astype(BF16)
        return _dot_nt(q2, k_ref[win, :])

    def finish(s, v_ref, off, win, valid, o_dst, l_dst):
        s = jnp.where(valid, s, NEG_INF)
        m = jnp.max(s, axis=-1, keepdims=True)
        p = jnp.exp(s - m)
        l = jnp.sum(p, axis=-1, keepdims=True)
        o2 = _dot(p.astype(BF16), v_ref[win, :]) / l
        lse = m + jnp.log(l)
        sl = pl.ds(pl.multiple_of(off, blk), blk)
        o_dst[sl, :] = jnp.where(first_half, o2[:blk], o2[blk:])
        l_dst[sl, :] = jnp.where(first_half, lse[:blk], lse[blk:])

    def run_blocks(n, locate, valid, q_ref, k_ref, v_ref, o_dst, l_dst):
        def body(i, s):
            s_next = scores(q_ref, k_ref, *locate(jnp.minimum(i + 1, n - 1)))
            off, win = locate(i)
            finish(s, v_ref, off, win, valid, o_dst, l_dst)
            return s_next

        lax.fori_loop(0, n, body, scores(q_ref, k_ref, *locate(0)), unroll=_unroll_of(n))

    branches = ((q1, k1, v1, o1_scr, l1_scr, o1_scr, l1_scr),
                (q2, k2, v2, bo_scr, bl_scr, o2_scr, l2_scr),
                (q3, k3, v3, bo_scr, bl_scr, o3_scr, l3_scr))
    for (_, d), (q_ref, k_ref, v_ref, o_dst, l_dst, o_tok, l_tok) in zip(DSW_BRANCHES, branches):
        L = S // d
        nb = L // blk

        def locate_first(rr, L=L):
            off = rr * L
            return off, pl.ds(pl.multiple_of(off, blk), blk)

        def locate_next(idx, L=L, nb=nb):
            rr = idx // (nb - 1)
            off = rr * L + (idx - rr * (nb - 1) + 1) * blk
            return off, pl.ds(pl.multiple_of(off - blk, blk), 2 * blk)

        run_blocks(d, locate_first, valid_first, q_ref, k_ref, v_ref, o_dst, l_dst)
        if nb > 1:
            run_blocks(d * (nb - 1), locate_next, valid_next, q_ref, k_ref, v_ref, o_dst, l_dst)
        if d > 1:
            for rr in range(d):
                o_tok[pl.ds(rr, L, stride=d), :] = o_dst[rr * L:(rr + 1) * L, :]
                l_tok[pl.ds(rr, L, stride=d), :] = l_dst[rr * L:(rr + 1) * L, :]

    ch = 256
    for c0 in range(0, S, ch):
        sl = slice(c0, c0 + ch)
        la, lb, lc = l1_scr[sl, :], l2_scr[sl, :], l3_scr[sl, :]
        m = jnp.maximum(jnp.maximum(la, lb), lc)
        wa, wb, wc = jnp.exp(la - m), jnp.exp(lb - m), jnp.exp(lc - m)
        num = wa * o1_scr[sl, :] + wb * o2_scr[sl, :] + wc * o3_scr[sl, :]
        o_ref[sl, :] = (num / (wa + wb + wc)).astype(BF16)


def _dsw_attn(qkv):
    B, P = qkv[0].shape[0], qkv[0].shape[1]
    S = qkv[0].shape[2] * qkv[0].shape[3]
    flat = [a.reshape(B, P, S, LANES) for a in qkv]
    spec = pl.BlockSpec((None, None, S, LANES), lambda b, j: (b, j, 0, 0))
    return pl.pallas_call(
        functools.partial(_dsw_attn_kernel, S=S),
        grid=(B, P),
        in_specs=[spec] * 9,
        out_specs=pl.BlockSpec((None, S, LANES), lambda b, j: (b, 0, j)),
        out_shape=jax.ShapeDtypeStruct((B, S, P * LANES), BF16),
        scratch_shapes=[pltpu.VMEM((S, LANES), F32)] * 8,
        compiler_params=_cparams(("parallel", "parallel")),
        name="dsw_attn",
    )(*flat)


META_E1, META_E2, META_R1, META_R2, META_G1, META_G2 = range(6)


def _dsw_router_kernel(h_ref, o_ref, wo_ref, g_ref, wr_ref, h3_ref, hn_ref, meta_ref, cnt_ref, base_scr):
    @pl.when(pl.program_id(0) == 0)
    def _():
        base_scr[...] = jnp.zeros_like(base_scr)

    h3 = h_ref[...] + _dot(o_ref[...], wo_ref[...])
    h3_ref[...] = h3
    hn = _rms(h3, g_ref[...])
    hn_ref[...] = hn
    logits = jnp.dot(hn, wr_ref[...], preferred_element_type=F32, precision=lax.Precision.HIGHEST)
    tm = logits.shape[0]
    lane = lax.broadcasted_iota(jnp.int32, logits.shape, 1).astype(F32)
    logits = jnp.where(lane < N_EXPERTS, logits, -jnp.inf)
    v1 = jnp.max(logits, axis=-1, keepdims=True)
    i1 = jnp.min(jnp.where(logits == v1, lane, float(LANES)), axis=-1, keepdims=True)
    rest = jnp.where(lane == i1, -jnp.inf, logits)
    v2 = jnp.max(rest, axis=-1, keepdims=True)
    i2 = jnp.min(jnp.where(rest == v2, lane, float(LANES)), axis=-1, keepdims=True)
    e2 = jnp.exp(v2 - v1)
    den = 1.0 + e2
    onehot = jnp.where((lane == i1) | (lane == i2), 1.0, 0.0)
    row = lax.broadcasted_iota(jnp.int32, (tm, tm), 0)
    col = lax.broadcasted_iota(jnp.int32, (tm, tm), 1)
    earlier = jnp.where(col < row, 1.0, 0.0).astype(BF16)
    before = _dot(earlier, onehot.astype(BF16)) + base_scr[0:1, :]
    r1 = jnp.sum(jnp.where(lane == i1, before, 0.0), axis=-1, keepdims=True)
    r2 = jnp.sum(jnp.where(lane == i2, before, 0.0), axis=-1, keepdims=True)
    total = base_scr[0:1, :] + jnp.sum(onehot, axis=0, keepdims=True)
    base_scr[...] = jnp.broadcast_to(total, base_scr.shape)
    cnt_ref[...] = jnp.broadcast_to(total, cnt_ref.shape)
    meta = jnp.zeros_like(logits)
    for k, val in ((META_E1, i1), (META_E2, i2), (META_R1, r1), (META_R2, r2),
                   (META_G1, 1.0 / den), (META_G2, e2 / den)):
        meta = jnp.where(lane == float(k), val, meta)
    meta_ref[...] = meta


def _dsw_router(h, o, wo, g, wr, tm=512):
    T = h.shape[0]
    return pl.pallas_call(
        _dsw_router_kernel,
        grid=(T // tm,),
        in_specs=[
            pl.BlockSpec((tm, D_MODEL), lambda i: (i, 0)),
            pl.BlockSpec((tm, D_MODEL), lambda i: (i, 0)),
            _resident(wo.shape), _resident(g.shape), _resident(wr.shape),
        ],
        out_specs=[
            pl.BlockSpec((tm, D_MODEL), lambda i: (i, 0)),
            pl.BlockSpec((tm, D_MODEL), lambda i: (i, 0)),
            pl.BlockSpec((tm, LANES), lambda i: (i, 0)),
            pl.BlockSpec((8, LANES), lambda i: (0, 0)),
        ],
        out_shape=[
            jax.ShapeDtypeStruct((T, D_MODEL), F32),
            jax.ShapeDtypeStruct((T, D_MODEL), F32),
            jax.ShapeDtypeStruct((T, LANES), F32),
            jax.ShapeDtypeStruct((8, LANES), F32),
        ],
        scratch_shapes=[pltpu.VMEM((8, LANES), F32)],
        compiler_params=_cparams(("arbitrary",)),
        name="dsw_router",
    )(h, o, wo, g, wr)


MOE_ROWS = 512


def _row_copy(src, src_row, dst, dst_row, sem):
    return pltpu.make_async_copy(src.at[pl.ds(src_row, 1)], dst.at[pl.ds(dst_row, 1)], sem)


def _moe_dispatch_kernel(d1_ref, d2_ref, hn_ref, xs_in_hbm, xs_hbm, sem, *, tm):
    del xs_in_hbm

    def body(r, c):
        _row_copy(hn_ref, r, xs_hbm, d1_ref[0, r], sem).start()
        _row_copy(hn_ref, r, xs_hbm, d2_ref[0, r], sem).start()
        return c

    lax.fori_loop(0, tm, body, 0, unroll=8)
    for _ in range(2):
        pltpu.make_async_copy(hn_ref, xs_hbm.at[pl.ds(0, tm)], sem).wait()


def _moe_dispatch(d1, d2, hn, n_rows, tm=512):
    T = hn.shape[0]
    smem = lambda: pl.BlockSpec((None, 1, tm), lambda i: (i, 0, 0), memory_space=pltpu.SMEM)
    return pl.pallas_call(
        functools.partial(_moe_dispatch_kernel, tm=tm),
        grid=(T // tm,),
        in_specs=[smem(), smem(), pl.BlockSpec((tm, D_MODEL), lambda i: (i, 0)),
                  pl.BlockSpec(memory_space=pl.ANY)],
        out_specs=pl.BlockSpec(memory_space=pl.ANY),
        out_shape=jax.ShapeDtypeStruct((n_rows, D_MODEL), F32),
        scratch_shapes=[pltpu.SemaphoreType.DMA],
        input_output_aliases={3: 0},
        compiler_params=_cparams(("arbitrary",)),
        name="moe_dispatch",
    )(d1.reshape(T // tm, 1, tm), d2.reshape(T // tm, 1, tm), hn, jnp.zeros((n_rows, D_MODEL), F32))


def _moe_group_kernel(te_ref, nu_ref, xs_ref, wg_ref, wu_ref, wd_ref, ys_ref, xb_scr, acc_scr):
    j = pl.program_id(0)
    f = pl.program_id(1)
    used = j < nu_ref[0]

    @pl.when(used & (f == 0))
    def _():
        xb_scr[...] = xs_ref[...].astype(BF16)
        acc_scr[...] = jnp.zeros_like(acc_scr)

    @pl.when(used)
    def _():
        acc_scr[...] += _swiglu_acc(xb_scr[...], wg_ref, wu_ref, wd_ref, wg_ref.shape[-1] // 2)

    @pl.when(f == pl.num_programs(1) - 1)
    def _():
        ys_ref[...] = jnp.where(used, acc_scr[...], 0.0)


def _moe_group(tile_expert, n_used, xs, wg, wu, wd, tf=1792):
    n_rows = xs.shape[0]
    R = MOE_ROWS
    E, _, Fe = wg.shape
    nf = Fe // tf

    def w_col(j, f, te, nu):
        return jnp.where(j < nu[0], f, nf - 1)

    return pl.pallas_call(
        _moe_group_kernel,
        grid_spec=pltpu.PrefetchScalarGridSpec(
            num_scalar_prefetch=2,
            grid=(n_rows // R, nf),
            in_specs=[
                pl.BlockSpec((R, D_MODEL), lambda j, f, te, nu: (jnp.maximum(jnp.minimum(j, nu[0] - 1), 0), 0)),
                pl.BlockSpec((None, D_MODEL, tf), lambda j, f, te, nu: (te[j], 0, w_col(j, f, te, nu))),
                pl.BlockSpec((None, D_MODEL, tf), lambda j, f, te, nu: (te[j], 0, w_col(j, f, te, nu))),
                pl.BlockSpec((None, tf, D_MODEL), lambda j, f, te, nu: (te[j], w_col(j, f, te, nu), 0)),
            ],
            out_specs=pl.BlockSpec((R, D_MODEL), lambda j, f, te, nu: (j, 0)),
            scratch_shapes=[pltpu.VMEM((R, D_MODEL), BF16), pltpu.VMEM((R, D_MODEL), F32)],
        ),
        out_shape=jax.ShapeDtypeStruct((n_rows, D_MODEL), F32),
        compiler_params=_cparams(("arbitrary", "arbitrary")),
        name="moe_group",
    )(tile_expert, n_used, xs, wg, wu, wd)


def _moe_combine_kernel(d1_ref, d2_ref, meta_ref, h3_ref, gf_ref, ys_hbm, out_ref, y1_scr, y2_scr, sem, *, tm):
    def body(r, c):
        _row_copy(ys_hbm, d1_ref[0, r], y1_scr, r, sem).start()
        _row_copy(ys_hbm, d2_ref[0, r], y2_scr, r, sem).start()
        return c

    lax.fori_loop(0, tm, body, 0, unroll=8)
    pltpu.make_async_copy(ys_hbm.at[pl.ds(0, tm)], y1_scr, sem).wait()
    pltpu.make_async_copy(ys_hbm.at[pl.ds(0, tm)], y2_scr, sem).wait()
    meta = meta_ref[...]
    lane = lax.broadcasted_iota(jnp.int32, meta.shape, 1)
    g1 = jnp.sum(jnp.where(lane == META_G1, meta, 0.0), axis=-1, keepdims=True)
    g2 = jnp.sum(jnp.where(lane == META_G2, meta, 0.0), axis=-1, keepdims=True)
    out_ref[...] = _rms(h3_ref[...] + g1 * y1_scr[...] + g2 * y2_scr[...], gf_ref[...])


def _moe_combine(d1, d2, meta, h3, gf, ys, tm=256):
    T = h3.shape[0]
    smem = lambda: pl.BlockSpec((None, 1, tm), lambda i: (i, 0, 0), memory_space=pltpu.SMEM)
    return pl.pallas_call(
        functools.partial(_moe_combine_kernel, tm=tm),
        grid=(T // tm,),
        in_specs=[
            smem(), smem(),
            pl.BlockSpec((tm, LANES), lambda i: (i, 0)),
            pl.BlockSpec((tm, D_MODEL), lambda i: (i, 0)),
            pl.BlockSpec((1, D_MODEL), lambda i: (0, 0)),
            pl.BlockSpec(memory_space=pl.ANY),
        ],
        out_specs=pl.BlockSpec((tm, D_MODEL), lambda i: (i, 0)),
        out_shape=jax.ShapeDtypeStruct((T, D_MODEL), F32),
        scratch_shapes=[pltpu.VMEM((tm, D_MODEL), F32), pltpu.VMEM((tm, D_MODEL), F32),
                        pltpu.SemaphoreType.DMA],
        compiler_params=_cparams(("arbitrary",)),
        name="moe_combine",
    )(d1.reshape(T // tm, 1, tm), d2.reshape(T // tm, 1, tm), meta, h3, gf, ys)


def _moe_plan(meta, counts, n_tiles):
    R = MOE_ROWS
    cnt = counts[0, :N_EXPERTS].astype(jnp.int32)
    tiles = (cnt + R - 1) // R
    tile_end = jnp.cumsum(tiles)
    row_start = (tile_end - tiles) * R
    e1 = meta[:, META_E1].astype(jnp.int32)
    e2 = meta[:, META_E2].astype(jnp.int32)
    d1 = row_start[e1] + meta[:, META_R1].astype(jnp.int32)
    d2 = row_start[e2] + meta[:, META_R2].astype(jnp.int32)
    n_used = tile_end[-1:]
    j = jnp.minimum(jnp.arange(n_tiles, dtype=jnp.int32), n_used - 1)
    tile_expert = jnp.sum(j[:, None] >= tile_end[None, :], axis=1).astype(jnp.int32)
    return d1, d2, tile_expert, n_used.astype(jnp.int32)


def _take_cols(w, src):
    src = np.asarray(src)
    cols = jnp.take(w, jnp.asarray(np.maximum(src, 0)), axis=1)
    return jnp.where(jnp.asarray(src >= 0)[None, :], cols, 0.0)


def _mla_lane_of_dim():
    lane = np.zeros(MLA_QK, np.int64)
    for dd in range(MLA_NOPE):
        lane[dd] = 16 + dd if dd < 48 else 80 + (dd - 48)
    half = MLA_ROPE // 2
    for i in range(MLA_ROPE):
        lane[MLA_NOPE + i] = i if i < half else 64 + (i - half)
    return lane


def _dsw_lane_of(hh, dd):
    half = DSW_ROT // 2
    if dd < half:
        return hh * half + dd
    if dd < DSW_ROT:
        return 64 + hh * half + (dd - half)
    return (16 if hh == 0 else 80) + (dd - DSW_ROT)


def _rope_lane_tables(half, per_lane_index):
    inv_freq = jnp.power(jnp.float32(ROPE_THETA), -jnp.arange(half, dtype=F32) * (2.0 / (2 * half)))
    idx = np.zeros(LANES, np.int64)
    on = np.zeros(LANES, bool)
    sign = np.zeros(LANES, np.float32)
    for l in range(LANES):
        i = per_lane_index(l)
        if i is not None:
            idx[l], on[l] = i, True
            sign[l] = -1.0 if l < 64 else 1.0
    freq = jnp.where(jnp.asarray(on), jnp.take(inv_freq, jnp.asarray(idx)), 0.0)
    return freq.reshape(1, LANES), jnp.asarray(sign).reshape(1, LANES)


def kernel(x, positions, norm_attn, norm_ffn, mla_w_down, mla_q_norm, mla_w_uq, mla_kv_norm, mla_w_ukv,
           mla_w_o, dsw_kv_norm, dsw_w_kv, dsw_w_q, dsw_w_o, ffn_w_gate, ffn_w_up, ffn_w_down, moe_router,
           moe_w_gate, moe_w_up, moe_w_down, final_norm):
    B, S, D = x.shape
    T = B * S
    pos = positions.reshape(B, S, 1)

    lane_of = _mla_lane_of_dim()
    H = MLA_HEADS
    src_dn = np.full(MLA_Q_LORA + MLA_KV_LORA + LANES, -1, np.int64)
    src_dn[:MLA_Q_LORA + MLA_KV_LORA] = np.arange(MLA_Q_LORA + MLA_KV_LORA)
    src_uq = np.full(H * LANES, -1, np.int64)
    src_k = np.full(H * LANES, -1, np.int64)
    src_v = np.zeros(H * MLA_V, np.int64)
    for i in range(MLA_ROPE):
        src_dn[MLA_Q_LORA + MLA_KV_LORA + lane_of[MLA_NOPE + i]] = MLA_Q_LORA + MLA_KV_LORA + i
    for h in range(H):
        for dd in range(MLA_QK):
            src_uq[h * LANES + lane_of[dd]] = h * MLA_QK + dd
        for dd in range(MLA_NOPE):
            src_k[h * LANES + lane_of[dd]] = h * (MLA_NOPE + MLA_V) + dd
        for dd in range(MLA_V):
            src_v[h * MLA_V + dd] = h * (MLA_NOPE + MLA_V) + MLA_NOPE + dd
    wdn = _take_cols(mla_w_down[0], src_dn).astype(BF16)
    wuq = _take_cols(mla_w_uq[0], src_uq).astype(BF16)
    wk = _take_cols(mla_w_ukv[0], src_k).astype(BF16)
    wv = _take_cols(mla_w_ukv[0], src_v).astype(BF16)
    half = MLA_ROPE // 2
    mla_freq, mla_sign = _rope_lane_tables(
        half, lambda l: l if l < half else (l - 64 if 64 <= l < 64 + half else None))

    q, k, v = _mla_proj(x, pos, norm_attn[0:1], wdn, mla_q_norm[0:1], mla_kv_norm[0:1], wuq, wk, wv,
                        mla_freq, mla_sign, tm=MLA_TILE)
    o = _mla_attn(q, k, v)
    h = _attn_ffn(x.reshape(T, D), o.reshape(T, D), mla_w_o[0].astype(BF16), norm_ffn[0:1],
                  ffn_w_gate[0].astype(BF16), ffn_w_up[0].astype(BF16), ffn_w_down[0].astype(BF16))

    src_pair = np.zeros(DSW_WIDTH, np.int64)
    for j in range(DSW_HEADS // 2):
        for hh in range(2):
            for dd in range(DSW_HD):
                src_pair[j * LANES + _dsw_lane_of(hh, dd)] = (2 * j + hh) * DSW_HD + dd
    wq_b, wk_b, wv_b = [], [], []
    for g in range(N_BR):
        wq_b.append(_take_cols(dsw_w_q[0], g * DSW_WIDTH + src_pair))
        wk_b.append(_take_cols(dsw_w_kv, g * DSW_WIDTH + src_pair))
        wv_b.append(dsw_w_kv[:, (N_BR + g) * DSW_WIDTH:(N_BR + g + 1) * DSW_WIDTH])
    wq_b = jnp.stack(wq_b).astype(BF16)
    wk_b = jnp.stack(wk_b).astype(BF16)
    wv_b = jnp.stack(wv_b).astype(BF16)
    half = DSW_ROT // 2
    dsw_freq, dsw_sign = _rope_lane_tables(
        half, lambda l: l % half if l < 2 * half else ((l - 64) % half if 64 <= l < 64 + 2 * half else None))

    qkv = _dsw_proj(h.reshape(B, S, D), pos, norm_attn[1:2], dsw_kv_norm.reshape(1, D), wq_b, wk_b, wv_b,
                    dsw_freq, dsw_sign)
    o = _dsw_attn(qkv)
    wr = jnp.pad(moe_router[0], ((0, 0), (0, LANES - N_EXPERTS)))
    h3, hn, meta, counts = _dsw_router(h, o.reshape(T, D), dsw_w_o[0].astype(BF16), norm_ffn[1:2], wr)

    n_tiles = (2 * T) // MOE_ROWS + N_EXPERTS
    d1, d2, tile_expert, n_used = _moe_plan(meta, counts, n_tiles)
    xs = _moe_dispatch(d1, d2, hn, n_tiles * MOE_ROWS)
    ys = _moe_group(tile_expert, n_used, xs, moe_w_gate[0].astype(BF16), moe_w_up[0].astype(BF16),
                    moe_w_down[0].astype(BF16))
    out = _moe_combine(d1, d2, meta, h3, final_norm.reshape(1, D), ys)
    return out.reshape(B, S, D)
```

```python
import functools
import math

import numpy as np
import jax
import jax.numpy as jnp
from jax import lax
from jax.experimental import pallas as pl
from jax.experimental.pallas import tpu as pltpu

F32 = jnp.float32
BF16 = jnp.bfloat16

D_MODEL = 1024
RMS_EPS = 1e-6
ROPE_THETA = 500000.0
NEG_INF = -1e30
LANES = 128

MLA_HEADS = 16
MLA_NOPE = 64
MLA_ROPE = 32
MLA_V = 64
MLA_QK = MLA_NOPE + MLA_ROPE
MLA_Q_LORA = 384
MLA_KV_LORA = 256

DSW_BRANCHES = ((128, 1), (512, 4), (2048, 16))
N_BR = 3
DSW_HEADS = 16
DSW_HD = 64
DSW_ROT = 16
DSW_WIDTH = DSW_HEADS * DSW_HD
DSW_BLOCK = 128

FFN_DIM = 2816
N_EXPERTS = 8
EXPERT_DIM = 3584

VMEM_LIMIT = 56 * 1024 * 1024


def _cparams(sem):
    return pltpu.CompilerParams(dimension_semantics=sem, vmem_limit_bytes=VMEM_LIMIT)


def _resident(shape):
    nd = len(shape)
    return pl.BlockSpec(shape, lambda *_: (0,) * nd, pipeline_mode=pl.Buffered(1))


def _rms(x, g):
    return x * lax.rsqrt(jnp.mean(x * x, axis=-1, keepdims=True) + RMS_EPS) * g


def _dot(a, b):
    return jnp.dot(a, b, preferred_element_type=F32)


def _dot_nt(a, b):
    return lax.dot_general(a, b, (((1,), (1,)), ((), ())), preferred_element_type=F32)


def _rope(t, cos, sin_signed):
    return t * cos + pltpu.roll(t, 64, 1) * sin_signed


def _mla_proj_kernel(x_ref, pos_ref, g_ref, wdn_ref, qn_ref, kvn_ref, wuq_ref, wk_ref, wv_ref,
                     freq_ref, sign_ref, q_ref, k_ref, v_ref):
    xn = _rms(x_ref[...], g_ref[...]).astype(BF16)
    down = _dot(xn, wdn_ref[...])
    cq = _rms(down[:, :MLA_Q_LORA], qn_ref[...]).astype(BF16)
    ckv = _rms(down[:, MLA_Q_LORA:MLA_Q_LORA + MLA_KV_LORA], kvn_ref[...]).astype(BF16)
    kr = down[:, MLA_Q_LORA + MLA_KV_LORA:]
    ang = pos_ref[...].astype(F32) * freq_ref[...]
    cos = jnp.cos(ang)
    sin = jnp.sin(ang) * sign_ref[...]
    kr = _rope(kr, cos, sin)
    scale = 1.0 / math.sqrt(MLA_QK)
    cos_q = cos * scale
    sin_q = sin * scale
    q = _dot(cq, wuq_ref[...])
    for h in range(MLA_HEADS):
        q_ref[h] = _rope(q[:, h * LANES:(h + 1) * LANES], cos_q, sin_q).T.astype(BF16)
    kp = _dot(ckv, wk_ref[...])
    for h in range(MLA_HEADS):
        k_ref[h] = (kp[:, h * LANES:(h + 1) * LANES] + kr).astype(BF16)
    v = _dot(ckv, wv_ref[...])
    for j in range(MLA_HEADS // 2):
        v_ref[j] = v[:, j * LANES:(j + 1) * LANES].T.astype(BF16)


def _mla_proj(x, pos, g, wdn, qn, kvn, wuq, wk, wv, freq, sign, tm):
    B, S, _ = x.shape
    H = MLA_HEADS
    return pl.pallas_call(
        _mla_proj_kernel,
        grid=(B, S // tm),
        in_specs=[
            pl.BlockSpec((None, tm, D_MODEL), lambda b, i: (b, i, 0)),
            pl.BlockSpec((None, tm, 1), lambda b, i: (b, i, 0)),
            _resident(g.shape), _resident(wdn.shape), _resident(qn.shape), _resident(kvn.shape),
            _resident(wuq.shape), _resident(wk.shape), _resident(wv.shape),
            _resident(freq.shape), _resident(sign.shape),
        ],
        out_specs=[
            pl.BlockSpec((None, H, None, LANES, tm), lambda b, i: (b, 0, i, 0, 0)),
            pl.BlockSpec((None, H, tm, LANES), lambda b, i: (b, 0, i, 0)),
            pl.BlockSpec((None, H // 2, None, LANES, tm), lambda b, i: (b, 0, i, 0, 0)),
        ],
        out_shape=[
            jax.ShapeDtypeStruct((B, H, S // tm, LANES, tm), BF16),
            jax.ShapeDtypeStruct((B, H, S, LANES), BF16),
            jax.ShapeDtypeStruct((B, H // 2, S // tm, LANES, tm), BF16),
        ],
        compiler_params=_cparams(("parallel", "parallel")),
        name="mla_proj",
    )(x, pos, g, wdn, qn, kvn, wuq, wk, wv, freq, sign)


MLA_TILE = 512

def _mla_attn_kernel(qT_ref, k_ref, vT_ref, o_ref, m_scr, l_scr, acc_scr, s_scr, *, t):
    i = pl.program_id(2)
    m_scr[...] = jnp.full(m_scr.shape, NEG_INF, F32)
    l_scr[...] = jnp.zeros(l_scr.shape, F32)
    acc_scr[...] = jnp.zeros(acc_scr.shape, F32)

    def scores(c, hh):
        return _dot(k_ref[hh, pl.ds(pl.multiple_of(c * t, t), t), :], qT_ref[hh])

    def update(s, c, hh):
        rows = slice(hh * MLA_V, (hh + 1) * MLA_V)
        m_prev = m_scr[hh]
        m_new = jnp.maximum(m_prev, jnp.max(s, axis=0, keepdims=True))
        alpha = jnp.exp(m_prev - m_new)
        p = jnp.exp(s - m_new)
        l_scr[hh] = alpha * l_scr[hh] + jnp.sum(p, axis=0, keepdims=True)
        acc_scr[rows, :] = alpha * acc_scr[rows, :] + _dot(vT_ref[c, rows, :], p.astype(BF16))
        m_scr[hh] = m_new

    for hh in range(2):
        s_scr[hh] = scores(0, hh)

    def body(c, carry):
        for hh in range(2):
            s = s_scr[hh]
            s_scr[hh] = scores(c + 1, hh)
            update(s, c, hh)
        return carry

    lax.fori_loop(0, i, body, 0)
    key = lax.broadcasted_iota(jnp.int32, (t, t), 0)
    qry = lax.broadcasted_iota(jnp.int32, (t, t), 1)
    for hh in range(2):
        update(jnp.where(key <= qry, s_scr[hh], NEG_INF), i, hh)
    inv = jnp.concatenate([jnp.broadcast_to(1.0 / l_scr[hh], (MLA_V, t)) for hh in range(2)], axis=0)
    o_ref[...] = (acc_scr[...] * inv).T.astype(BF16)


def _mla_attn(qT, k, vT):
    B, H, n, _, t = qT.shape
    S = n * t
    return pl.pallas_call(
        functools.partial(_mla_attn_kernel, t=t),
        grid=(B, H // 2, n),
        in_specs=[
            pl.BlockSpec((None, 2, None, LANES, t), lambda b, j, i: (b, j, i, 0, 0)),
            pl.BlockSpec((None, 2, S, LANES), lambda b, j, i: (b, j, 0, 0)),
            pl.BlockSpec((None, None, n, LANES, t), lambda b, j, i: (b, j, 0, 0, 0)),
        ],
        out_specs=pl.BlockSpec((None, t, LANES), lambda b, j, i: (b, i, j)),
        out_shape=jax.ShapeDtypeStruct((B, S, (H // 2) * LANES), BF16),
        scratch_shapes=[pltpu.VMEM((2, 1, t), F32), pltpu.VMEM((2, 1, t), F32), pltpu.VMEM((LANES, t), F32),
                        pltpu.VMEM((2, t, t), F32)],
        compiler_params=_cparams(("parallel", "parallel", "parallel")),
        name="mla_attn",
    )(qT, k, vT)


def _swiglu_acc(hn, wg_ref, wu_ref, wd_ref, chunk, scale=None):
    acc = None
    n = wg_ref.shape[-1]
    for c0 in range(0, n, chunk):
        g = _dot(hn, wg_ref[:, c0:c0 + chunk])
        u = _dot(hn, wu_ref[:, c0:c0 + chunk])
        a = g * jax.nn.sigmoid(g) * u
        if scale is not None:
            a = a * scale
        part = _dot(a.astype(BF16), wd_ref[c0:c0 + chunk, :])
        acc = part if acc is None else acc + part
    return acc


def _attn_ffn_kernel(x_ref, o_ref, wo_ref, g_ref, wg_ref, wu_ref, wd_ref, h_ref):
    h1 = x_ref[...] + _dot(o_ref[...], wo_ref[...])
    hn = _rms(h1, g_ref[...]).astype(BF16)
    h_ref[...] = h1 + _swiglu_acc(hn, wg_ref, wu_ref, wd_ref, FFN_DIM // 2)


def _attn_ffn(x, o, wo, g, wg, wu, wd, tm=512):
    T = x.shape[0]
    return pl.pallas_call(
        _attn_ffn_kernel,
        grid=(T // tm,),
        in_specs=[
            pl.BlockSpec((tm, D_MODEL), lambda i: (i, 0)),
            pl.BlockSpec((tm, D_MODEL), lambda i: (i, 0)),
            _resident(wo.shape), _resident(g.shape), _resident(wg.shape), _resident(wu.shape),
            _resident(wd.shape),
        ],
        out_specs=pl.BlockSpec((tm, D_MODEL), lambda i: (i, 0)),
        out_shape=jax.ShapeDtypeStruct((T, D_MODEL), F32),
        compiler_params=_cparams(("parallel",)),
        name="attn_ffn",
    )(x, o, wo, g, wg, wu, wd)


def _dsw_proj_kernel(h_ref, pos_ref, gq_ref, gkv_ref, wq_ref, wk_ref, wv_ref, freq_ref, sign_ref,
                     q1, k1, v1, q2, k2, v2, q3, k3, v3,
                     xq_scr, xkv_scr, cos_scr, sin_scr, pq_scr, pkv_scr, pcos_scr, psin_scr, *, tm):
    h = h_ref[...]
    r = lax.rsqrt(jnp.mean(h * h, axis=-1, keepdims=True) + RMS_EPS)
    hr = h * r
    nblk = D_MODEL // LANES
    xq_f = hr * gq_ref[...]
    xkv_f = hr * gkv_ref[...]
    for cb in range(nblk):
        xq_scr[cb] = xq_f[:, cb * LANES:(cb + 1) * LANES]
        xkv_scr[cb] = xkv_f[:, cb * LANES:(cb + 1) * LANES]
    ang = pos_ref[...].astype(F32) * freq_ref[...]
    cos_scr[...] = jnp.cos(ang)
    sin_scr[...] = jnp.sin(ang) * sign_ref[...]
    scale = 1.0 / math.sqrt(DSW_HD)
    outs = ((q1, k1, v1), (q2, k2, v2), (q3, k3, v3))
    for g, (_, d) in enumerate(DSW_BRANCHES):
        n = tm // d
        q_out, k_out, v_out = outs[g]
        if d == 1:
            xq = xq_f.astype(BF16)
            xkv = xkv_f.astype(BF16)
            cos = cos_scr[...]
            sin = sin_scr[...]
        else:
            for rr in range(d):
                rows = slice(rr * n, (rr + 1) * n)
                for cb in range(nblk):
                    cols = slice(cb * LANES, (cb + 1) * LANES)
                    pq_scr[rows, cols] = xq_scr[cb, pl.ds(rr, n, stride=d), :].astype(BF16)
                    pkv_scr[rows, cols] = xkv_scr[cb, pl.ds(rr, n, stride=d), :].astype(BF16)
                pcos_scr[rows, :] = cos_scr[pl.ds(rr, n, stride=d), :]
                psin_scr[rows, :] = sin_scr[pl.ds(rr, n, stride=d), :]
            xq = pq_scr[...]
            xkv = pkv_scr[...]
            cos = pcos_scr[...]
            sin = psin_scr[...]
        q = _dot(xq, wq_ref[g])
        k = _dot(xkv, wk_ref[g])
        v = _dot(xkv, wv_ref[g])
        cos_q = cos * scale
        sin_q = sin * scale
        for j in range(DSW_HEADS // 2):
            sl = slice(j * LANES, (j + 1) * LANES)
            qj = _rope(q[:, sl], cos_q, sin_q).astype(BF16)
            kj = _rope(k[:, sl], cos, sin).astype(BF16)
            vj = v[:, sl].astype(BF16)
            for rr in range(d):
                q_out[j, rr] = qj[rr * n:(rr + 1) * n]
                k_out[j, rr] = kj[rr * n:(rr + 1) * n]
                v_out[j, rr] = vj[rr * n:(rr + 1) * n]


def _dsw_proj(h, pos, gq, gkv, wq, wk, wv, freq, sign, tm=256):
    B, S, _ = h.shape
    P = DSW_HEADS // 2
    out_specs, out_shape = [], []
    for _, d in DSW_BRANCHES:
        for _ in range(3):
            out_specs.append(pl.BlockSpec((None, P, d, tm // d, LANES), lambda b, i: (b, 0, 0, i, 0)))
            out_shape.append(jax.ShapeDtypeStruct((B, P, d, S // d, LANES), BF16))
    return pl.pallas_call(
        functools.partial(_dsw_proj_kernel, tm=tm),
        grid=(B, S // tm),
        in_specs=[
            pl.BlockSpec((None, tm, D_MODEL), lambda b, i: (b, i, 0)),
            pl.BlockSpec((None, tm, 1), lambda b, i: (b, i, 0)),
            _resident(gq.shape), _resident(gkv.shape), _resident(wq.shape), _resident(wk.shape),
            _resident(wv.shape), _resident(freq.shape), _resident(sign.shape),
        ],
        out_specs=out_specs,
        out_shape=out_shape,
        scratch_shapes=[
            pltpu.VMEM((D_MODEL // LANES, tm, LANES), F32), pltpu.VMEM((D_MODEL // LANES, tm, LANES), F32),
            pltpu.VMEM((tm, LANES), F32), pltpu.VMEM((tm, LANES), F32),
            pltpu.VMEM((tm, D_MODEL), BF16), pltpu.VMEM((tm, D_MODEL), BF16),
            pltpu.VMEM((tm, LANES), F32), pltpu.VMEM((tm, LANES), F32),
        ],
        compiler_params=_cparams(("parallel", "parallel")),
        name="dsw_proj",
    )(h, pos, gq, gkv, wq, wk, wv, freq, sign)


def _unroll_of(n):
    for u in (4, 5, 3, 2):
        if n % u == 0:
            return u
    return 1


def _dsw_attn_kernel(q1, k1, v1, q2, k2, v2, q3, k3, v3, o_ref,
                     o1_scr, l1_scr, o2_scr, l2_scr, o3_scr, l3_scr, bo_scr, bl_scr, s_scr, *, S):
    blk = DSW_BLOCK
    lane = lax.broadcasted_iota(jnp.int32, (blk, LANES), 1)
    mask_a = (lane < 8) | ((lane >= 16) & (lane < 72))
    first_half = lane < DSW_HD
    row1 = lax.broadcasted_iota(jnp.int32, (2 * blk, blk), 0)
    col1 = lax.broadcasted_iota(jnp.int32, (2 * blk, blk), 1)
    row1 = jnp.where(row1 >= blk, row1 - blk, row1)
    valid_first = col1 <= row1
    row2 = lax.broadcasted_iota(jnp.int32, (2 * blk, 2 * blk), 0)
    col2 = lax.broadcasted_iota(jnp.int32, (2 * blk, 2 * blk), 1)
    row2 = jnp.where(row2 >= blk, row2 - blk, row2)
    valid_next = ((col2 < blk) & (col2 >= row2)) | ((col2 >= blk) & (col2 - blk <= row2))

    def scores(q_ref, k_ref, off, win):
        qf = q_ref[pl.ds(pl.multiple_of(off, blk), blk), :].astype(F32)
        q2 = jnp.concatenate([jnp.where(mask_a, qf, 0.0), jnp.where(mask_a, 0.0, qf)], axis=0).astype(BF16)
        return _dot_nt(q2, k_ref[win, :])

    def finish(s, v_ref, off, win, valid, o_dst, l_dst):
        s = jnp.where(valid, s, NEG_INF)
        m = jnp.max(s, axis=-1, keepdims=True)
        p = jnp.exp(s - m)
        l = jnp.sum(p, axis=-1, keepdims=True)
        o2 = _dot(p.astype(BF16), v_ref[win, :]) / l
        lse = m + jnp.log(l)
        sl = pl.ds(pl.multiple_of(off, blk), blk)
        o_dst[sl, :] = jnp.where(first_half, o2[:blk], o2[blk:])
        l_dst[sl, :] = jnp.where(first_half, lse[:blk], lse[blk:])

    def run_blocks(n, locate, valid, q_ref, k_ref, v_ref, o_dst, l_dst):
        nk = valid.shape[1]
        s_scr[:, :nk] = scores(q_ref, k_ref, *locate(0))

        def body(i, c):
            s = s_scr[:, :nk]
            s_scr[:, :nk] = scores(q_ref, k_ref, *locate(jnp.minimum(i + 1, n - 1)))
            off, win = locate(i)
            finish(s, v_ref, off, win, valid, o_dst, l_dst)
            return c

        lax.fori_loop(0, n, body, 0, unroll=_unroll_of(n))

    branches = ((q1, k1, v1, o1_scr, l1_scr, o1_scr, l1_scr),
                (q2, k2, v2, bo_scr, bl_scr, o2_scr, l2_scr),
                (q3, k3, v3, bo_scr, bl_scr, o3_scr, l3_scr))
    for (_, d), (q_ref, k_ref, v_ref, o_dst, l_dst, o_tok, l_tok) in zip(DSW_BRANCHES, branches):
        L = S // d
        nb = L // blk

        def locate_first(rr, L=L):
            off = rr * L
            return off, pl.ds(pl.multiple_of(off, blk), blk)

        def locate_next(idx, L=L, nb=nb):
            rr = idx // (nb - 1)
            off = rr * L + (idx - rr * (nb - 1) + 1) * blk
            return off, pl.ds(pl.multiple_of(off - blk, blk), 2 * blk)

        run_blocks(d, locate_first, valid_first, q_ref, k_ref, v_ref, o_dst, l_dst)
        if nb > 1:
            run_blocks(d * (nb - 1), locate_next, valid_next, q_ref, k_ref, v_ref, o_dst, l_dst)
        if d > 1:
            for rr in range(d):
                o_tok[pl.ds(rr, L, stride=d), :] = o_dst[rr * L:(rr + 1) * L, :]
                l_tok[pl.ds(rr, L, stride=d), :] = l_dst[rr * L:(rr + 1) * L, :]

    ch = 256
    for c0 in range(0, S, ch):
        sl = slice(c0, c0 + ch)
        la, lb, lc = l1_scr[sl, :], l2_scr[sl, :], l3_scr[sl, :]
        m = jnp.maximum(jnp.maximum(la, lb), lc)
        wa, wb, wc = jnp.exp(la - m), jnp.exp(lb - m), jnp.exp(lc - m)
        num = wa * o1_scr[sl, :] + wb * o2_scr[sl, :] + wc * o3_scr[sl, :]
        o_ref[sl, :] = (num / (wa + wb + wc)).astype(BF16)


def _dsw_attn(qkv):
    B, P = qkv[0].shape[0], qkv[0].shape[1]
    S = qkv[0].shape[2] * qkv[0].shape[3]
    flat = [a.reshape(B, P, S, LANES) for a in qkv]
    spec = pl.BlockSpec((None, None, S, LANES), lambda b, j: (b, j, 0, 0))
    return pl.pallas_call(
        functools.partial(_dsw_attn_kernel, S=S),
        grid=(B, P),
        in_specs=[spec] * 9,
        out_specs=pl.BlockSpec((None, S, LANES), lambda b, j: (b, 0, j)),
        out_shape=jax.ShapeDtypeStruct((B, S, P * LANES), BF16),
        scratch_shapes=[pltpu.VMEM((S, LANES), F32)] * 8 + [pltpu.VMEM((2 * DSW_BLOCK, 2 * DSW_BLOCK), F32)],
        compiler_params=_cparams(("parallel", "parallel")),
        name="dsw_attn",
    )(*flat)


META_E1, META_E2, META_R1, META_R2, META_G1, META_G2 = range(6)


def _dsw_router_kernel(h_ref, o_ref, wo_ref, g_ref, wr_ref, h3_ref, hn_ref, meta_ref, cnt_ref, base_scr):
    @pl.when(pl.program_id(0) == 0)
    def _():
        base_scr[...] = jnp.zeros_like(base_scr)

    h3 = h_ref[...] + _dot(o_ref[...], wo_ref[...])
    h3_ref[...] = h3
    hn = _rms(h3, g_ref[...])
    hn_ref[...] = hn
    hi = hn.astype(BF16)
    lo = (hn - hi.astype(F32)).astype(BF16)
    part = _dot(hi, wr_ref[...])
    logits = part[:, :LANES] + part[:, LANES:] + _dot(lo, wr_ref[:, :LANES])
    tm = logits.shape[0]
    lane = lax.broadcasted_iota(jnp.int32, logits.shape, 1).astype(F32)
    logits = jnp.where(lane < N_EXPERTS, logits, -jnp.inf)
    v1 = jnp.max(logits, axis=-1, keepdims=True)
    i1 = jnp.min(jnp.where(logits == v1, lane, float(LANES)), axis=-1, keepdims=True)
    rest = jnp.where(lane == i1, -jnp.inf, logits)
    v2 = jnp.max(rest, axis=-1, keepdims=True)
    i2 = jnp.min(jnp.where(rest == v2, lane, float(LANES)), axis=-1, keepdims=True)
    e2 = jnp.exp(v2 - v1)
    den = 1.0 + e2
    onehot = jnp.where((lane == i1) | (lane == i2), 1.0, 0.0)
    row = lax.broadcasted_iota(jnp.int32, (tm, tm), 0)
    col = lax.broadcasted_iota(jnp.int32, (tm, tm), 1)
    earlier = jnp.where(col < row, 1.0, 0.0).astype(BF16)
    before = _dot(earlier, onehot.astype(BF16)) + base_scr[0:1, :]
    r1 = jnp.sum(jnp.where(lane == i1, before, 0.0), axis=-1, keepdims=True)
    r2 = jnp.sum(jnp.where(lane == i2, before, 0.0), axis=-1, keepdims=True)
    total = base_scr[0:1, :] + jnp.sum(onehot, axis=0, keepdims=True)
    base_scr[...] = jnp.broadcast_to(total, base_scr.shape)
    cnt_ref[...] = jnp.broadcast_to(total, cnt_ref.shape)
    meta = jnp.zeros_like(logits)
    for k, val in ((META_E1, i1), (META_E2, i2), (META_R1, r1), (META_R2, r2),
                   (META_G1, 1.0 / den), (META_G2, e2 / den)):
        meta = jnp.where(lane == float(k), val, meta)
    meta_ref[...] = meta


def _dsw_router(h, o, wo, g, wr, tm=512):
    T = h.shape[0]
    return pl.pallas_call(
        _dsw_router_kernel,
        grid=(T // tm,),
        in_specs=[
            pl.BlockSpec((tm, D_MODEL), lambda i: (i, 0)),
            pl.BlockSpec((tm, D_MODEL), lambda i: (i, 0)),
            _resident(wo.shape), _resident(g.shape), _resident(wr.shape),
        ],
        out_specs=[
            pl.BlockSpec((tm, D_MODEL), lambda i: (i, 0)),
            pl.BlockSpec((tm, D_MODEL), lambda i: (i, 0)),
            pl.BlockSpec((tm, LANES), lambda i: (i, 0)),
            pl.BlockSpec((8, LANES), lambda i: (0, 0)),
        ],
        out_shape=[
            jax.ShapeDtypeStruct((T, D_MODEL), F32),
            jax.ShapeDtypeStruct((T, D_MODEL), F32),
            jax.ShapeDtypeStruct((T, LANES), F32),
            jax.ShapeDtypeStruct((8, LANES), F32),
        ],
        scratch_shapes=[pltpu.VMEM((8, LANES), F32)],
        compiler_params=_cparams(("arbitrary",)),
        name="dsw_router",
    )(h, o, wo, g, wr)


MOE_ROWS = 512


def _row_copy(src, src_row, dst, dst_row, sem):
    return pltpu.make_async_copy(src.at[pl.ds(src_row, 1)], dst.at[pl.ds(dst_row, 1)], sem)


def _moe_dispatch_kernel(d1_ref, d2_ref, hn_ref, xs_in_hbm, xs_hbm, sem, *, tm):
    del xs_in_hbm

    def body(r, c):
        _row_copy(hn_ref, r, xs_hbm, d1_ref[0, r], sem).start(priority=0)
        _row_copy(hn_ref, r, xs_hbm, d2_ref[0, r], sem).start(priority=1)
        return c

    lax.fori_loop(0, tm, body, 0, unroll=8)
    for _ in range(2):
        pltpu.make_async_copy(hn_ref, xs_hbm.at[pl.ds(0, tm)], sem).wait()


def _moe_dispatch(d1, d2, hn, n_rows, tm=512):
    T = hn.shape[0]
    smem = lambda: pl.BlockSpec((None, 1, tm), lambda i: (i, 0, 0), memory_space=pltpu.SMEM)
    return pl.pallas_call(
        functools.partial(_moe_dispatch_kernel, tm=tm),
        grid=(T // tm,),
        in_specs=[smem(), smem(), pl.BlockSpec((tm, D_MODEL), lambda i: (i, 0)),
                  pl.BlockSpec(memory_space=pl.ANY)],
        out_specs=pl.BlockSpec(memory_space=pl.ANY),
        out_shape=jax.ShapeDtypeStruct((n_rows, D_MODEL), F32),
        scratch_shapes=[pltpu.SemaphoreType.DMA],
        input_output_aliases={3: 0},
        compiler_params=_cparams(("arbitrary",)),
        name="moe_dispatch",
    )(d1.reshape(T // tm, 1, tm), d2.reshape(T // tm, 1, tm), hn, jnp.zeros((n_rows, D_MODEL), F32))


def _moe_group_kernel(te_ref, nu_ref, xs_ref, wg_ref, wu_ref, wd_ref, ys_ref, xb_scr, acc_scr):
    j = pl.program_id(0)
    f = pl.program_id(1)
    used = j < nu_ref[0]

    @pl.when(used & (f == 0))
    def _():
        xb_scr[...] = xs_ref[...].astype(BF16)
        acc_scr[...] = jnp.zeros_like(acc_scr)

    @pl.when(used)
    def _():
        acc_scr[...] += _swiglu_acc(xb_scr[...], wg_ref, wu_ref, wd_ref, wg_ref.shape[-1] // 2)

    @pl.when(f == pl.num_programs(1) - 1)
    def _():
        ys_ref[...] = jnp.where(used, acc_scr[...], 0.0)


def _moe_group(tile_expert, n_used, xs, wg, wu, wd, tf=1792):
    n_rows = xs.shape[0]
    R = MOE_ROWS
    E, _, Fe = wg.shape
    nf = Fe // tf

    def w_col(j, f, te, nu):
        return jnp.where(j < nu[0], f, nf - 1)

    return pl.pallas_call(
        _moe_group_kernel,
        grid_spec=pltpu.PrefetchScalarGridSpec(
            num_scalar_prefetch=2,
            grid=(n_rows // R, nf),
            in_specs=[
                pl.BlockSpec((R, D_MODEL), lambda j, f, te, nu: (jnp.maximum(jnp.minimum(j, nu[0] - 1), 0), 0)),
                pl.BlockSpec((None, D_MODEL, tf), lambda j, f, te, nu: (te[j], 0, w_col(j, f, te, nu))),
                pl.BlockSpec((None, D_MODEL, tf), lambda j, f, te, nu: (te[j], 0, w_col(j, f, te, nu))),
                pl.BlockSpec((None, tf, D_MODEL), lambda j, f, te, nu: (te[j], w_col(j, f, te, nu), 0)),
            ],
            out_specs=pl.BlockSpec((R, D_MODEL), lambda j, f, te, nu: (j, 0)),
            scratch_shapes=[pltpu.VMEM((R, D_MODEL), BF16), pltpu.VMEM((R, D_MODEL), F32)],
        ),
        out_shape=jax.ShapeDtypeStruct((n_rows, D_MODEL), F32),
        compiler_params=_cparams(("arbitrary", "arbitrary")),
        name="moe_group",
    )(tile_expert, n_used, xs, wg, wu, wd)


def _moe_combine_kernel(d1_ref, d2_ref, n1_ref, n2_ref, meta_ref, h3_ref, gf_ref, ys_hbm, out_ref,
                        y_scr, sems, *, tm):
    i = pl.program_id(0)
    slot = i % 2

    def gather(da_ref, db_ref, s):
        def body(r, c):
            _row_copy(ys_hbm, da_ref[0, r], y_scr.at[s, 0], r, sems.at[s]).start(priority=0)
            _row_copy(ys_hbm, db_ref[0, r], y_scr.at[s, 1], r, sems.at[s]).start(priority=1)
            return c

        lax.fori_loop(0, tm, body, 0, unroll=8)

    @pl.when(i == 0)
    def _():
        gather(d1_ref, d2_ref, 0)

    @pl.when(i + 1 < pl.num_programs(0))
    def _():
        gather(n1_ref, n2_ref, 1 - slot)

    for k in range(2):
        pltpu.make_async_copy(ys_hbm.at[pl.ds(0, tm)], y_scr.at[slot, k], sems.at[slot]).wait()
    meta = meta_ref[...]
    lane = lax.broadcasted_iota(jnp.int32, meta.shape, 1)
    g1 = jnp.sum(jnp.where(lane == META_G1, meta, 0.0), axis=-1, keepdims=True)
    g2 = jnp.sum(jnp.where(lane == META_G2, meta, 0.0), axis=-1, keepdims=True)
    out_ref[...] = _rms(h3_ref[...] + g1 * y_scr[slot, 0] + g2 * y_scr[slot, 1], gf_ref[...])


def _moe_combine(d1, d2, meta, h3, gf, ys, tm=256):
    T = h3.shape[0]
    n = T // tm
    smem = lambda nxt: pl.BlockSpec((None, 1, tm), lambda i: (jnp.minimum(i + nxt, n - 1), 0, 0),
                                    memory_space=pltpu.SMEM)
    d1 = d1.reshape(n, 1, tm)
    d2 = d2.reshape(n, 1, tm)
    return pl.pallas_call(
        functools.partial(_moe_combine_kernel, tm=tm),
        grid=(n,),
        in_specs=[
            smem(0), smem(0), smem(1), smem(1),
            pl.BlockSpec((tm, LANES), lambda i: (i, 0)),
            pl.BlockSpec((tm, D_MODEL), lambda i: (i, 0)),
            pl.BlockSpec((1, D_MODEL), lambda i: (0, 0)),
            pl.BlockSpec(memory_space=pl.ANY),
        ],
        out_specs=pl.BlockSpec((tm, D_MODEL), lambda i: (i, 0)),
        out_shape=jax.ShapeDtypeStruct((T, D_MODEL), F32),
        scratch_shapes=[pltpu.VMEM((2, 2, tm, D_MODEL), F32), pltpu.SemaphoreType.DMA((2,))],
        compiler_params=_cparams(("arbitrary",)),
        name="moe_combine",
    )(d1, d2, d1, d2, meta, h3, gf, ys)


def _moe_plan(meta, counts, n_tiles):
    R = MOE_ROWS
    cnt = counts[0, :N_EXPERTS].astype(jnp.int32)
    tiles = (cnt + R - 1) // R
    tile_end = jnp.cumsum(tiles)
    row_start = (tile_end - tiles) * R
    e1 = meta[:, META_E1].astype(jnp.int32)
    e2 = meta[:, META_E2].astype(jnp.int32)
    d1 = row_start[e1] + meta[:, META_R1].astype(jnp.int32)
    d2 = row_start[e2] + meta[:, META_R2].astype(jnp.int32)
    n_used = tile_end[-1:]
    j = jnp.minimum(jnp.arange(n_tiles, dtype=jnp.int32), n_used - 1)
    tile_expert = jnp.sum(j[:, None] >= tile_end[None, :], axis=1).astype(jnp.int32)
    return d1, d2, tile_expert, n_used.astype(jnp.int32)


def _take_cols(w, src):
    src = np.asarray(src)
    cols = jnp.take(w, jnp.asarray(np.maximum(src, 0)), axis=1)
    return jnp.where(jnp.asarray(src >= 0)[None, :], cols, 0.0)


def _mla_lane_of_dim():
    lane = np.zeros(MLA_QK, np.int64)
    for dd in range(MLA_NOPE):
        lane[dd] = 16 + dd if dd < 48 else 80 + (dd - 48)
    half = MLA_ROPE // 2
    for i in range(MLA_ROPE):
        lane[MLA_NOPE + i] = i if i < half else 64 + (i - half)
    return lane


def _dsw_lane_of(hh, dd):
    half = DSW_ROT // 2
    if dd < half:
        return hh * half + dd
    if dd < DSW_ROT:
        return 64 + hh * half + (dd - half)
    return (16 if hh == 0 else 80) + (dd - DSW_ROT)


def _rope_lane_tables(half, per_lane_index):
    inv_freq = jnp.power(jnp.float32(ROPE_THETA), -jnp.arange(half, dtype=F32) * (2.0 / (2 * half)))
    idx = np.zeros(LANES, np.int64)
    on = np.zeros(LANES, bool)
    sign = np.zeros(LANES, np.float32)
    for l in range(LANES):
        i = per_lane_index(l)
        if i is not None:
            idx[l], on[l] = i, True
            sign[l] = -1.0 if l < 64 else 1.0
    freq = jnp.where(jnp.asarray(on), jnp.take(inv_freq, jnp.asarray(idx)), 0.0)
    return freq.reshape(1, LANES), jnp.asarray(sign).reshape(1, LANES)


def kernel(x, positions, norm_attn, norm_ffn, mla_w_down, mla_q_norm, mla_w_uq, mla_kv_norm, mla_w_ukv,
           mla_w_o, dsw_kv_norm, dsw_w_kv, dsw_w_q, dsw_w_o, ffn_w_gate, ffn_w_up, ffn_w_down, moe_router,
           moe_w_gate, moe_w_up, moe_w_down, final_norm):
    B, S, D = x.shape
    T = B * S
    pos = positions.reshape(B, S, 1)

    lane_of = _mla_lane_of_dim()
    H = MLA_HEADS
    src_dn = np.full(MLA_Q_LORA + MLA_KV_LORA + LANES, -1, np.int64)
    src_dn[:MLA_Q_LORA + MLA_KV_LORA] = np.arange(MLA_Q_LORA + MLA_KV_LORA)
    src_uq = np.full(H * LANES, -1, np.int64)
    src_k = np.full(H * LANES, -1, np.int64)
    src_v = np.zeros(H * MLA_V, np.int64)
    for i in range(MLA_ROPE):
        src_dn[MLA_Q_LORA + MLA_KV_LORA + lane_of[MLA_NOPE + i]] = MLA_Q_LORA + MLA_KV_LORA + i
    for h in range(H):
        for dd in range(MLA_QK):
            src_uq[h * LANES + lane_of[dd]] = h * MLA_QK + dd
        for dd in range(MLA_NOPE):
            src_k[h * LANES + lane_of[dd]] = h * (MLA_NOPE + MLA_V) + dd
        for dd in range(MLA_V):
            src_v[h * MLA_V + dd] = h * (MLA_NOPE + MLA_V) + MLA_NOPE + dd
    wdn = _take_cols(mla_w_down[0], src_dn).astype(BF16)
    wuq = _take_cols(mla_w_uq[0], src_uq).astype(BF16)
    wk = _take_cols(mla_w_ukv[0], src_k).astype(BF16)
    wv = _take_cols(mla_w_ukv[0], src_v).astype(BF16)
    half = MLA_ROPE // 2
    mla_freq, mla_sign = _rope_lane_tables(
        half, lambda l: l if l < half else (l - 64 if 64 <= l < 64 + half else None))

    q, k, v = _mla_proj(x, pos, norm_attn[0:1], wdn, mla_q_norm[0:1], mla_kv_norm[0:1], wuq, wk, wv,
                        mla_freq, mla_sign, tm=MLA_TILE)
    o = _mla_attn(q, k, v)
    h = _attn_ffn(x.reshape(T, D), o.reshape(T, D), mla_w_o[0].astype(BF16), norm_ffn[0:1],
                  ffn_w_gate[0].astype(BF16), ffn_w_up[0].astype(BF16), ffn_w_down[0].astype(BF16))

    src_pair = np.zeros(DSW_WIDTH, np.int64)
    for j in range(DSW_HEADS // 2):
        for hh in range(2):
            for dd in range(DSW_HD):
                src_pair[j * LANES + _dsw_lane_of(hh, dd)] = (2 * j + hh) * DSW_HD + dd
    wq_b, wk_b, wv_b = [], [], []
    for g in range(N_BR):
        wq_b.append(_take_cols(dsw_w_q[0], g * DSW_WIDTH + src_pair))
        wk_b.append(_take_cols(dsw_w_kv, g * DSW_WIDTH + src_pair))
        wv_b.append(dsw_w_kv[:, (N_BR + g) * DSW_WIDTH:(N_BR + g + 1) * DSW_WIDTH])
    wq_b = jnp.stack(wq_b).astype(BF16)
    wk_b = jnp.stack(wk_b).astype(BF16)
    wv_b = jnp.stack(wv_b).astype(BF16)
    half = DSW_ROT // 2
    dsw_freq, dsw_sign = _rope_lane_tables(
        half, lambda l: l % half if l < 2 * half else ((l - 64) % half if 64 <= l < 64 + 2 * half else None))

    qkv = _dsw_proj(h.reshape(B, S, D), pos, norm_attn[1:2], dsw_kv_norm.reshape(1, D), wq_b, wk_b, wv_b,
                    dsw_freq, dsw_sign)
    o = _dsw_attn(qkv)
    wr = jnp.pad(moe_router[0], ((0, 0), (0, LANES - N_EXPERTS)))
    wr_hi = wr.astype(BF16)
    wr = jnp.concatenate([wr_hi, (wr - wr_hi.astype(F32)).astype(BF16)], axis=1)
    h3, hn, meta, counts = _dsw_router(h, o.reshape(T, D), dsw_w_o[0].astype(BF16), norm_ffn[1:2], wr)

    n_tiles = (2 * T) // MOE_ROWS + N_EXPERTS
    d1, d2, tile_expert, n_used = _moe_plan(meta, counts, n_tiles)
    xs = _moe_dispatch(d1, d2, hn, n_tiles * MOE_ROWS)
    ys = _moe_group(tile_expert, n_used, xs, moe_w_gate[0].astype(BF16), moe_w_up[0].astype(BF16),
                    moe_w_down[0].astype(BF16))
    out = _moe_combine(d1, d2, meta, h3, final_norm.reshape(1, D), ys)
    return out.reshape(B, S, D)
```

```python
import functools
import math

import numpy as np
import jax
import jax.numpy as jnp
from jax import lax
from jax.experimental import pallas as pl
from jax.experimental.pallas import tpu as pltpu

F32 = jnp.float32
BF16 = jnp.bfloat16

D_MODEL = 1024
RMS_EPS = 1e-6
ROPE_THETA = 500000.0
NEG_INF = -1e30
LANES = 128

MLA_HEADS = 16
MLA_NOPE = 64
MLA_ROPE = 32
MLA_V = 64
MLA_QK = MLA_NOPE + MLA_ROPE
MLA_Q_LORA = 384
MLA_KV_LORA = 256

DSW_BRANCHES = ((128, 1), (512, 4), (2048, 16))
N_BR = 3
DSW_HEADS = 16
DSW_HD = 64
DSW_ROT = 16
DSW_WIDTH = DSW_HEADS * DSW_HD
DSW_BLOCK = 128

FFN_DIM = 2816
N_EXPERTS = 8
EXPERT_DIM = 3584

VMEM_LIMIT = 56 * 1024 * 1024


def _cparams(sem):
    return pltpu.CompilerParams(dimension_semantics=sem, vmem_limit_bytes=VMEM_LIMIT)


def _resident(shape):
    nd = len(shape)
    return pl.BlockSpec(shape, lambda *_: (0,) * nd, pipeline_mode=pl.Buffered(1))


def _rms(x, g):
    return x * lax.rsqrt(jnp.mean(x * x, axis=-1, keepdims=True) + RMS_EPS) * g


def _dot(a, b):
    return jnp.dot(a, b, preferred_element_type=F32)


def _dot_nt(a, b):
    return lax.dot_general(a, b, (((1,), (1,)), ((), ())), preferred_element_type=F32)


def _rope(t, cos, sin_signed):
    return t * cos + pltpu.roll(t, 64, 1) * sin_signed


def _mla_proj_kernel(x_ref, pos_ref, g_ref, wdn_ref, qn_ref, kvn_ref, wuq_ref, wk_ref, wv_ref,
                     freq_ref, sign_ref, q_ref, k_ref, v_ref):
    xn = _rms(x_ref[...], g_ref[...]).astype(BF16)
    down = _dot(xn, wdn_ref[...])
    cq = _rms(down[:, :MLA_Q_LORA], qn_ref[...]).astype(BF16)
    ckv = _rms(down[:, MLA_Q_LORA:MLA_Q_LORA + MLA_KV_LORA], kvn_ref[...]).astype(BF16)
    kr = down[:, MLA_Q_LORA + MLA_KV_LORA:]
    ang = pos_ref[...].astype(F32) * freq_ref[...]
    cos = jnp.cos(ang)
    sin = jnp.sin(ang) * sign_ref[...]
    kr = _rope(kr, cos, sin)
    scale = LOG2E / math.sqrt(MLA_QK)
    cos_q = cos * scale
    sin_q = sin * scale
    q = _dot(cq, wuq_ref[...])
    for h in range(MLA_HEADS):
        q_ref[h] = _rope(q[:, h * LANES:(h + 1) * LANES], cos_q, sin_q).T.astype(BF16)
    kp = _dot(ckv, wk_ref[...])
    for h in range(MLA_HEADS):
        k_ref[h] = (kp[:, h * LANES:(h + 1) * LANES] + kr).astype(BF16)
    v = _dot(ckv, wv_ref[...])
    for j in range(MLA_HEADS // 2):
        v_ref[j] = v[:, j * LANES:(j + 1) * LANES].T.astype(BF16)


def _mla_proj(x, pos, g, wdn, qn, kvn, wuq, wk, wv, freq, sign, tm):
    B, S, _ = x.shape
    H = MLA_HEADS
    return pl.pallas_call(
        _mla_proj_kernel,
        grid=(B, S // tm),
        in_specs=[
            pl.BlockSpec((None, tm, D_MODEL), lambda b, i: (b, i, 0)),
            pl.BlockSpec((None, tm, 1), lambda b, i: (b, i, 0)),
            _resident(g.shape), _resident(wdn.shape), _resident(qn.shape), _resident(kvn.shape),
            _resident(wuq.shape), _resident(wk.shape), _resident(wv.shape),
            _resident(freq.shape), _resident(sign.shape),
        ],
        out_specs=[
            pl.BlockSpec((None, H, None, LANES, tm), lambda b, i: (b, 0, i, 0, 0)),
            pl.BlockSpec((None, H, tm, LANES), lambda b, i: (b, 0, i, 0)),
            pl.BlockSpec((None, H // 2, None, LANES, tm), lambda b, i: (b, 0, i, 0, 0)),
        ],
        out_shape=[
            jax.ShapeDtypeStruct((B, H, S // tm, LANES, tm), BF16),
            jax.ShapeDtypeStruct((B, H, S, LANES), BF16),
            jax.ShapeDtypeStruct((B, H // 2, S // tm, LANES, tm), BF16),
        ],
        compiler_params=_cparams(("parallel", "parallel")),
        name="mla_proj",
    )(x, pos, g, wdn, qn, kvn, wuq, wk, wv, freq, sign)


MLA_TILE = 512
MLA_ONES = 16
LOG2E = 1.4426950408889634

def _mla_attn_kernel(qT_ref, k_ref, vT_ref, o_ref, m_scr, acc_scr, s_scr, *, t):
    i = pl.program_id(2)
    m_scr[...] = jnp.full(m_scr.shape, NEG_INF, F32)
    acc_scr[...] = jnp.zeros(acc_scr.shape, F32)
    ones = jnp.ones((MLA_ONES, t), BF16)

    def scores(c, hh):
        return _dot(k_ref[hh, pl.ds(pl.multiple_of(c * t, t), t), :], qT_ref[hh])

    def update(s, c, hh):
        m_prev = m_scr[hh]
        m_new = jnp.maximum(m_prev, jnp.max(s, axis=0, keepdims=True))
        alpha = jnp.exp2(m_prev - m_new)
        p = jnp.exp2(s - m_new).astype(BF16)
        lhs = jnp.concatenate([vT_ref[c, hh * MLA_V:(hh + 1) * MLA_V, :], ones], axis=0)
        acc_scr[hh] = alpha * acc_scr[hh] + _dot(lhs, p)
        m_scr[hh] = m_new

    for hh in range(2):
        s_scr[hh] = scores(0, hh)

    def step(c):
        for hh in range(2):
            s = s_scr[hh]
            s_scr[hh] = scores(c + 1, hh)
            update(s, c, hh)

    def body(c2, carry):
        step(2 * c2)
        step(2 * c2 + 1)
        return carry

    lax.fori_loop(0, i // 2, body, 0)

    @pl.when(i % 2 == 1)
    def _():
        step(i - 1)

    key = lax.broadcasted_iota(jnp.int32, (t, t), 0)
    qry = lax.broadcasted_iota(jnp.int32, (t, t), 1)
    for hh in range(2):
        update(jnp.where(key <= qry, s_scr[hh], NEG_INF), i, hh)
    o = [acc_scr[hh, :MLA_V, :] * (1.0 / acc_scr[hh, MLA_V:MLA_V + 1, :]) for hh in range(2)]
    o_ref[...] = jnp.concatenate(o, axis=0).T.astype(BF16)


def _mla_attn(qT, k, vT):
    B, H, n, _, t = qT.shape
    S = n * t
    return pl.pallas_call(
        functools.partial(_mla_attn_kernel, t=t),
        grid=(B, H // 2, n),
        in_specs=[
            pl.BlockSpec((None, 2, None, LANES, t), lambda b, j, i: (b, j, i, 0, 0)),
            pl.BlockSpec((None, 2, S, LANES), lambda b, j, i: (b, j, 0, 0)),
            pl.BlockSpec((None, None, n, LANES, t), lambda b, j, i: (b, j, 0, 0, 0)),
        ],
        out_specs=pl.BlockSpec((None, t, LANES), lambda b, j, i: (b, i, j)),
        out_shape=jax.ShapeDtypeStruct((B, S, (H // 2) * LANES), BF16),
        scratch_shapes=[pltpu.VMEM((2, 1, t), F32), pltpu.VMEM((2, MLA_V + MLA_ONES, t), F32),
                        pltpu.VMEM((2, t, t), F32)],
        compiler_params=_cparams(("parallel", "parallel", "parallel")),
        name="mla_attn",
    )(qT, k, vT)


def _swiglu_acc(hn, wg_ref, wu_ref, wd_ref, chunk, scale=None):
    acc = None
    n = wg_ref.shape[-1]
    for c0 in range(0, n, chunk):
        g = _dot(hn, wg_ref[:, c0:c0 + chunk])
        u = _dot(hn, wu_ref[:, c0:c0 + chunk])
        a = g * jax.nn.sigmoid(g) * u
        if scale is not None:
            a = a * scale
        part = _dot(a.astype(BF16), wd_ref[c0:c0 + chunk, :])
        acc = part if acc is None else acc + part
    return acc


def _attn_ffn_kernel(x_ref, o_ref, wo_ref, g_ref, wg_ref, wu_ref, wd_ref, h_ref):
    h1 = x_ref[...] + _dot(o_ref[...], wo_ref[...])
    hn = _rms(h1, g_ref[...]).astype(BF16)
    h_ref[...] = h1 + _swiglu_acc(hn, wg_ref, wu_ref, wd_ref, FFN_DIM // 2)


def _attn_ffn(x, o, wo, g, wg, wu, wd, tm=512):
    T = x.shape[0]
    return pl.pallas_call(
        _attn_ffn_kernel,
        grid=(T // tm,),
        in_specs=[
            pl.BlockSpec((tm, D_MODEL), lambda i: (i, 0)),
            pl.BlockSpec((tm, D_MODEL), lambda i: (i, 0)),
            _resident(wo.shape), _resident(g.shape), _resident(wg.shape), _resident(wu.shape),
            _resident(wd.shape),
        ],
        out_specs=pl.BlockSpec((tm, D_MODEL), lambda i: (i, 0)),
        out_shape=jax.ShapeDtypeStruct((T, D_MODEL), F32),
        compiler_params=_cparams(("parallel",)),
        name="attn_ffn",
    )(x, o, wo, g, wg, wu, wd)


def _dsw_proj_kernel(h_ref, pos_ref, gq_ref, gkv_ref, wq_ref, wk_ref, wv_ref, freq_ref, sign_ref,
                     q1, k1, v1, q2, k2, v2, q3, k3, v3,
                     xq_scr, xkv_scr, cos_scr, sin_scr, pq_scr, pkv_scr, pcos_scr, psin_scr, *, tm):
    h = h_ref[...]
    r = lax.rsqrt(jnp.mean(h * h, axis=-1, keepdims=True) + RMS_EPS)
    hr = h * r
    nblk = D_MODEL // LANES
    xq_f = hr * gq_ref[...]
    xkv_f = hr * gkv_ref[...]
    for cb in range(nblk):
        xq_scr[cb] = xq_f[:, cb * LANES:(cb + 1) * LANES]
        xkv_scr[cb] = xkv_f[:, cb * LANES:(cb + 1) * LANES]
    ang = pos_ref[...].astype(F32) * freq_ref[...]
    cos_scr[...] = jnp.cos(ang)
    sin_scr[...] = jnp.sin(ang) * sign_ref[...]
    scale = LOG2E / math.sqrt(DSW_HD)
    outs = ((q1, k1, v1), (q2, k2, v2), (q3, k3, v3))
    for g, (_, d) in enumerate(DSW_BRANCHES):
        n = tm // d
        q_out, k_out, v_out = outs[g]
        if d == 1:
            xq = xq_f.astype(BF16)
            xkv = xkv_f.astype(BF16)
            cos = cos_scr[...]
            sin = sin_scr[...]
        else:
            for rr in range(d):
                rows = slice(rr * n, (rr + 1) * n)
                for cb in range(nblk):
                    cols = slice(cb * LANES, (cb + 1) * LANES)
                    pq_scr[rows, cols] = xq_scr[cb, pl.ds(rr, n, stride=d), :].astype(BF16)
                    pkv_scr[rows, cols] = xkv_scr[cb, pl.ds(rr, n, stride=d), :].astype(BF16)
                pcos_scr[rows, :] = cos_scr[pl.ds(rr, n, stride=d), :]
                psin_scr[rows, :] = sin_scr[pl.ds(rr, n, stride=d), :]
            xq = pq_scr[...]
            xkv = pkv_scr[...]
            cos = pcos_scr[...]
            sin = psin_scr[...]
        q = _dot(xq, wq_ref[g])
        k = _dot(xkv, wk_ref[g])
        v = _dot(xkv, wv_ref[g])
        cos_q = cos * scale
        sin_q = sin * scale
        for j in range(DSW_HEADS // 2):
            sl = slice(j * LANES, (j + 1) * LANES)
            qj = _rope(q[:, sl], cos_q, sin_q).astype(BF16)
            kj = _rope(k[:, sl], cos, sin).astype(BF16)
            vj = v[:, sl].astype(BF16)
            for rr in range(d):
                q_out[j, rr] = qj[rr * n:(rr + 1) * n]
                k_out[j, rr] = kj[rr * n:(rr + 1) * n]
                v_out[j, rr] = vj[rr * n:(rr + 1) * n]


def _dsw_proj(h, pos, gq, gkv, wq, wk, wv, freq, sign, tm=512):
    B, S, _ = h.shape
    P = DSW_HEADS // 2
    out_specs, out_shape = [], []
    for _, d in DSW_BRANCHES:
        for _ in range(3):
            out_specs.append(pl.BlockSpec((None, P, d, tm // d, LANES), lambda b, i: (b, 0, 0, i, 0)))
            out_shape.append(jax.ShapeDtypeStruct((B, P, d, S // d, LANES), BF16))
    return pl.pallas_call(
        functools.partial(_dsw_proj_kernel, tm=tm),
        grid=(B, S // tm),
        in_specs=[
            pl.BlockSpec((None, tm, D_MODEL), lambda b, i: (b, i, 0)),
            pl.BlockSpec((None, tm, 1), lambda b, i: (b, i, 0)),
            _resident(gq.shape), _resident(gkv.shape), _resident(wq.shape), _resident(wk.shape),
            _resident(wv.shape), _resident(freq.shape), _resident(sign.shape),
        ],
        out_specs=out_specs,
        out_shape=out_shape,
        scratch_shapes=[
            pltpu.VMEM((D_MODEL // LANES, tm, LANES), F32), pltpu.VMEM((D_MODEL // LANES, tm, LANES), F32),
            pltpu.VMEM((tm, LANES), F32), pltpu.VMEM((tm, LANES), F32),
            pltpu.VMEM((tm, D_MODEL), BF16), pltpu.VMEM((tm, D_MODEL), BF16),
            pltpu.VMEM((tm, LANES), F32), pltpu.VMEM((tm, LANES), F32),
        ],
        compiler_params=_cparams(("parallel", "parallel")),
        name="dsw_proj",
    )(h, pos, gq, gkv, wq, wk, wv, freq, sign)


def _unroll_of(n):
    for u in (16, 15, 12, 8, 6, 5, 4, 3, 2):
        if n % u == 0:
            return u
    return 1


def _dsw_attn_kernel(q1, k1, v1, q2, k2, v2, q3, k3, v3, o_ref,
                     o1_scr, l1_scr, o2_scr, l2_scr, o3_scr, l3_scr, bo_scr, bl_scr, s_scr, *, S):
    blk = DSW_BLOCK
    lane = lax.broadcasted_iota(jnp.int32, (blk, LANES), 1)
    mask_a = (lane < 8) | ((lane >= 16) & (lane < 72))
    first_half = lane < DSW_HD
    row1 = lax.broadcasted_iota(jnp.int32, (2 * blk, blk), 0)
    col1 = lax.broadcasted_iota(jnp.int32, (2 * blk, blk), 1)
    row1 = jnp.where(row1 >= blk, row1 - blk, row1)
    valid_first = col1 <= row1
    row2 = lax.broadcasted_iota(jnp.int32, (2 * blk, 2 * blk), 0)
    col2 = lax.broadcasted_iota(jnp.int32, (2 * blk, 2 * blk), 1)
    row2 = jnp.where(row2 >= blk, row2 - blk, row2)
    valid_next = ((col2 < blk) & (col2 >= row2)) | ((col2 >= blk) & (col2 - blk <= row2))

    def scores(q_ref, k_ref, off, win):
        qf = q_ref[pl.ds(pl.multiple_of(off, blk), blk), :].astype(F32)
        q2 = jnp.concatenate([jnp.where(mask_a, qf, 0.0), jnp.where(mask_a, 0.0, qf)], axis=0).astype(BF16)
        return _dot_nt(q2, k_ref[win, :])

    def finish(s, v_ref, off, win, valid, o_dst, l_dst):
        s = jnp.where(valid, s, NEG_INF)
        m = jnp.max(s, axis=-1, keepdims=True)
        p = jnp.exp2(s - m)
        l = jnp.sum(p, axis=-1, keepdims=True)
        o2 = _dot(p.astype(BF16), v_ref[win, :]) / l
        lse = m * (1.0 / LOG2E) + jnp.log(l)
        sl = pl.ds(pl.multiple_of(off, blk), blk)
        o_dst[sl, :] = jnp.where(first_half, o2[:blk], o2[blk:])
        l_dst[sl, :] = jnp.where(first_half, lse[:blk], lse[blk:])

    def run_blocks(n, locate, valid, q_ref, k_ref, v_ref, o_dst, l_dst):
        nk = valid.shape[1]
        s_scr[:, :nk] = scores(q_ref, k_ref, *locate(0))

        def body(i, c):
            s = s_scr[:, :nk]
            s_scr[:, :nk] = scores(q_ref, k_ref, *locate(jnp.minimum(i + 1, n - 1)))
            off, win = locate(i)
            finish(s, v_ref, off, win, valid, o_dst, l_dst)
            return c

        lax.fori_loop(0, n, body, 0, unroll=_unroll_of(n))

    branches = ((q1, k1, v1, o1_scr, l1_scr, o1_scr, l1_scr),
                (q2, k2, v2, bo_scr, bl_scr, o2_scr, l2_scr),
                (q3, k3, v3, bo_scr, bl_scr, o3_scr, l3_scr))
    for (_, d), (q_ref, k_ref, v_ref, o_dst, l_dst, o_tok, l_tok) in zip(DSW_BRANCHES, branches):
        L = S // d
        nb = L // blk

        def locate_first(rr, L=L):
            off = rr * L
            return off, pl.ds(pl.multiple_of(off, blk), blk)

        def locate_next(idx, L=L, nb=nb):
            rr = idx // (nb - 1)
            off = rr * L + (idx - rr * (nb - 1) + 1) * blk
            return off, pl.ds(pl.multiple_of(off - blk, blk), 2 * blk)

        run_blocks(d, locate_first, valid_first, q_ref, k_ref, v_ref, o_dst, l_dst)
        if nb > 1:
            run_blocks(d * (nb - 1), locate_next, valid_next, q_ref, k_ref, v_ref, o_dst, l_dst)
        if d > 1:
            for rr in range(d):
                o_tok[pl.ds(rr, L, stride=d), :] = o_dst[rr * L:(rr + 1) * L, :]
                l_tok[pl.ds(rr, L, stride=d), :] = l_dst[rr * L:(rr + 1) * L, :]

    ch = 256
    for c0 in range(0, S, ch):
        sl = slice(c0, c0 + ch)
        la, lb, lc = l1_scr[sl, :], l2_scr[sl, :], l3_scr[sl, :]
        m = jnp.maximum(jnp.maximum(la, lb), lc)
        wa, wb, wc = jnp.exp(la - m), jnp.exp(lb - m), jnp.exp(lc - m)
        num = wa * o1_scr[sl, :] + wb * o2_scr[sl, :] + wc * o3_scr[sl, :]
        o_ref[sl, :] = (num / (wa + wb + wc)).astype(BF16)


def _dsw_attn(qkv):
    B, P = qkv[0].shape[0], qkv[0].shape[1]
    S = qkv[0].shape[2] * qkv[0].shape[3]
    flat = [a.reshape(B, P, S, LANES) for a in qkv]
    spec = pl.BlockSpec((None, None, S, LANES), lambda b, j: (b, j, 0, 0))
    return pl.pallas_call(
        functools.partial(_dsw_attn_kernel, S=S),
        grid=(B, P),
        in_specs=[spec] * 9,
        out_specs=pl.BlockSpec((None, S, LANES), lambda b, j: (b, 0, j)),
        out_shape=jax.ShapeDtypeStruct((B, S, P * LANES), BF16),
        scratch_shapes=[pltpu.VMEM((S, LANES), F32)] * 8 + [pltpu.VMEM((2 * DSW_BLOCK, 2 * DSW_BLOCK), F32)],
        compiler_params=_cparams(("parallel", "parallel")),
        name="dsw_attn",
    )(*flat)


META_E1, META_E2, META_R1, META_R2, META_G1, META_G2 = range(6)


def _dsw_router_kernel(h_ref, o_ref, wo_ref, g_ref, wr_ref, h3_ref, hn_ref, meta_ref, cnt_ref, base_scr):
    @pl.when(pl.program_id(0) == 0)
    def _():
        base_scr[...] = jnp.zeros_like(base_scr)

    h3 = h_ref[...] + _dot(o_ref[...], wo_ref[...])
    h3_ref[...] = h3
    hn = _rms(h3, g_ref[...])
    hn_ref[...] = hn
    hi = hn.astype(BF16)
    lo = (hn - hi.astype(F32)).astype(BF16)
    part = _dot(hi, wr_ref[...])
    logits = part[:, :LANES] + part[:, LANES:] + _dot(lo, wr_ref[:, :LANES])
    tm = logits.shape[0]
    lane = lax.broadcasted_iota(jnp.int32, logits.shape, 1).astype(F32)
    logits = jnp.where(lane < N_EXPERTS, logits, -jnp.inf)
    v1 = jnp.max(logits, axis=-1, keepdims=True)
    i1 = jnp.min(jnp.where(logits == v1, lane, float(LANES)), axis=-1, keepdims=True)
    rest = jnp.where(lane == i1, -jnp.inf, logits)
    v2 = jnp.max(rest, axis=-1, keepdims=True)
    i2 = jnp.min(jnp.where(rest == v2, lane, float(LANES)), axis=-1, keepdims=True)
    e2 = jnp.exp(v2 - v1)
    den = 1.0 + e2
    onehot = jnp.where((lane == i1) | (lane == i2), 1.0, 0.0)
    row = lax.broadcasted_iota(jnp.int32, (tm, tm), 0)
    col = lax.broadcasted_iota(jnp.int32, (tm, tm), 1)
    earlier = jnp.where(col < row, 1.0, 0.0).astype(BF16)
    before = _dot(earlier, onehot.astype(BF16)) + base_scr[0:1, :]
    r1 = jnp.sum(jnp.where(lane == i1, before, 0.0), axis=-1, keepdims=True)
    r2 = jnp.sum(jnp.where(lane == i2, before, 0.0), axis=-1, keepdims=True)
    total = base_scr[0:1, :] + jnp.sum(onehot, axis=0, keepdims=True)
    base_scr[...] = jnp.broadcast_to(total, base_scr.shape)
    cnt_ref[...] = jnp.broadcast_to(total, cnt_ref.shape)
    meta = jnp.zeros_like(logits)
    for k, val in ((META_E1, i1), (META_E2, i2), (META_R1, r1), (META_R2, r2),
                   (META_G1, 1.0 / den), (META_G2, e2 / den)):
        meta = jnp.where(lane == float(k), val, meta)
    meta_ref[...] = meta


def _dsw_router(h, o, wo, g, wr, tm=512):
    T = h.shape[0]
    return pl.pallas_call(
        _dsw_router_kernel,
        grid=(T // tm,),
        in_specs=[
            pl.BlockSpec((tm, D_MODEL), lambda i: (i, 0)),
            pl.BlockSpec((tm, D_MODEL), lambda i: (i, 0)),
            _resident(wo.shape), _resident(g.shape), _resident(wr.shape),
        ],
        out_specs=[
            pl.BlockSpec((tm, D_MODEL), lambda i: (i, 0)),
            pl.BlockSpec((tm, D_MODEL), lambda i: (i, 0)),
            pl.BlockSpec((tm, LANES), lambda i: (i, 0)),
            pl.BlockSpec((8, LANES), lambda i: (0, 0)),
        ],
        out_shape=[
            jax.ShapeDtypeStruct((T, D_MODEL), F32),
            jax.ShapeDtypeStruct((T, D_MODEL), F32),
            jax.ShapeDtypeStruct((T, LANES), F32),
            jax.ShapeDtypeStruct((8, LANES), F32),
        ],
        scratch_shapes=[pltpu.VMEM((8, LANES), F32)],
        compiler_params=_cparams(("arbitrary",)),
        name="dsw_router",
    )(h, o, wo, g, wr)


MOE_ROWS = 512


def _row_copy(src, src_row, dst, dst_row, sem):
    return pltpu.make_async_copy(src.at[pl.ds(src_row, 1)], dst.at[pl.ds(dst_row, 1)], sem)


def _moe_dispatch_kernel(d1_ref, d2_ref, hn_ref, xs_in_hbm, xs_hbm, sem, *, tm):
    del xs_in_hbm

    def body(r, c):
        _row_copy(hn_ref, r, xs_hbm, d1_ref[0, r], sem).start(priority=0)
        _row_copy(hn_ref, r, xs_hbm, d2_ref[0, r], sem).start(priority=1)
        return c

    lax.fori_loop(0, tm, body, 0, unroll=8)
    for _ in range(2):
        pltpu.make_async_copy(hn_ref, xs_hbm.at[pl.ds(0, tm)], sem).wait()


def _moe_dispatch(d1, d2, hn, n_rows, tm=512):
    T = hn.shape[0]
    smem = lambda: pl.BlockSpec((None, 1, tm), lambda i: (i, 0, 0), memory_space=pltpu.SMEM)
    return pl.pallas_call(
        functools.partial(_moe_dispatch_kernel, tm=tm),
        grid=(T // tm,),
        in_specs=[smem(), smem(), pl.BlockSpec((tm, D_MODEL), lambda i: (i, 0)),
                  pl.BlockSpec(memory_space=pl.ANY)],
        out_specs=pl.BlockSpec(memory_space=pl.ANY),
        out_shape=jax.ShapeDtypeStruct((n_rows, D_MODEL), F32),
        scratch_shapes=[pltpu.SemaphoreType.DMA],
        input_output_aliases={3: 0},
        compiler_params=_cparams(("arbitrary",)),
        name="moe_dispatch",
    )(d1.reshape(T // tm, 1, tm), d2.reshape(T // tm, 1, tm), hn, jnp.zeros((n_rows, D_MODEL), F32))


def _moe_group_kernel(te_ref, nu_ref, xs_ref, wg_ref, wu_ref, wd_ref, ys_ref, xb_scr, acc_scr):
    j = pl.program_id(0)
    f = pl.program_id(1)
    used = j < nu_ref[0]

    @pl.when(used & (f == 0))
    def _():
        xb_scr[...] = xs_ref[...].astype(BF16)
        acc_scr[...] = jnp.zeros_like(acc_scr)

    @pl.when(used)
    def _():
        acc_scr[...] += _swiglu_acc(xb_scr[...], wg_ref, wu_ref, wd_ref, wg_ref.shape[-1] // 2)

    @pl.when(f == pl.num_programs(1) - 1)
    def _():
        ys_ref[...] = jnp.where(used, acc_scr[...], 0.0)


def _moe_group(tile_expert, n_used, xs, wg, wu, wd, tf=1792):
    n_rows = xs.shape[0]
    R = MOE_ROWS
    E, _, Fe = wg.shape
    nf = Fe // tf

    def w_col(j, f, te, nu):
        return jnp.where(j < nu[0], f, nf - 1)

    return pl.pallas_call(
        _moe_group_kernel,
        grid_spec=pltpu.PrefetchScalarGridSpec(
            num_scalar_prefetch=2,
            grid=(n_rows // R, nf),
            in_specs=[
                pl.BlockSpec((R, D_MODEL), lambda j, f, te, nu: (jnp.maximum(jnp.minimum(j, nu[0] - 1), 0), 0)),
                pl.BlockSpec((None, D_MODEL, tf), lambda j, f, te, nu: (te[j], 0, w_col(j, f, te, nu))),
                pl.BlockSpec((None, D_MODEL, tf), lambda j, f, te, nu: (te[j], 0, w_col(j, f, te, nu))),
                pl.BlockSpec((None, tf, D_MODEL), lambda j, f, te, nu: (te[j], w_col(j, f, te, nu), 0)),
            ],
            out_specs=pl.BlockSpec((R, D_MODEL), lambda j, f, te, nu: (j, 0)),
            scratch_shapes=[pltpu.VMEM((R, D_MODEL), BF16), pltpu.VMEM((R, D_MODEL), F32)],
        ),
        out_shape=jax.ShapeDtypeStruct((n_rows, D_MODEL), F32),
        compiler_params=_cparams(("arbitrary", "arbitrary")),
        name="moe_group",
    )(tile_expert, n_used, xs, wg, wu, wd)


def _moe_combine_kernel(d1_ref, d2_ref, n1_ref, n2_ref, meta_ref, h3_ref, gf_ref, ys_hbm, out_ref,
                        y_scr, sems, *, tm):
    i = pl.program_id(0)
    slot = i % 2

    def gather(da_ref, db_ref, s):
        def body(r, c):
            _row_copy(ys_hbm, da_ref[0, r], y_scr.at[s, 0], r, sems.at[s]).start(priority=0)
            _row_copy(ys_hbm, db_ref[0, r], y_scr.at[s, 1], r, sems.at[s]).start(priority=1)
            return c

        lax.fori_loop(0, tm, body, 0, unroll=8)

    @pl.when(i == 0)
    def _():
        gather(d1_ref, d2_ref, 0)

    @pl.when(i + 1 < pl.num_programs(0))
    def _():
        gather(n1_ref, n2_ref, 1 - slot)

    for k in range(2):
        pltpu.make_async_copy(ys_hbm.at[pl.ds(0, tm)], y_scr.at[slot, k], sems.at[slot]).wait()
    meta = meta_ref[...]
    lane = lax.broadcasted_iota(jnp.int32, meta.shape, 1)
    g1 = jnp.sum(jnp.where(lane == META_G1, meta, 0.0), axis=-1, keepdims=True)
    g2 = jnp.sum(jnp.where(lane == META_G2, meta, 0.0), axis=-1, keepdims=True)
    out_ref[...] = _rms(h3_ref[...] + g1 * y_scr[slot, 0] + g2 * y_scr[slot, 1], gf_ref[...])


def _moe_combine(d1, d2, meta, h3, gf, ys, tm=256):
    T = h3.shape[0]
    n = T // tm
    smem = lambda nxt: pl.BlockSpec((None, 1, tm), lambda i: (jnp.minimum(i + nxt, n - 1), 0, 0),
                                    memory_space=pltpu.SMEM)
    d1 = d1.reshape(n, 1, tm)
    d2 = d2.reshape(n, 1, tm)
    return pl.pallas_call(
        functools.partial(_moe_combine_kernel, tm=tm),
        grid=(n,),
        in_specs=[
            smem(0), smem(0), smem(1), smem(1),
            pl.BlockSpec((tm, LANES), lambda i: (i, 0)),
            pl.BlockSpec((tm, D_MODEL), lambda i: (i, 0)),
            pl.BlockSpec((1, D_MODEL), lambda i: (0, 0)),
            pl.BlockSpec(memory_space=pl.ANY),
        ],
        out_specs=pl.BlockSpec((tm, D_MODEL), lambda i: (i, 0)),
        out_shape=jax.ShapeDtypeStruct((T, D_MODEL), F32),
        scratch_shapes=[pltpu.VMEM((2, 2, tm, D_MODEL), F32), pltpu.SemaphoreType.DMA((2,))],
        compiler_params=_cparams(("arbitrary",)),
        name="moe_combine",
    )(d1, d2, d1, d2, meta, h3, gf, ys)


def _moe_plan(meta, counts, n_tiles):
    R = MOE_ROWS
    cnt = counts[0, :N_EXPERTS].astype(jnp.int32)
    tiles = (cnt + R - 1) // R
    tile_end = jnp.cumsum(tiles)
    row_start = (tile_end - tiles) * R
    e1 = meta[:, META_E1].astype(jnp.int32)
    e2 = meta[:, META_E2].astype(jnp.int32)
    d1 = row_start[e1] + meta[:, META_R1].astype(jnp.int32)
    d2 = row_start[e2] + meta[:, META_R2].astype(jnp.int32)
    n_used = tile_end[-1:]
    j = jnp.minimum(jnp.arange(n_tiles, dtype=jnp.int32), n_used - 1)
    tile_expert = jnp.sum(j[:, None] >= tile_end[None, :], axis=1).astype(jnp.int32)
    return d1, d2, tile_expert, n_used.astype(jnp.int32)


def _take_cols(w, src):
    src = np.asarray(src)
    cols = jnp.take(w, jnp.asarray(np.maximum(src, 0)), axis=1)
    return jnp.where(jnp.asarray(src >= 0)[None, :], cols, 0.0)


def _mla_lane_of_dim():
    lane = np.zeros(MLA_QK, np.int64)
    for dd in range(MLA_NOPE):
        lane[dd] = 16 + dd if dd < 48 else 80 + (dd - 48)
    half = MLA_ROPE // 2
    for i in range(MLA_ROPE):
        lane[MLA_NOPE + i] = i if i < half else 64 + (i - half)
    return lane


def _dsw_lane_of(hh, dd):
    half = DSW_ROT // 2
    if dd < half:
        return hh * half + dd
    if dd < DSW_ROT:
        return 64 + hh * half + (dd - half)
    return (16 if hh == 0 else 80) + (dd - DSW_ROT)


def _rope_lane_tables(half, per_lane_index):
    inv_freq = jnp.power(jnp.float32(ROPE_THETA), -jnp.arange(half, dtype=F32) * (2.0 / (2 * half)))
    idx = np.zeros(LANES, np.int64)
    on = np.zeros(LANES, bool)
    sign = np.zeros(LANES, np.float32)
    for l in range(LANES):
        i = per_lane_index(l)
        if i is not None:
            idx[l], on[l] = i, True
            sign[l] = -1.0 if l < 64 else 1.0
    freq = jnp.where(jnp.asarray(on), jnp.take(inv_freq, jnp.asarray(idx)), 0.0)
    return freq.reshape(1, LANES), jnp.asarray(sign).reshape(1, LANES)


def kernel(x, positions, norm_attn, norm_ffn, mla_w_down, mla_q_norm, mla_w_uq, mla_kv_norm, mla_w_ukv,
           mla_w_o, dsw_kv_norm, dsw_w_kv, dsw_w_q, dsw_w_o, ffn_w_gate, ffn_w_up, ffn_w_down, moe_router,
           moe_w_gate, moe_w_up, moe_w_down, final_norm):
    B, S, D = x.shape
    T = B * S
    pos = positions.reshape(B, S, 1)

    lane_of = _mla_lane_of_dim()
    H = MLA_HEADS
    src_dn = np.full(MLA_Q_LORA + MLA_KV_LORA + LANES, -1, np.int64)
    src_dn[:MLA_Q_LORA + MLA_KV_LORA] = np.arange(MLA_Q_LORA + MLA_KV_LORA)
    src_uq = np.full(H * LANES, -1, np.int64)
    src_k = np.full(H * LANES, -1, np.int64)
    src_v = np.zeros(H * MLA_V, np.int64)
    for i in range(MLA_ROPE):
        src_dn[MLA_Q_LORA + MLA_KV_LORA + lane_of[MLA_NOPE + i]] = MLA_Q_LORA + MLA_KV_LORA + i
    for h in range(H):
        for dd in range(MLA_QK):
            src_uq[h * LANES + lane_of[dd]] = h * MLA_QK + dd
        for dd in range(MLA_NOPE):
            src_k[h * LANES + lane_of[dd]] = h * (MLA_NOPE + MLA_V) + dd
        for dd in range(MLA_V):
            src_v[h * MLA_V + dd] = h * (MLA_NOPE + MLA_V) + MLA_NOPE + dd
    wdn = _take_cols(mla_w_down[0], src_dn).astype(BF16)
    wuq = _take_cols(mla_w_uq[0], src_uq).astype(BF16)
    wk = _take_cols(mla_w_ukv[0], src_k).astype(BF16)
    wv = _take_cols(mla_w_ukv[0], src_v).astype(BF16)
    half = MLA_ROPE // 2
    mla_freq, mla_sign = _rope_lane_tables(
        half, lambda l: l if l < half else (l - 64 if 64 <= l < 64 + half else None))

    q, k, v = _mla_proj(x, pos, norm_attn[0:1], wdn, mla_q_norm[0:1], mla_kv_norm[0:1], wuq, wk, wv,
                        mla_freq, mla_sign, tm=MLA_TILE)
    o = _mla_attn(q, k, v)
    h = _attn_ffn(x.reshape(T, D), o.reshape(T, D), mla_w_o[0].astype(BF16), norm_ffn[0:1],
                  ffn_w_gate[0].astype(BF16), ffn_w_up[0].astype(BF16), ffn_w_down[0].astype(BF16))

    src_pair = np.zeros(DSW_WIDTH, np.int64)
    for j in range(DSW_HEADS // 2):
        for hh in range(2):
            for dd in range(DSW_HD):
                src_pair[j * LANES + _dsw_lane_of(hh, dd)] = (2 * j + hh) * DSW_HD + dd
    wq_b, wk_b, wv_b = [], [], []
    for g in range(N_BR):
        wq_b.append(_take_cols(dsw_w_q[0], g * DSW_WIDTH + src_pair))
        wk_b.append(_take_cols(dsw_w_kv, g * DSW_WIDTH + src_pair))
        wv_b.append(dsw_w_kv[:, (N_BR + g) * DSW_WIDTH:(N_BR + g + 1) * DSW_WIDTH])
    wq_b = jnp.stack(wq_b).astype(BF16)
    wk_b = jnp.stack(wk_b).astype(BF16)
    wv_b = jnp.stack(wv_b).astype(BF16)
    half = DSW_ROT // 2
    dsw_freq, dsw_sign = _rope_lane_tables(
        half, lambda l: l % half if l < 2 * half else ((l - 64) % half if 64 <= l < 64 + 2 * half else None))

    qkv = _dsw_proj(h.reshape(B, S, D), pos, norm_attn[1:2], dsw_kv_norm.reshape(1, D), wq_b, wk_b, wv_b,
                    dsw_freq, dsw_sign)
    o = _dsw_attn(qkv)
    wr = jnp.pad(moe_router[0], ((0, 0), (0, LANES - N_EXPERTS)))
    wr_hi = wr.astype(BF16)
    wr = jnp.concatenate([wr_hi, (wr - wr_hi.astype(F32)).astype(BF16)], axis=1)
    h3, hn, meta, counts = _dsw_router(h, o.reshape(T, D), dsw_w_o[0].astype(BF16), norm_ffn[1:2], wr)

    n_tiles = (2 * T) // MOE_ROWS + N_EXPERTS
    d1, d2, tile_expert, n_used = _moe_plan(meta, counts, n_tiles)
    xs = _moe_dispatch(d1, d2, hn, n_tiles * MOE_ROWS)
    ys = _moe_group(tile_expert, n_used, xs, moe_w_gate[0].astype(BF16), moe_w_up[0].astype(BF16),
                    moe_w_down[0].astype(BF16))
    out = _moe_combine(d1, d2, meta, h3, final_norm.reshape(1, D), ys)
    return out.reshape(B, S, D)
```

```python
import functools
import math

import numpy as np
import jax
import jax.numpy as jnp
from jax import lax
from jax.experimental import pallas as pl
from jax.experimental.pallas import tpu as pltpu

F32 = jnp.float32
BF16 = jnp.bfloat16

D_MODEL = 1024
RMS_EPS = 1e-6
ROPE_THETA = 500000.0
NEG_INF = -1e30
LANES = 128

MLA_HEADS = 16
MLA_NOPE = 64
MLA_ROPE = 32
MLA_V = 64
MLA_QK = MLA_NOPE + MLA_ROPE
MLA_Q_LORA = 384
MLA_KV_LORA = 256

DSW_BRANCHES = ((128, 1), (512, 4), (2048, 16))
N_BR = 3
DSW_HEADS = 16
DSW_HD = 64
DSW_ROT = 16
DSW_WIDTH = DSW_HEADS * DSW_HD
DSW_BLOCK = 128

FFN_DIM = 2816
N_EXPERTS = 8
EXPERT_DIM = 3584

VMEM_LIMIT = 56 * 1024 * 1024


def _cparams(sem):
    return pltpu.CompilerParams(dimension_semantics=sem, vmem_limit_bytes=VMEM_LIMIT)


def _resident(shape):
    nd = len(shape)
    return pl.BlockSpec(shape, lambda *_: (0,) * nd, pipeline_mode=pl.Buffered(1))


def _rms(x, g):
    return x * lax.rsqrt(jnp.mean(x * x, axis=-1, keepdims=True) + RMS_EPS) * g


def _dot(a, b):
    return jnp.dot(a, b, preferred_element_type=F32)


def _dot_nt(a, b):
    return lax.dot_general(a, b, (((1,), (1,)), ((), ())), preferred_element_type=F32)


def _rope(t, cos, sin_signed):
    return t * cos + pltpu.roll(t, 64, 1) * sin_signed


def _mla_proj_kernel(x_ref, pos_ref, g_ref, wdn_ref, qn_ref, kvn_ref, wuq_ref, wk_ref, wv_ref,
                     freq_ref, sign_ref, q_ref, k_ref, v_ref):
    xn = _rms(x_ref[...], g_ref[...]).astype(BF16)
    down = _dot(xn, wdn_ref[...])
    cq = _rms(down[:, :MLA_Q_LORA], qn_ref[...]).astype(BF16)
    ckv = _rms(down[:, MLA_Q_LORA:MLA_Q_LORA + MLA_KV_LORA], kvn_ref[...]).astype(BF16)
    kr = down[:, MLA_Q_LORA + MLA_KV_LORA:]
    ang = pos_ref[...].astype(F32) * freq_ref[...]
    cos = jnp.cos(ang)
    sin = jnp.sin(ang) * sign_ref[...]
    kr = _rope(kr, cos, sin)
    scale = LOG2E / math.sqrt(MLA_QK)
    cos_q = cos * scale
    sin_q = sin * scale
    q = _dot(cq, wuq_ref[...])
    for h in range(MLA_HEADS):
        q_ref[h] = _rope(q[:, h * LANES:(h + 1) * LANES], cos_q, sin_q).T.astype(BF16)
    kp = _dot(ckv, wk_ref[...])
    for h in range(MLA_HEADS):
        k_ref[h] = (kp[:, h * LANES:(h + 1) * LANES] + kr).astype(BF16)
    v = _dot(ckv, wv_ref[...])
    for j in range(MLA_HEADS // 2):
        v_ref[j] = v[:, j * LANES:(j + 1) * LANES].T.astype(BF16)


def _mla_proj(x, pos, g, wdn, qn, kvn, wuq, wk, wv, freq, sign, tm):
    B, S, _ = x.shape
    H = MLA_HEADS
    return pl.pallas_call(
        _mla_proj_kernel,
        grid=(B, S // tm),
        in_specs=[
            pl.BlockSpec((None, tm, D_MODEL), lambda b, i: (b, i, 0)),
            pl.BlockSpec((None, tm, 1), lambda b, i: (b, i, 0)),
            _resident(g.shape), _resident(wdn.shape), _resident(qn.shape), _resident(kvn.shape),
            _resident(wuq.shape), _resident(wk.shape), _resident(wv.shape),
            _resident(freq.shape), _resident(sign.shape),
        ],
        out_specs=[
            pl.BlockSpec((None, H, None, LANES, tm), lambda b, i: (b, 0, i, 0, 0)),
            pl.BlockSpec((None, H, tm, LANES), lambda b, i: (b, 0, i, 0)),
            pl.BlockSpec((None, H // 2, None, LANES, tm), lambda b, i: (b, 0, i, 0, 0)),
        ],
        out_shape=[
            jax.ShapeDtypeStruct((B, H, S // tm, LANES, tm), BF16),
            jax.ShapeDtypeStruct((B, H, S, LANES), BF16),
            jax.ShapeDtypeStruct((B, H // 2, S // tm, LANES, tm), BF16),
        ],
        compiler_params=_cparams(("parallel", "parallel")),
        name="mla_proj",
    )(x, pos, g, wdn, qn, kvn, wuq, wk, wv, freq, sign)


MLA_TILE = 512
MLA_ONES = 16
LOG2E = 1.4426950408889634

def _mla_attn_kernel(qT_ref, k_ref, vT_ref, o_ref, m_scr, acc_scr, s_scr, *, t):
    i = pl.program_id(2)
    m_scr[...] = jnp.full(m_scr.shape, NEG_INF, F32)
    acc_scr[...] = jnp.zeros(acc_scr.shape, F32)
    ones = jnp.ones((MLA_ONES, t), BF16)

    def scores(c, hh):
        return _dot(k_ref[hh, pl.ds(pl.multiple_of(c * t, t), t), :], qT_ref[hh])

    def update(s, c, hh):
        m_prev = m_scr[hh]
        m_new = jnp.maximum(m_prev, jnp.max(s, axis=0, keepdims=True))
        alpha = jnp.exp2(m_prev - m_new)
        p = jnp.exp2(s - m_new).astype(BF16)
        lhs = jnp.concatenate([vT_ref[c, hh * MLA_V:(hh + 1) * MLA_V, :], ones], axis=0)
        acc_scr[hh] = alpha * acc_scr[hh] + _dot(lhs, p)
        m_scr[hh] = m_new

    for hh in range(2):
        s_scr[hh] = scores(0, hh)

    def step(c):
        for hh in range(2):
            s = s_scr[hh]
            s_scr[hh] = scores(c + 1, hh)
            update(s, c, hh)

    def body(c2, carry):
        step(2 * c2)
        step(2 * c2 + 1)
        return carry

    lax.fori_loop(0, i // 2, body, 0)

    @pl.when(i % 2 == 1)
    def _():
        step(i - 1)

    key = lax.broadcasted_iota(jnp.int32, (t, t), 0)
    qry = lax.broadcasted_iota(jnp.int32, (t, t), 1)
    for hh in range(2):
        update(jnp.where(key <= qry, s_scr[hh], NEG_INF), i, hh)
    o = [acc_scr[hh, :MLA_V, :] * (1.0 / acc_scr[hh, MLA_V:MLA_V + 1, :]) for hh in range(2)]
    o_ref[...] = jnp.concatenate(o, axis=0).T.astype(BF16)


def _mla_attn(qT, k, vT):
    B, H, n, _, t = qT.shape
    S = n * t
    return pl.pallas_call(
        functools.partial(_mla_attn_kernel, t=t),
        grid=(B, H // 2, n),
        in_specs=[
            pl.BlockSpec((None, 2, None, LANES, t), lambda b, j, i: (b, j, i, 0, 0)),
            pl.BlockSpec((None, 2, S, LANES), lambda b, j, i: (b, j, 0, 0)),
            pl.BlockSpec((None, None, n, LANES, t), lambda b, j, i: (b, j, 0, 0, 0)),
        ],
        out_specs=pl.BlockSpec((None, t, LANES), lambda b, j, i: (b, i, j)),
        out_shape=jax.ShapeDtypeStruct((B, S, (H // 2) * LANES), BF16),
        scratch_shapes=[pltpu.VMEM((2, 1, t), F32), pltpu.VMEM((2, MLA_V + MLA_ONES, t), F32),
                        pltpu.VMEM((2, t, t), F32)],
        compiler_params=_cparams(("parallel", "parallel", "parallel")),
        name="mla_attn",
    )(qT, k, vT)


def _swiglu_acc(hn, wg_ref, wu_ref, wd_ref, chunk, scale=None):
    acc = None
    n = wg_ref.shape[-1]
    for c0 in range(0, n, chunk):
        g = _dot(hn, wg_ref[:, c0:c0 + chunk])
        u = _dot(hn, wu_ref[:, c0:c0 + chunk])
        a = g * jax.nn.sigmoid(g) * u
        if scale is not None:
            a = a * scale
        part = _dot(a.astype(BF16), wd_ref[c0:c0 + chunk, :])
        acc = part if acc is None else acc + part
    return acc


def _attn_ffn_kernel(x_ref, o_ref, wo_ref, g_ref, wg_ref, wu_ref, wd_ref, h_ref):
    h1 = x_ref[...] + _dot(o_ref[...], wo_ref[...])
    hn = _rms(h1, g_ref[...]).astype(BF16)
    h_ref[...] = h1 + _swiglu_acc(hn, wg_ref, wu_ref, wd_ref, FFN_DIM // 2)


def _attn_ffn(x, o, wo, g, wg, wu, wd, tm=512):
    T = x.shape[0]
    return pl.pallas_call(
        _attn_ffn_kernel,
        grid=(T // tm,),
        in_specs=[
            pl.BlockSpec((tm, D_MODEL), lambda i: (i, 0)),
            pl.BlockSpec((tm, D_MODEL), lambda i: (i, 0)),
            _resident(wo.shape), _resident(g.shape), _resident(wg.shape), _resident(wu.shape),
            _resident(wd.shape),
        ],
        out_specs=pl.BlockSpec((tm, D_MODEL), lambda i: (i, 0)),
        out_shape=jax.ShapeDtypeStruct((T, D_MODEL), F32),
        compiler_params=_cparams(("parallel",)),
        name="attn_ffn",
    )(x, o, wo, g, wg, wu, wd)


def _dsw_proj_kernel(h_ref, pos_ref, gq_ref, gkv_ref, wq_ref, wk_ref, wv_ref, freq_ref, sign_ref,
                     q1, k1, v1, q2, k2, v2, q3, k3, v3,
                     xq_scr, xkv_scr, cos_scr, sin_scr, pq_scr, pkv_scr, pcos_scr, psin_scr, *, tm):
    h = h_ref[...]
    r = lax.rsqrt(jnp.mean(h * h, axis=-1, keepdims=True) + RMS_EPS)
    hr = h * r
    nblk = D_MODEL // LANES
    xq_f = hr * gq_ref[...]
    xkv_f = hr * gkv_ref[...]
    for cb in range(nblk):
        xq_scr[cb] = xq_f[:, cb * LANES:(cb + 1) * LANES]
        xkv_scr[cb] = xkv_f[:, cb * LANES:(cb + 1) * LANES]
    ang = pos_ref[...].astype(F32) * freq_ref[...]
    cos_scr[...] = jnp.cos(ang)
    sin_scr[...] = jnp.sin(ang) * sign_ref[...]
    scale = LOG2E / math.sqrt(DSW_HD)
    outs = ((q1, k1, v1), (q2, k2, v2), (q3, k3, v3))
    for g, (_, d) in enumerate(DSW_BRANCHES):
        n = tm // d
        q_out, k_out, v_out = outs[g]
        if d == 1:
            xq = xq_f.astype(BF16)
            xkv = xkv_f.astype(BF16)
            cos = cos_scr[...]
            sin = sin_scr[...]
        else:
            for rr in range(d):
                rows = slice(rr * n, (rr + 1) * n)
                for cb in range(nblk):
                    cols = slice(cb * LANES, (cb + 1) * LANES)
                    pq_scr[rows, cols] = xq_scr[cb, pl.ds(rr, n, stride=d), :].astype(BF16)
                    pkv_scr[rows, cols] = xkv_scr[cb, pl.ds(rr, n, stride=d), :].astype(BF16)
                pcos_scr[rows, :] = cos_scr[pl.ds(rr, n, stride=d), :]
                psin_scr[rows, :] = sin_scr[pl.ds(rr, n, stride=d), :]
            xq = pq_scr[...]
            xkv = pkv_scr[...]
            cos = pcos_scr[...]
            sin = psin_scr[...]
        q = _dot(xq, wq_ref[g])
        k = _dot(xkv, wk_ref[g])
        v = _dot(xkv, wv_ref[g])
        cos_q = cos * scale
        sin_q = sin * scale
        for j in range(DSW_HEADS // 2):
            sl = slice(j * LANES, (j + 1) * LANES)
            qj = _rope(q[:, sl], cos_q, sin_q).astype(BF16)
            kj = _rope(k[:, sl], cos, sin).astype(BF16)
            vj = v[:, sl].astype(BF16)
            for rr in range(d):
                q_out[j, rr] = qj[rr * n:(rr + 1) * n]
                k_out[j, rr] = kj[rr * n:(rr + 1) * n]
                v_out[j, rr] = vj[rr * n:(rr + 1) * n]


def _dsw_proj(h, pos, gq, gkv, wq, wk, wv, freq, sign, tm=512):
    B, S, _ = h.shape
    P = DSW_HEADS // 2
    out_specs, out_shape = [], []
    for _, d in DSW_BRANCHES:
        for _ in range(3):
            out_specs.append(pl.BlockSpec((None, P, d, tm // d, LANES), lambda b, i: (b, 0, 0, i, 0)))
            out_shape.append(jax.ShapeDtypeStruct((B, P, d, S // d, LANES), BF16))
    return pl.pallas_call(
        functools.partial(_dsw_proj_kernel, tm=tm),
        grid=(B, S // tm),
        in_specs=[
            pl.BlockSpec((None, tm, D_MODEL), lambda b, i: (b, i, 0)),
            pl.BlockSpec((None, tm, 1), lambda b, i: (b, i, 0)),
            _resident(gq.shape), _resident(gkv.shape), _resident(wq.shape), _resident(wk.shape),
            _resident(wv.shape), _resident(freq.shape), _resident(sign.shape),
        ],
        out_specs=out_specs,
        out_shape=out_shape,
        scratch_shapes=[
            pltpu.VMEM((D_MODEL // LANES, tm, LANES), F32), pltpu.VMEM((D_MODEL // LANES, tm, LANES), F32),
            pltpu.VMEM((tm, LANES), F32), pltpu.VMEM((tm, LANES), F32),
            pltpu.VMEM((tm, D_MODEL), BF16), pltpu.VMEM((tm, D_MODEL), BF16),
            pltpu.VMEM((tm, LANES), F32), pltpu.VMEM((tm, LANES), F32),
        ],
        compiler_params=_cparams(("parallel", "parallel")),
        name="dsw_proj",
    )(h, pos, gq, gkv, wq, wk, wv, freq, sign)


def _dsw_attn_kernel(q1, k1, v1, q2, k2, v2, q3, k3, v3, o_ref,
                     o1_scr, l1_scr, o2_scr, l2_scr, o3_scr, l3_scr, s_scr, *, S):
    blk = DSW_BLOCK
    lane = lax.broadcasted_iota(jnp.int32, (blk, LANES), 1)
    mask_a = (lane < 8) | ((lane >= 16) & (lane < 72))
    first_half = lane < DSW_HD
    row1 = lax.broadcasted_iota(jnp.int32, (2 * blk, blk), 0)
    col1 = lax.broadcasted_iota(jnp.int32, (2 * blk, blk), 1)
    row1 = jnp.where(row1 >= blk, row1 - blk, row1)
    valid_first = col1 <= row1
    row2 = lax.broadcasted_iota(jnp.int32, (2 * blk, 2 * blk), 0)
    col2 = lax.broadcasted_iota(jnp.int32, (2 * blk, 2 * blk), 1)
    row2 = jnp.where(row2 >= blk, row2 - blk, row2)
    valid_next = ((col2 < blk) & (col2 >= row2)) | ((col2 >= blk) & (col2 - blk <= row2))

    def scores(q_ref, k_ref, off, nkeys):
        qf = q_ref[off:off + blk, :].astype(F32)
        q2 = jnp.concatenate([jnp.where(mask_a, qf, 0.0), jnp.where(mask_a, 0.0, qf)], axis=0).astype(BF16)
        return _dot_nt(q2, k_ref[off + blk - nkeys:off + blk, :])

    def finish(s, v_ref, off, valid, o_tok, l_tok, tok0, d):
        nkeys = valid.shape[1]
        s = jnp.where(valid, s, NEG_INF)
        m = jnp.max(s, axis=-1, keepdims=True)
        p = jnp.exp2(s - m)
        l = jnp.sum(p, axis=-1, keepdims=True)
        o2 = _dot(p.astype(BF16), v_ref[off + blk - nkeys:off + blk, :]) / l
        lse = m + jnp.log2(l)
        rows = pl.ds(tok0, blk, stride=d) if d > 1 else pl.ds(tok0, blk)
        o_tok[rows, :] = jnp.where(first_half, o2[:blk], o2[blk:])
        l_tok[rows, :] = jnp.where(first_half, lse[:blk], lse[blk:])

    for (_, d), (q_ref, k_ref, v_ref, o_tok, l_tok) in zip(
            DSW_BRANCHES, ((q1, k1, v1, o1_scr, l1_scr), (q2, k2, v2, o2_scr, l2_scr), (q3, k3, v3, o3_scr, l3_scr))):
        L = S // d
        blocks = [(rr * L + ii * blk, ii * blk * d + rr, valid_first if ii == 0 else valid_next)
                  for rr in range(d) for ii in range(L // blk)]
        off, _, valid = blocks[0]
        s_scr[:, :valid.shape[1]] = scores(q_ref, k_ref, off, valid.shape[1])
        for n, (off, tok0, valid) in enumerate(blocks):
            s = s_scr[:, :valid.shape[1]]
            if n + 1 < len(blocks):
                noff, _, nvalid = blocks[n + 1]
                s_scr[:, :nvalid.shape[1]] = scores(q_ref, k_ref, noff, nvalid.shape[1])
            finish(s, v_ref, off, valid, o_tok, l_tok, tok0, d)

    ch = 256
    for c0 in range(0, S, ch):
        sl = slice(c0, c0 + ch)
        la, lb, lc = l1_scr[sl, :], l2_scr[sl, :], l3_scr[sl, :]
        m = jnp.maximum(jnp.maximum(la, lb), lc)
        wa, wb, wc = jnp.exp2(la - m), jnp.exp2(lb - m), jnp.exp2(lc - m)
        num = wa * o1_scr[sl, :] + wb * o2_scr[sl, :] + wc * o3_scr[sl, :]
        o_ref[sl, :] = (num / (wa + wb + wc)).astype(BF16)


def _dsw_attn(qkv):
    B, P = qkv[0].shape[0], qkv[0].shape[1]
    S = qkv[0].shape[2] * qkv[0].shape[3]
    flat = [a.reshape(B, P, S, LANES) for a in qkv]
    spec = pl.BlockSpec((None, None, S, LANES), lambda b, j: (b, j, 0, 0))
    return pl.pallas_call(
        functools.partial(_dsw_attn_kernel, S=S),
        grid=(B, P),
        in_specs=[spec] * 9,
        out_specs=pl.BlockSpec((None, S, LANES), lambda b, j: (b, 0, j)),
        out_shape=jax.ShapeDtypeStruct((B, S, P * LANES), BF16),
        scratch_shapes=[pltpu.VMEM((S, LANES), F32)] * 6 + [pltpu.VMEM((2 * DSW_BLOCK, 2 * DSW_BLOCK), F32)],
        compiler_params=_cparams(("parallel", "parallel")),
        name="dsw_attn",
    )(*flat)


META_E1, META_E2, META_R1, META_R2, META_G1, META_G2 = range(6)


def _dsw_router_kernel(h_ref, o_ref, wo_ref, g_ref, wr_ref, h3_ref, hn_ref, meta_ref, cnt_ref, base_scr):
    @pl.when(pl.program_id(0) == 0)
    def _():
        base_scr[...] = jnp.zeros_like(base_scr)

    h3 = h_ref[...] + _dot(o_ref[...], wo_ref[...])
    h3_ref[...] = h3
    hn = _rms(h3, g_ref[...])
    hn_ref[...] = hn
    hi = hn.astype(BF16)
    lo = (hn - hi.astype(F32)).astype(BF16)
    part = _dot(hi, wr_ref[...])
    logits = part[:, :LANES] + part[:, LANES:] + _dot(lo, wr_ref[:, :LANES])
    tm = logits.shape[0]
    lane = lax.broadcasted_iota(jnp.int32, logits.shape, 1).astype(F32)
    logits = jnp.where(lane < N_EXPERTS, logits, -jnp.inf)
    v1 = jnp.max(logits, axis=-1, keepdims=True)
    i1 = jnp.min(jnp.where(logits == v1, lane, float(LANES)), axis=-1, keepdims=True)
    rest = jnp.where(lane == i1, -jnp.inf, logits)
    v2 = jnp.max(rest, axis=-1, keepdims=True)
    i2 = jnp.min(jnp.where(rest == v2, lane, float(LANES)), axis=-1, keepdims=True)
    e2 = jnp.exp(v2 - v1)
    den = 1.0 + e2
    onehot = jnp.where((lane == i1) | (lane == i2), 1.0, 0.0)
    row = lax.broadcasted_iota(jnp.int32, (tm, tm), 0)
    col = lax.broadcasted_iota(jnp.int32, (tm, tm), 1)
    earlier = jnp.where(col < row, 1.0, 0.0).astype(BF16)
    before = _dot(earlier, onehot.astype(BF16)) + base_scr[0:1, :]
    r1 = jnp.sum(jnp.where(lane == i1, before, 0.0), axis=-1, keepdims=True)
    r2 = jnp.sum(jnp.where(lane == i2, before, 0.0), axis=-1, keepdims=True)
    total = base_scr[0:1, :] + jnp.sum(onehot, axis=0, keepdims=True)
    base_scr[...] = jnp.broadcast_to(total, base_scr.shape)
    cnt_ref[...] = jnp.broadcast_to(total, cnt_ref.shape)
    meta = jnp.zeros_like(logits)
    for k, val in ((META_E1, i1), (META_E2, i2), (META_R1, r1), (META_R2, r2),
                   (META_G1, 1.0 / den), (META_G2, e2 / den)):
        meta = jnp.where(lane == float(k), val, meta)
    meta_ref[...] = meta


def _dsw_router(h, o, wo, g, wr, tm=512):
    T = h.shape[0]
    return pl.pallas_call(
        _dsw_router_kernel,
        grid=(T // tm,),
        in_specs=[
            pl.BlockSpec((tm, D_MODEL), lambda i: (i, 0)),
            pl.BlockSpec((tm, D_MODEL), lambda i: (i, 0)),
            _resident(wo.shape), _resident(g.shape), _resident(wr.shape),
        ],
        out_specs=[
            pl.BlockSpec((tm, D_MODEL), lambda i: (i, 0)),
            pl.BlockSpec((tm, D_MODEL), lambda i: (i, 0)),
            pl.BlockSpec((tm, LANES), lambda i: (i, 0)),
            pl.BlockSpec((8, LANES), lambda i: (0, 0)),
        ],
        out_shape=[
            jax.ShapeDtypeStruct((T, D_MODEL), F32),
            jax.ShapeDtypeStruct((T, D_MODEL), F32),
            jax.ShapeDtypeStruct((T, LANES), F32),
            jax.ShapeDtypeStruct((8, LANES), F32),
        ],
        scratch_shapes=[pltpu.VMEM((8, LANES), F32)],
        compiler_params=_cparams(("arbitrary",)),
        name="dsw_router",
    )(h, o, wo, g, wr)


MOE_ROWS = 512


def _row_copy(src, src_row, dst, dst_row, sem):
    return pltpu.make_async_copy(src.at[pl.ds(src_row, 1)], dst.at[pl.ds(dst_row, 1)], sem)


def _moe_dispatch_kernel(pad_start_ref, pad_len_ref, nu_ref, d1_ref, d2_ref, hn_ref, xs_hbm, zero_scr, sem, zsem,
                         *, tm):
    @pl.when(pl.program_id(0) == 0)
    def _():
        zero_scr[...] = jnp.zeros_like(zero_scr)
        R = zero_scr.shape[0]

        def zero_tile(j, c):
            cp = pltpu.make_async_copy(zero_scr, xs_hbm.at[pl.ds(pl.multiple_of(j * R, R), R)], zsem)
            cp.start()
            cp.wait()
            return c

        lax.fori_loop(nu_ref[0], xs_hbm.shape[0] // R, zero_tile, 0)
        for e in range(N_EXPERTS):
            def zero_row(r, c, e=e):
                _row_copy(zero_scr, 0, xs_hbm, pad_start_ref[e] + r, zsem).start()
                return c

            def wait_row(r, c, e=e):
                _row_copy(zero_scr, 0, xs_hbm, pad_start_ref[e] + r, zsem).wait()
                return c

            lax.fori_loop(0, pad_len_ref[e], zero_row, 0)
            lax.fori_loop(0, pad_len_ref[e], wait_row, 0)

    def body(r, c):
        _row_copy(hn_ref, r, xs_hbm, d1_ref[0, r], sem).start(priority=0)
        _row_copy(hn_ref, r, xs_hbm, d2_ref[0, r], sem).start(priority=1)
        return c

    lax.fori_loop(0, tm, body, 0, unroll=8)
    for _ in range(2):
        pltpu.make_async_copy(hn_ref, xs_hbm.at[pl.ds(0, tm)], sem).wait()


def _moe_dispatch(pad_start, pad_len, n_used, d1, d2, hn, n_rows, tm=512):
    T = hn.shape[0]
    smem = lambda: pl.BlockSpec((None, 1, tm), lambda i, ps, pn, nu: (i, 0, 0), memory_space=pltpu.SMEM)
    return pl.pallas_call(
        functools.partial(_moe_dispatch_kernel, tm=tm),
        grid_spec=pltpu.PrefetchScalarGridSpec(
            num_scalar_prefetch=3,
            grid=(T // tm,),
            in_specs=[smem(), smem(), pl.BlockSpec((tm, D_MODEL), lambda i, ps, pn, nu: (i, 0))],
            out_specs=pl.BlockSpec(memory_space=pl.ANY),
            scratch_shapes=[pltpu.VMEM((MOE_ROWS, D_MODEL), F32), pltpu.SemaphoreType.DMA,
                            pltpu.SemaphoreType.DMA],
        ),
        out_shape=jax.ShapeDtypeStruct((n_rows, D_MODEL), F32),
        compiler_params=_cparams(("arbitrary",)),
        name="moe_dispatch",
    )(pad_start, pad_len, n_used, d1.reshape(T // tm, 1, tm), d2.reshape(T // tm, 1, tm), hn)


def _moe_group_kernel(te_ref, nu_ref, xs_ref, wg_ref, wu_ref, wd_ref, ys_ref, xb_scr, acc_scr):
    j = pl.program_id(0)
    f = pl.program_id(1)
    used = j < nu_ref[0]

    @pl.when(used & (f == 0))
    def _():
        xb_scr[...] = xs_ref[...].astype(BF16)
        acc_scr[...] = jnp.zeros_like(acc_scr)

    @pl.when(used)
    def _():
        acc_scr[...] += _swiglu_acc(xb_scr[...], wg_ref, wu_ref, wd_ref, wg_ref.shape[-1] // 2)

    @pl.when(f == pl.num_programs(1) - 1)
    def _():
        ys_ref[...] = jnp.where(used, acc_scr[...], 0.0)


def _moe_group(tile_expert, n_used, xs, wg, wu, wd, tf=1792):
    n_rows = xs.shape[0]
    R = MOE_ROWS
    E, _, Fe = wg.shape
    nf = Fe // tf

    def w_col(j, f, te, nu):
        return jnp.where(j < nu[0], f, nf - 1)

    return pl.pallas_call(
        _moe_group_kernel,
        grid_spec=pltpu.PrefetchScalarGridSpec(
            num_scalar_prefetch=2,
            grid=(n_rows // R, nf),
            in_specs=[
                pl.BlockSpec((R, D_MODEL), lambda j, f, te, nu: (jnp.maximum(jnp.minimum(j, nu[0] - 1), 0), 0)),
                pl.BlockSpec((None, D_MODEL, tf), lambda j, f, te, nu: (te[j], 0, w_col(j, f, te, nu))),
                pl.BlockSpec((None, D_MODEL, tf), lambda j, f, te, nu: (te[j], 0, w_col(j, f, te, nu))),
                pl.BlockSpec((None, tf, D_MODEL), lambda j, f, te, nu: (te[j], w_col(j, f, te, nu), 0)),
            ],
            out_specs=pl.BlockSpec((R, D_MODEL), lambda j, f, te, nu: (j, 0)),
            scratch_shapes=[pltpu.VMEM((R, D_MODEL), BF16), pltpu.VMEM((R, D_MODEL), F32)],
        ),
        out_shape=jax.ShapeDtypeStruct((n_rows, D_MODEL), F32),
        compiler_params=_cparams(("arbitrary", "arbitrary")),
        name="moe_group",
    )(tile_expert, n_used, xs, wg, wu, wd)


def _moe_combine_kernel(d1_ref, d2_ref, n1_ref, n2_ref, meta_ref, h3_ref, gf_ref, ys_hbm, out_ref,
                        y_scr, sems, *, tm):
    i = pl.program_id(0)
    slot = i % 2

    def gather(da_ref, db_ref, s):
        def body(r, c):
            _row_copy(ys_hbm, da_ref[0, r], y_scr.at[s, 0], r, sems.at[s]).start(priority=0)
            _row_copy(ys_hbm, db_ref[0, r], y_scr.at[s, 1], r, sems.at[s]).start(priority=1)
            return c

        lax.fori_loop(0, tm, body, 0, unroll=8)

    @pl.when(i == 0)
    def _():
        gather(d1_ref, d2_ref, 0)

    @pl.when(i + 1 < pl.num_programs(0))
    def _():
        gather(n1_ref, n2_ref, 1 - slot)

    for k in range(2):
        pltpu.make_async_copy(ys_hbm.at[pl.ds(0, tm)], y_scr.at[slot, k], sems.at[slot]).wait()
    meta = meta_ref[...]
    lane = lax.broadcasted_iota(jnp.int32, meta.shape, 1)
    g1 = jnp.sum(jnp.where(lane == META_G1, meta, 0.0), axis=-1, keepdims=True)
    g2 = jnp.sum(jnp.where(lane == META_G2, meta, 0.0), axis=-1, keepdims=True)
    out_ref[...] = _rms(h3_ref[...] + g1 * y_scr[slot, 0] + g2 * y_scr[slot, 1], gf_ref[...])


def _moe_combine(d1, d2, meta, h3, gf, ys, tm=512):
    T = h3.shape[0]
    n = T // tm
    smem = lambda nxt: pl.BlockSpec((None, 1, tm), lambda i: (jnp.minimum(i + nxt, n - 1), 0, 0),
                                    memory_space=pltpu.SMEM)
    d1 = d1.reshape(n, 1, tm)
    d2 = d2.reshape(n, 1, tm)
    return pl.pallas_call(
        functools.partial(_moe_combine_kernel, tm=tm),
        grid=(n,),
        in_specs=[
            smem(0), smem(0), smem(1), smem(1),
            pl.BlockSpec((tm, LANES), lambda i: (i, 0)),
            pl.BlockSpec((tm, D_MODEL), lambda i: (i, 0)),
            pl.BlockSpec((1, D_MODEL), lambda i: (0, 0)),
            pl.BlockSpec(memory_space=pl.ANY),
        ],
        out_specs=pl.BlockSpec((tm, D_MODEL), lambda i: (i, 0)),
        out_shape=jax.ShapeDtypeStruct((T, D_MODEL), F32),
        scratch_shapes=[pltpu.VMEM((2, 2, tm, D_MODEL), F32), pltpu.SemaphoreType.DMA((2,))],
        compiler_params=_cparams(("arbitrary",)),
        name="moe_combine",
    )(d1, d2, d1, d2, meta, h3, gf, ys)


def _moe_plan(meta, counts, n_tiles):
    R = MOE_ROWS
    cnt = counts[0, :N_EXPERTS].astype(jnp.int32)
    tiles = (cnt + R - 1) // R
    tile_end = jnp.cumsum(tiles)
    row_start = (tile_end - tiles) * R
    e1 = meta[:, META_E1].astype(jnp.int32)
    e2 = meta[:, META_E2].astype(jnp.int32)
    d1 = row_start[e1] + meta[:, META_R1].astype(jnp.int32)
    d2 = row_start[e2] + meta[:, META_R2].astype(jnp.int32)
    n_used = tile_end[-1:]
    j = jnp.minimum(jnp.arange(n_tiles, dtype=jnp.int32), n_used - 1)
    tile_expert = jnp.sum(j[:, None] >= tile_end[None, :], axis=1).astype(jnp.int32)
    pad_start = (row_start + cnt).astype(jnp.int32)
    pad_len = (tiles * R - cnt).astype(jnp.int32)
    return d1, d2, tile_expert, n_used.astype(jnp.int32), pad_start, pad_len


def _take_cols(w, src):
    src = np.asarray(src)
    cols = jnp.take(w, jnp.asarray(np.maximum(src, 0)), axis=1)
    return jnp.where(jnp.asarray(src >= 0)[None, :], cols, 0.0)


def _mla_lane_of_dim():
    lane = np.zeros(MLA_QK, np.int64)
    for dd in range(MLA_NOPE):
        lane[dd] = 16 + dd if dd < 48 else 80 + (dd - 48)
    half = MLA_ROPE // 2
    for i in range(MLA_ROPE):
        lane[MLA_NOPE + i] = i if i < half else 64 + (i - half)
    return lane


def _dsw_lane_of(hh, dd):
    half = DSW_ROT // 2
    if dd < half:
        return hh * half + dd
    if dd < DSW_ROT:
        return 64 + hh * half + (dd - half)
    return (16 if hh == 0 else 80) + (dd - DSW_ROT)


def _rope_lane_tables(half, per_lane_index):
    inv_freq = jnp.power(jnp.float32(ROPE_THETA), -jnp.arange(half, dtype=F32) * (2.0 / (2 * half)))
    idx = np.zeros(LANES, np.int64)
    on = np.zeros(LANES, bool)
    sign = np.zeros(LANES, np.float32)
    for l in range(LANES):
        i = per_lane_index(l)
        if i is not None:
            idx[l], on[l] = i, True
            sign[l] = -1.0 if l < 64 else 1.0
    freq = jnp.where(jnp.asarray(on), jnp.take(inv_freq, jnp.asarray(idx)), 0.0)
    return freq.reshape(1, LANES), jnp.asarray(sign).reshape(1, LANES)


def kernel(x, positions, norm_attn, norm_ffn, mla_w_down, mla_q_norm, mla_w_uq, mla_kv_norm, mla_w_ukv,
           mla_w_o, dsw_kv_norm, dsw_w_kv, dsw_w_q, dsw_w_o, ffn_w_gate, ffn_w_up, ffn_w_down, moe_router,
           moe_w_gate, moe_w_up, moe_w_down, final_norm):
    B, S, D = x.shape
    T = B * S
    pos = positions.reshape(B, S, 1)

    lane_of = _mla_lane_of_dim()
    H = MLA_HEADS
    src_dn = np.full(MLA_Q_LORA + MLA_KV_LORA + LANES, -1, np.int64)
    src_dn[:MLA_Q_LORA + MLA_KV_LORA] = np.arange(MLA_Q_LORA + MLA_KV_LORA)
    src_uq = np.full(H * LANES, -1, np.int64)
    src_k = np.full(H * LANES, -1, np.int64)
    src_v = np.zeros(H * MLA_V, np.int64)
    for i in range(MLA_ROPE):
        src_dn[MLA_Q_LORA + MLA_KV_LORA + lane_of[MLA_NOPE + i]] = MLA_Q_LORA + MLA_KV_LORA + i
    for h in range(H):
        for dd in range(MLA_QK):
            src_uq[h * LANES + lane_of[dd]] = h * MLA_QK + dd
        for dd in range(MLA_NOPE):
            src_k[h * LANES + lane_of[dd]] = h * (MLA_NOPE + MLA_V) + dd
        for dd in range(MLA_V):
            src_v[h * MLA_V + dd] = h * (MLA_NOPE + MLA_V) + MLA_NOPE + dd
    wdn = _take_cols(mla_w_down[0], src_dn).astype(BF16)
    wuq = _take_cols(mla_w_uq[0], src_uq).astype(BF16)
    wk = _take_cols(mla_w_ukv[0], src_k).astype(BF16)
    wv = _take_cols(mla_w_ukv[0], src_v).astype(BF16)
    half = MLA_ROPE // 2
    mla_freq, mla_sign = _rope_lane_tables(
        half, lambda l: l if l < half else (l - 64 if 64 <= l < 64 + half else None))

    q, k, v = _mla_proj(x, pos, norm_attn[0:1], wdn, mla_q_norm[0:1], mla_kv_norm[0:1], wuq, wk, wv,
                        mla_freq, mla_sign, tm=MLA_TILE)
    o = _mla_attn(q, k, v)
    h = _attn_ffn(x.reshape(T, D), o.reshape(T, D), mla_w_o[0].astype(BF16), norm_ffn[0:1],
                  ffn_w_gate[0].astype(BF16), ffn_w_up[0].astype(BF16), ffn_w_down[0].astype(BF16))

    src_pair = np.zeros(DSW_WIDTH, np.int64)
    for j in range(DSW_HEADS // 2):
        for hh in range(2):
            for dd in range(DSW_HD):
                src_pair[j * LANES + _dsw_lane_of(hh, dd)] = (2 * j + hh) * DSW_HD + dd
    wq_b, wk_b, wv_b = [], [], []
    for g in range(N_BR):
        wq_b.append(_take_cols(dsw_w_q[0], g * DSW_WIDTH + src_pair))
        wk_b.append(_take_cols(dsw_w_kv, g * DSW_WIDTH + src_pair))
        wv_b.append(dsw_w_kv[:, (N_BR + g) * DSW_WIDTH:(N_BR + g + 1) * DSW_WIDTH])
    wq_b = jnp.stack(wq_b).astype(BF16)
    wk_b = jnp.stack(wk_b).astype(BF16)
    wv_b = jnp.stack(wv_b).astype(BF16)
    half = DSW_ROT // 2
    dsw_freq, dsw_sign = _rope_lane_tables(
        half, lambda l: l % half if l < 2 * half else ((l - 64) % half if 64 <= l < 64 + 2 * half else None))

    qkv = _dsw_proj(h.reshape(B, S, D), pos, norm_attn[1:2], dsw_kv_norm.reshape(1, D), wq_b, wk_b, wv_b,
                    dsw_freq, dsw_sign)
    o = _dsw_attn(qkv)
    wr = jnp.pad(moe_router[0], ((0, 0), (0, LANES - N_EXPERTS)))
    wr_hi = wr.astype(BF16)
    wr = jnp.concatenate([wr_hi, (wr - wr_hi.astype(F32)).astype(BF16)], axis=1)
    h3, hn, meta, counts = _dsw_router(h, o.reshape(T, D), dsw_w_o[0].astype(BF16), norm_ffn[1:2], wr)

    n_tiles = (2 * T) // MOE_ROWS + N_EXPERTS
    d1, d2, tile_expert, n_used, pad_start, pad_len = _moe_plan(meta, counts, n_tiles)
    xs = _moe_dispatch(pad_start, pad_len, n_used, d1, d2, hn, n_tiles * MOE_ROWS)
    ys = _moe_group(tile_expert, n_used, xs, moe_w_gate[0].astype(BF16), moe_w_up[0].astype(BF16),
                    moe_w_down[0].astype(BF16))
    out = _moe_combine(d1, d2, meta, h3, final_norm.reshape(1, D), ys)
    return out.reshape(B, S, D)
```

```python
import functools
import math

import numpy as np
import jax
import jax.numpy as jnp
from jax import lax
from jax.experimental import pallas as pl
from jax.experimental.pallas import tpu as pltpu

F32 = jnp.float32
BF16 = jnp.bfloat16

D_MODEL = 1024
RMS_EPS = 1e-6
ROPE_THETA = 500000.0
NEG_INF = -1e30
LANES = 128

MLA_HEADS = 16
MLA_NOPE = 64
MLA_ROPE = 32
MLA_V = 64
MLA_QK = MLA_NOPE + MLA_ROPE
MLA_Q_LORA = 384
MLA_KV_LORA = 256

DSW_BRANCHES = ((128, 1), (512, 4), (2048, 16))
N_BR = 3
DSW_HEADS = 16
DSW_HD = 64
DSW_ROT = 16
DSW_WIDTH = DSW_HEADS * DSW_HD
DSW_BLOCK = 128

FFN_DIM = 2816
N_EXPERTS = 8
EXPERT_DIM = 3584

VMEM_LIMIT = 56 * 1024 * 1024


def _cparams(sem):
    return pltpu.CompilerParams(dimension_semantics=sem, vmem_limit_bytes=VMEM_LIMIT)


def _resident(shape):
    nd = len(shape)
    return pl.BlockSpec(shape, lambda *_: (0,) * nd, pipeline_mode=pl.Buffered(1))


def _rms(x, g):
    return x * lax.rsqrt(jnp.mean(x * x, axis=-1, keepdims=True) + RMS_EPS) * g


def _dot(a, b):
    return jnp.dot(a, b, preferred_element_type=F32)


def _dot_nt(a, b):
    return lax.dot_general(a, b, (((1,), (1,)), ((), ())), preferred_element_type=F32)


def _rope(t, cos, sin_signed):
    return t * cos + pltpu.roll(t, 64, 1) * sin_signed


def _mla_proj_kernel(x_ref, pos_ref, g_ref, wdn_ref, qn_ref, kvn_ref, wuq_ref, wk_ref, wv_ref,
                     freq_ref, sign_ref, q_ref, k_ref, v_ref):
    xn = _rms(x_ref[...], g_ref[...]).astype(BF16)
    down = _dot(xn, wdn_ref[...])
    cq = _rms(down[:, :MLA_Q_LORA], qn_ref[...]).astype(BF16)
    ckv = _rms(down[:, MLA_Q_LORA:MLA_Q_LORA + MLA_KV_LORA], kvn_ref[...]).astype(BF16)
    kr = down[:, MLA_Q_LORA + MLA_KV_LORA:]
    ang = pos_ref[...].astype(F32) * freq_ref[...]
    cos = jnp.cos(ang)
    sin = jnp.sin(ang) * sign_ref[...]
    kr = _rope(kr, cos, sin)
    scale = LOG2E / math.sqrt(MLA_QK)
    cos_q = cos * scale
    sin_q = sin * scale
    q = _dot(cq, wuq_ref[...])
    for h in range(MLA_HEADS):
        q_ref[h] = _rope(q[:, h * LANES:(h + 1) * LANES], cos_q, sin_q).T.astype(BF16)
    kp = _dot(ckv, wk_ref[...])
    for h in range(MLA_HEADS):
        k_ref[h] = (kp[:, h * LANES:(h + 1) * LANES] + kr).astype(BF16)
    v = _dot(ckv, wv_ref[...])
    for j in range(MLA_HEADS // 2):
        v_ref[j] = v[:, j * LANES:(j + 1) * LANES].T.astype(BF16)


def _mla_proj(x, pos, g, wdn, qn, kvn, wuq, wk, wv, freq, sign, tm):
    B, S, _ = x.shape
    H = MLA_HEADS
    return pl.pallas_call(
        _mla_proj_kernel,
        grid=(B, S // tm),
        in_specs=[
            pl.BlockSpec((None, tm, D_MODEL), lambda b, i: (b, i, 0)),
            pl.BlockSpec((None, tm, 1), lambda b, i: (b, i, 0)),
            _resident(g.shape), _resident(wdn.shape), _resident(qn.shape), _resident(kvn.shape),
            _resident(wuq.shape), _resident(wk.shape), _resident(wv.shape),
            _resident(freq.shape), _resident(sign.shape),
        ],
        out_specs=[
            pl.BlockSpec((None, H, None, LANES, tm), lambda b, i: (b, 0, i, 0, 0)),
            pl.BlockSpec((None, H, tm, LANES), lambda b, i: (b, 0, i, 0)),
            pl.BlockSpec((None, H // 2, None, LANES, tm), lambda b, i: (b, 0, i, 0, 0)),
        ],
        out_shape=[
            jax.ShapeDtypeStruct((B, H, S // tm, LANES, tm), BF16),
            jax.ShapeDtypeStruct((B, H, S, LANES), BF16),
            jax.ShapeDtypeStruct((B, H // 2, S // tm, LANES, tm), BF16),
        ],
        compiler_params=_cparams(("parallel", "parallel")),
        name="mla_proj",
    )(x, pos, g, wdn, qn, kvn, wuq, wk, wv, freq, sign)


MLA_TILE = 512
MLA_ONES = 16
LOG2E = 1.4426950408889634

def _mla_attn_kernel(qT_ref, k_ref, vT_ref, *rest, t, n_cast):
    cast_in, o_ref, cast_out = rest[:n_cast], rest[n_cast], rest[n_cast + 1:2 * n_cast + 1]
    m_scr, acc_scr, s_scr = rest[2 * n_cast + 1:]
    for src, dst in zip(cast_in, cast_out):
        dst[...] = src[...].astype(BF16)
    i = pl.program_id(2)
    m_scr[...] = jnp.full(m_scr.shape, NEG_INF, F32)
    acc_scr[...] = jnp.zeros(acc_scr.shape, F32)
    ones = jnp.ones((MLA_ONES, t), BF16)

    def scores(c, hh):
        return _dot(k_ref[hh, pl.ds(pl.multiple_of(c * t, t), t), :], qT_ref[hh])

    def update(s, c, hh):
        m_prev = m_scr[hh]
        m_new = jnp.maximum(m_prev, jnp.max(s, axis=0, keepdims=True))
        alpha = jnp.exp2(m_prev - m_new)
        p = jnp.exp2(s - m_new).astype(BF16)
        lhs = jnp.concatenate([vT_ref[c, hh * MLA_V:(hh + 1) * MLA_V, :], ones], axis=0)
        acc_scr[hh] = alpha * acc_scr[hh] + _dot(lhs, p)
        m_scr[hh] = m_new

    for hh in range(2):
        s_scr[hh] = scores(0, hh)

    def step(c):
        for hh in range(2):
            s = s_scr[hh]
            s_scr[hh] = scores(c + 1, hh)
            update(s, c, hh)

    def body(c2, carry):
        step(2 * c2)
        step(2 * c2 + 1)
        return carry

    lax.fori_loop(0, i // 2, body, 0)

    @pl.when(i % 2 == 1)
    def _():
        step(i - 1)

    key = lax.broadcasted_iota(jnp.int32, (t, t), 0)
    qry = lax.broadcasted_iota(jnp.int32, (t, t), 1)
    for hh in range(2):
        update(jnp.where(key <= qry, s_scr[hh], NEG_INF), i, hh)
    o = [acc_scr[hh, :MLA_V, :] * (1.0 / acc_scr[hh, MLA_V:MLA_V + 1, :]) for hh in range(2)]
    o_ref[...] = jnp.concatenate(o, axis=0).T.astype(BF16)


def _mla_attn(qT, k, vT, cast=()):
    B, H, n, _, t = qT.shape
    S = n * t
    P = H // 2
    steps = B * P * n
    cast_specs, cast_shapes = [], []
    for w in cast:
        E, rows, cols = w.shape
        cr = E * rows // steps
        assert E * rows == cr * steps and rows % cr == 0 and cr % 16 == 0, (w.shape, steps)
        per = rows // cr
        cast_specs.append(pl.BlockSpec(
            (None, cr, cols), lambda b, j, i, per=per: (((b * P + j) * n + i) // per, ((b * P + j) * n + i) % per, 0)))
        cast_shapes.append(jax.ShapeDtypeStruct(w.shape, BF16))
    outs = pl.pallas_call(
        functools.partial(_mla_attn_kernel, t=t, n_cast=len(cast)),
        grid=(B, P, n),
        in_specs=[
            pl.BlockSpec((None, 2, None, LANES, t), lambda b, j, i: (b, j, i, 0, 0)),
            pl.BlockSpec((None, 2, S, LANES), lambda b, j, i: (b, j, 0, 0)),
            pl.BlockSpec((None, None, n, LANES, t), lambda b, j, i: (b, j, 0, 0, 0)),
        ] + cast_specs,
        out_specs=[pl.BlockSpec((None, t, LANES), lambda b, j, i: (b, i, j))] + cast_specs,
        out_shape=[jax.ShapeDtypeStruct((B, S, P * LANES), BF16)] + cast_shapes,
        scratch_shapes=[pltpu.VMEM((2, 1, t), F32), pltpu.VMEM((2, MLA_V + MLA_ONES, t), F32),
                        pltpu.VMEM((2, t, t), F32)],
        compiler_params=_cparams(("parallel", "parallel", "parallel")),
        name="mla_attn",
    )(qT, k, vT, *cast)
    return outs[0], outs[1:]


def _swiglu_acc(hn, wg_ref, wu_ref, wd_ref, chunk, scale=None):
    acc = None
    n = wg_ref.shape[-1]
    for c0 in range(0, n, chunk):
        g = _dot(hn, wg_ref[:, c0:c0 + chunk])
        u = _dot(hn, wu_ref[:, c0:c0 + chunk])
        a = g * jax.nn.sigmoid(g) * u
        if scale is not None:
            a = a * scale
        part = _dot(a.astype(BF16), wd_ref[c0:c0 + chunk, :])
        acc = part if acc is None else acc + part
    return acc


def _attn_ffn_kernel(x_ref, o_ref, wo_ref, g_ref, wg_ref, wu_ref, wd_ref, h_ref):
    h1 = x_ref[...] + _dot(o_ref[...], wo_ref[...])
    hn = _rms(h1, g_ref[...]).astype(BF16)
    h_ref[...] = h1 + _swiglu_acc(hn, wg_ref, wu_ref, wd_ref, FFN_DIM // 2)


def _attn_ffn(x, o, wo, g, wg, wu, wd, tm=512):
    T = x.shape[0]
    return pl.pallas_call(
        _attn_ffn_kernel,
        grid=(T // tm,),
        in_specs=[
            pl.BlockSpec((tm, D_MODEL), lambda i: (i, 0)),
            pl.BlockSpec((tm, D_MODEL), lambda i: (i, 0)),
            _resident(wo.shape), _resident(g.shape), _resident(wg.shape), _resident(wu.shape),
            _resident(wd.shape),
        ],
        out_specs=pl.BlockSpec((tm, D_MODEL), lambda i: (i, 0)),
        out_shape=jax.ShapeDtypeStruct((T, D_MODEL), F32),
        compiler_params=_cparams(("parallel",)),
        name="attn_ffn",
    )(x, o, wo, g, wg, wu, wd)


def _dsw_proj_kernel(h_ref, pos_ref, gq_ref, gkv_ref, wq_ref, wk_ref, wv_ref, freq_ref, sign_ref,
                     q1, k1, v1, q2, k2, v2, q3, k3, v3,
                     xq_scr, xkv_scr, cos_scr, sin_scr, pq_scr, pkv_scr, pcos_scr, psin_scr, *, tm):
    h = h_ref[...]
    r = lax.rsqrt(jnp.mean(h * h, axis=-1, keepdims=True) + RMS_EPS)
    hr = h * r
    nblk = D_MODEL // LANES
    xq_f = hr * gq_ref[...]
    xkv_f = hr * gkv_ref[...]
    for cb in range(nblk):
        xq_scr[cb] = xq_f[:, cb * LANES:(cb + 1) * LANES]
        xkv_scr[cb] = xkv_f[:, cb * LANES:(cb + 1) * LANES]
    ang = pos_ref[...].astype(F32) * freq_ref[...]
    cos_scr[...] = jnp.cos(ang)
    sin_scr[...] = jnp.sin(ang) * sign_ref[...]
    scale = LOG2E / math.sqrt(DSW_HD)
    outs = ((q1, k1, v1), (q2, k2, v2), (q3, k3, v3))
    for g, (_, d) in enumerate(DSW_BRANCHES):
        n = tm // d
        q_out, k_out, v_out = outs[g]
        if d == 1:
            xq = xq_f.astype(BF16)
            xkv = xkv_f.astype(BF16)
            cos = cos_scr[...]
            sin = sin_scr[...]
        else:
            for rr in range(d):
                rows = slice(rr * n, (rr + 1) * n)
                for cb in range(nblk):
                    cols = slice(cb * LANES, (cb + 1) * LANES)
                    pq_scr[rows, cols] = xq_scr[cb, pl.ds(rr, n, stride=d), :].astype(BF16)
                    pkv_scr[rows, cols] = xkv_scr[cb, pl.ds(rr, n, stride=d), :].astype(BF16)
                pcos_scr[rows, :] = cos_scr[pl.ds(rr, n, stride=d), :]
                psin_scr[rows, :] = sin_scr[pl.ds(rr, n, stride=d), :]
            xq = pq_scr[...]
            xkv = pkv_scr[...]
            cos = pcos_scr[...]
            sin = psin_scr[...]
        q = _dot(xq, wq_ref[g])
        k = _dot(xkv, wk_ref[g])
        v = _dot(xkv, wv_ref[g])
        cos_q = cos * scale
        sin_q = sin * scale
        for j in range(DSW_HEADS // 2):
            sl = slice(j * LANES, (j + 1) * LANES)
            qj = _rope(q[:, sl], cos_q, sin_q).astype(BF16)
            kj = _rope(k[:, sl], cos, sin).astype(BF16)
            vj = v[:, sl].astype(BF16)
            for rr in range(d):
                q_out[j, rr] = qj[rr * n:(rr + 1) * n]
                k_out[j, rr] = kj[rr * n:(rr + 1) * n]
                v_out[j, rr] = vj[rr * n:(rr + 1) * n]


def _dsw_proj(h, pos, gq, gkv, wq, wk, wv, freq, sign, tm=512):
    B, S, _ = h.shape
    P = DSW_HEADS // 2
    out_specs, out_shape = [], []
    for _, d in DSW_BRANCHES:
        for _ in range(3):
            out_specs.append(pl.BlockSpec((None, P, d, tm // d, LANES), lambda b, i: (b, 0, 0, i, 0)))
            out_shape.append(jax.ShapeDtypeStruct((B, P, d, S // d, LANES), BF16))
    return pl.pallas_call(
        functools.partial(_dsw_proj_kernel, tm=tm),
        grid=(B, S // tm),
        in_specs=[
            pl.BlockSpec((None, tm, D_MODEL), lambda b, i: (b, i, 0)),
            pl.BlockSpec((None, tm, 1), lambda b, i: (b, i, 0)),
            _resident(gq.shape), _resident(gkv.shape), _resident(wq.shape), _resident(wk.shape),
            _resident(wv.shape), _resident(freq.shape), _resident(sign.shape),
        ],
        out_specs=out_specs,
        out_shape=out_shape,
        scratch_shapes=[
            pltpu.VMEM((D_MODEL // LANES, tm, LANES), F32), pltpu.VMEM((D_MODEL // LANES, tm, LANES), F32),
            pltpu.VMEM((tm, LANES), F32), pltpu.VMEM((tm, LANES), F32),
            pltpu.VMEM((tm, D_MODEL), BF16), pltpu.VMEM((tm, D_MODEL), BF16),
            pltpu.VMEM((tm, LANES), F32), pltpu.VMEM((tm, LANES), F32),
        ],
        compiler_params=_cparams(("parallel", "parallel")),
        name="dsw_proj",
    )(h, pos, gq, gkv, wq, wk, wv, freq, sign)


def _dsw_attn_kernel(q1, k1, v1, q2, k2, v2, q3, k3, v3, o_ref,
                     o1_scr, l1_scr, o2_scr, l2_scr, o3_scr, l3_scr, s_scr, *, S):
    blk = DSW_BLOCK
    lane = lax.broadcasted_iota(jnp.int32, (blk, LANES), 1)
    mask_a = (lane < 8) | ((lane >= 16) & (lane < 72))
    first_half = lane < DSW_HD
    row1 = lax.broadcasted_iota(jnp.int32, (2 * blk, blk), 0)
    col1 = lax.broadcasted_iota(jnp.int32, (2 * blk, blk), 1)
    row1 = jnp.where(row1 >= blk, row1 - blk, row1)
    valid_first = col1 <= row1
    row2 = lax.broadcasted_iota(jnp.int32, (2 * blk, 2 * blk), 0)
    col2 = lax.broadcasted_iota(jnp.int32, (2 * blk, 2 * blk), 1)
    row2 = jnp.where(row2 >= blk, row2 - blk, row2)
    valid_next = ((col2 < blk) & (col2 >= row2)) | ((col2 >= blk) & (col2 - blk <= row2))

    def scores(q_ref, k_ref, off, nkeys):
        qf = q_ref[off:off + blk, :].astype(F32)
        q2 = jnp.concatenate([jnp.where(mask_a, qf, 0.0), jnp.where(mask_a, 0.0, qf)], axis=0).astype(BF16)
        return _dot_nt(q2, k_ref[off + blk - nkeys:off + blk, :])

    def finish(s, v_ref, off, valid, o_tok, l_tok, tok0, d):
        nkeys = valid.shape[1]
        s = jnp.where(valid, s, NEG_INF)
        m = jnp.max(s, axis=-1, keepdims=True)
        p = jnp.exp2(s - m)
        l = jnp.sum(p, axis=-1, keepdims=True)
        o2 = _dot(p.astype(BF16), v_ref[off + blk - nkeys:off + blk, :]) / l
        lse = m + jnp.log2(l)
        rows = pl.ds(tok0, blk, stride=d) if d > 1 else pl.ds(tok0, blk)
        o_tok[rows, :] = jnp.where(first_half, o2[:blk], o2[blk:])
        l_tok[rows, :] = jnp.where(first_half, lse[:blk], lse[blk:])

    for (_, d), (q_ref, k_ref, v_ref, o_tok, l_tok) in zip(
            DSW_BRANCHES, ((q1, k1, v1, o1_scr, l1_scr), (q2, k2, v2, o2_scr, l2_scr), (q3, k3, v3, o3_scr, l3_scr))):
        L = S // d
        blocks = [(rr * L + ii * blk, ii * blk * d + rr, valid_first if ii == 0 else valid_next)
                  for rr in range(d) for ii in range(L // blk)]
        off, _, valid = blocks[0]
        s_scr[:, :valid.shape[1]] = scores(q_ref, k_ref, off, valid.shape[1])
        for n, (off, tok0, valid) in enumerate(blocks):
            s = s_scr[:, :valid.shape[1]]
            if n + 1 < len(blocks):
                noff, _, nvalid = blocks[n + 1]
                s_scr[:, :nvalid.shape[1]] = scores(q_ref, k_ref, noff, nvalid.shape[1])
            finish(s, v_ref, off, valid, o_tok, l_tok, tok0, d)

    ch = 256
    for c0 in range(0, S, ch):
        sl = slice(c0, c0 + ch)
        la, lb, lc = l1_scr[sl, :], l2_scr[sl, :], l3_scr[sl, :]
        m = jnp.maximum(jnp.maximum(la, lb), lc)
        wa, wb, wc = jnp.exp2(la - m), jnp.exp2(lb - m), jnp.exp2(lc - m)
        num = wa * o1_scr[sl, :] + wb * o2_scr[sl, :] + wc * o3_scr[sl, :]
        o_ref[sl, :] = (num / (wa + wb + wc)).astype(BF16)


def _dsw_attn(qkv):
    B, P = qkv[0].shape[0], qkv[0].shape[1]
    S = qkv[0].shape[2] * qkv[0].shape[3]
    flat = [a.reshape(B, P, S, LANES) for a in qkv]
    spec = pl.BlockSpec((None, None, S, LANES), lambda b, j: (b, j, 0, 0))
    return pl.pallas_call(
        functools.partial(_dsw_attn_kernel, S=S),
        grid=(B, P),
        in_specs=[spec] * 9,
        out_specs=pl.BlockSpec((None, S, LANES), lambda b, j: (b, 0, j)),
        out_shape=jax.ShapeDtypeStruct((B, S, P * LANES), BF16),
        scratch_shapes=[pltpu.VMEM((S, LANES), F32)] * 6 + [pltpu.VMEM((2 * DSW_BLOCK, 2 * DSW_BLOCK), F32)],
        compiler_params=_cparams(("parallel", "parallel")),
        name="dsw_attn",
    )(*flat)


META_E1, META_E2, META_R1, META_R2, META_G1, META_G2 = range(6)


def _dsw_router_kernel(h_ref, o_ref, wo_ref, g_ref, wr_ref, h3_ref, hn_ref, meta_ref, cnt_ref, base_scr):
    @pl.when(pl.program_id(0) == 0)
    def _():
        base_scr[...] = jnp.zeros_like(base_scr)

    h3 = h_ref[...] + _dot(o_ref[...], wo_ref[...])
    h3_ref[...] = h3
    hn = _rms(h3, g_ref[...])
    hn_ref[...] = hn
    hi = hn.astype(BF16)
    lo = (hn - hi.astype(F32)).astype(BF16)
    part = _dot(hi, wr_ref[...])
    logits = part[:, :LANES] + part[:, LANES:] + _dot(lo, wr_ref[:, :LANES])
    tm = logits.shape[0]
    lane = lax.broadcasted_iota(jnp.int32, logits.shape, 1).astype(F32)
    logits = jnp.where(lane < N_EXPERTS, logits, -jnp.inf)
    v1 = jnp.max(logits, axis=-1, keepdims=True)
    i1 = jnp.min(jnp.where(logits == v1, lane, float(LANES)), axis=-1, keepdims=True)
    rest = jnp.where(lane == i1, -jnp.inf, logits)
    v2 = jnp.max(rest, axis=-1, keepdims=True)
    i2 = jnp.min(jnp.where(rest == v2, lane, float(LANES)), axis=-1, keepdims=True)
    e2 = jnp.exp(v2 - v1)
    den = 1.0 + e2
    onehot = jnp.where((lane == i1) | (lane == i2), 1.0, 0.0)
    row = lax.broadcasted_iota(jnp.int32, (tm, tm), 0)
    col = lax.broadcasted_iota(jnp.int32, (tm, tm), 1)
    earlier = jnp.where(col < row, 1.0, 0.0).astype(BF16)
    before = _dot(earlier, onehot.astype(BF16)) + base_scr[0:1, :]
    r1 = jnp.sum(jnp.where(lane == i1, before, 0.0), axis=-1, keepdims=True)
    r2 = jnp.sum(jnp.where(lane == i2, before, 0.0), axis=-1, keepdims=True)
    total = base_scr[0:1, :] + jnp.sum(onehot, axis=0, keepdims=True)
    base_scr[...] = jnp.broadcast_to(total, base_scr.shape)
    cnt_ref[...] = jnp.broadcast_to(total, cnt_ref.shape)
    meta = jnp.zeros_like(logits)
    for k, val in ((META_E1, i1), (META_E2, i2), (META_R1, r1), (META_R2, r2),
                   (META_G1, 1.0 / den), (META_G2, e2 / den)):
        meta = jnp.where(lane == float(k), val, meta)
    meta_ref[...] = meta


def _dsw_router(h, o, wo, g, wr, tm=512):
    T = h.shape[0]
    return pl.pallas_call(
        _dsw_router_kernel,
        grid=(T // tm,),
        in_specs=[
            pl.BlockSpec((tm, D_MODEL), lambda i: (i, 0)),
            pl.BlockSpec((tm, D_MODEL), lambda i: (i, 0)),
            _resident(wo.shape), _resident(g.shape), _resident(wr.shape),
        ],
        out_specs=[
            pl.BlockSpec((tm, D_MODEL), lambda i: (i, 0)),
            pl.BlockSpec((tm, D_MODEL), lambda i: (i, 0)),
            pl.BlockSpec((tm, LANES), lambda i: (i, 0)),
            pl.BlockSpec((8, LANES), lambda i: (0, 0)),
        ],
        out_shape=[
            jax.ShapeDtypeStruct((T, D_MODEL), F32),
            jax.ShapeDtypeStruct((T, D_MODEL), F32),
            jax.ShapeDtypeStruct((T, LANES), F32),
            jax.ShapeDtypeStruct((8, LANES), F32),
        ],
        scratch_shapes=[pltpu.VMEM((8, LANES), F32)],
        compiler_params=_cparams(("arbitrary",)),
        name="dsw_router",
    )(h, o, wo, g, wr)


MOE_ROWS = 512


def _row_copy(src, src_row, dst, dst_row, sem):
    return pltpu.make_async_copy(src.at[pl.ds(src_row, 1)], dst.at[pl.ds(dst_row, 1)], sem)


def _moe_dispatch_kernel(pad_start_ref, pad_len_ref, nu_ref, d1_ref, d2_ref, hn_ref, xs_hbm, zero_scr, sem, zsem,
                         *, tm):
    @pl.when(pl.program_id(0) == 0)
    def _():
        zero_scr[...] = jnp.zeros_like(zero_scr)
        R = zero_scr.shape[0]

        def zero_tile(j, c):
            cp = pltpu.make_async_copy(zero_scr, xs_hbm.at[pl.ds(pl.multiple_of(j * R, R), R)], zsem)
            cp.start()
            cp.wait()
            return c

        lax.fori_loop(nu_ref[0], xs_hbm.shape[0] // R, zero_tile, 0)
        def pad_copies(act):
            for e in range(N_EXPERTS):
                n, start = pad_len_ref[e], pad_start_ref[e]
                head = n & 7

                def row(r, c, start=start):
                    act(_row_copy(zero_scr, 0, xs_hbm, start + r, zsem))
                    return c

                lax.fori_loop(0, head, row, 0)
                p = R // 2
                while p >= 8:
                    @pl.when((n & p) != 0)
                    def _(p=p, n=n, start=start, head=head):
                        at = pl.multiple_of(start + head + ((n - head) & (-2 * p)), 8)
                        act(pltpu.make_async_copy(zero_scr.at[pl.ds(0, p)], xs_hbm.at[pl.ds(at, p)], zsem))
                    p //= 2

        pad_copies(lambda cp: cp.start())
        pad_copies(lambda cp: cp.wait())

    def body(r, c):
        _row_copy(hn_ref, r, xs_hbm, d1_ref[0, r], sem).start(priority=0)
        _row_copy(hn_ref, r, xs_hbm, d2_ref[0, r], sem).start(priority=1)
        return c

    lax.fori_loop(0, tm, body, 0, unroll=8)
    for _ in range(2):
        pltpu.make_async_copy(hn_ref, xs_hbm.at[pl.ds(0, tm)], sem).wait()


def _moe_dispatch(pad_start, pad_len, n_used, d1, d2, hn, n_rows, tm=512):
    T = hn.shape[0]
    smem = lambda: pl.BlockSpec((None, 1, tm), lambda i, ps, pn, nu: (i, 0, 0), memory_space=pltpu.SMEM)
    return pl.pallas_call(
        functools.partial(_moe_dispatch_kernel, tm=tm),
        grid_spec=pltpu.PrefetchScalarGridSpec(
            num_scalar_prefetch=3,
            grid=(T // tm,),
            in_specs=[smem(), smem(), pl.BlockSpec((tm, D_MODEL), lambda i, ps, pn, nu: (i, 0))],
            out_specs=pl.BlockSpec(memory_space=pl.ANY),
            scratch_shapes=[pltpu.VMEM((MOE_ROWS, D_MODEL), F32), pltpu.SemaphoreType.DMA,
                            pltpu.SemaphoreType.DMA],
        ),
        out_shape=jax.ShapeDtypeStruct((n_rows, D_MODEL), F32),
        compiler_params=_cparams(("arbitrary",)),
        name="moe_dispatch",
    )(pad_start, pad_len, n_used, d1.reshape(T // tm, 1, tm), d2.reshape(T // tm, 1, tm), hn)


def _moe_group_kernel(te_ref, nu_ref, xs_ref, wg_ref, wu_ref, wd_ref, ys_ref, xb_scr, acc_scr):
    j = pl.program_id(0)
    f = pl.program_id(1)
    used = j < nu_ref[0]

    @pl.when(used & (f == 0))
    def _():
        xb_scr[...] = xs_ref[...].astype(BF16)
        acc_scr[...] = jnp.zeros_like(acc_scr)

    @pl.when(used)
    def _():
        acc_scr[...] += _swiglu_acc(xb_scr[...], wg_ref, wu_ref, wd_ref, wg_ref.shape[-1] // 2)

    @pl.when(f == pl.num_programs(1) - 1)
    def _():
        ys_ref[...] = jnp.where(used, acc_scr[...], 0.0)


def _moe_group(tile_expert, n_used, xs, wg, wu, wd, tf=1792):
    n_rows = xs.shape[0]
    R = MOE_ROWS
    E, _, Fe = wg.shape
    nf = Fe // tf

    def w_col(j, f, te, nu):
        return jnp.where(j < nu[0], f, nf - 1)

    return pl.pallas_call(
        _moe_group_kernel,
        grid_spec=pltpu.PrefetchScalarGridSpec(
            num_scalar_prefetch=2,
            grid=(n_rows // R, nf),
            in_specs=[
                pl.BlockSpec((R, D_MODEL), lambda j, f, te, nu: (jnp.maximum(jnp.minimum(j, nu[0] - 1), 0), 0)),
                pl.BlockSpec((None, D_MODEL, tf), lambda j, f, te, nu: (te[j], 0, w_col(j, f, te, nu))),
                pl.BlockSpec((None, D_MODEL, tf), lambda j, f, te, nu: (te[j], 0, w_col(j, f, te, nu))),
                pl.BlockSpec((None, tf, D_MODEL), lambda j, f, te, nu: (te[j], w_col(j, f, te, nu), 0)),
            ],
            out_specs=pl.BlockSpec((R, D_MODEL), lambda j, f, te, nu: (j, 0)),
            scratch_shapes=[pltpu.VMEM((R, D_MODEL), BF16), pltpu.VMEM((R, D_MODEL), F32)],
        ),
        out_shape=jax.ShapeDtypeStruct((n_rows, D_MODEL), F32),
        compiler_params=_cparams(("arbitrary", "arbitrary")),
        name="moe_group",
    )(tile_expert, n_used, xs, wg, wu, wd)


def _moe_combine_kernel(d1_ref, d2_ref, n1_ref, n2_ref, meta_ref, h3_ref, gf_ref, ys_hbm, out_ref,
                        y_scr, sems, *, tm):
    i = pl.program_id(0)
    slot = i % 2

    def gather(da_ref, db_ref, s):
        def body(r, c):
            _row_copy(ys_hbm, da_ref[0, r], y_scr.at[s, 0], r, sems.at[s]).start(priority=0)
            _row_copy(ys_hbm, db_ref[0, r], y_scr.at[s, 1], r, sems.at[s]).start(priority=1)
            return c

        lax.fori_loop(0, tm, body, 0, unroll=8)

    @pl.when(i == 0)
    def _():
        gather(d1_ref, d2_ref, 0)

    @pl.when(i + 1 < pl.num_programs(0))
    def _():
        gather(n1_ref, n2_ref, 1 - slot)

    for k in range(2):
        pltpu.make_async_copy(ys_hbm.at[pl.ds(0, tm)], y_scr.at[slot, k], sems.at[slot]).wait()
    meta = meta_ref[...]
    lane = lax.broadcasted_iota(jnp.int32, meta.shape, 1)
    g1 = jnp.sum(jnp.where(lane == META_G1, meta, 0.0), axis=-1, keepdims=True)
    g2 = jnp.sum(jnp.where(lane == META_G2, meta, 0.0), axis=-1, keepdims=True)
    out_ref[...] = _rms(h3_ref[...] + g1 * y_scr[slot, 0] + g2 * y_scr[slot, 1], gf_ref[...])


def _moe_combine(d1, d2, meta, h3, gf, ys, tm=512):
    T = h3.shape[0]
    n = T // tm
    smem = lambda nxt: pl.BlockSpec((None, 1, tm), lambda i: (jnp.minimum(i + nxt, n - 1), 0, 0),
                                    memory_space=pltpu.SMEM)
    d1 = d1.reshape(n, 1, tm)
    d2 = d2.reshape(n, 1, tm)
    return pl.pallas_call(
        functools.partial(_moe_combine_kernel, tm=tm),
        grid=(n,),
        in_specs=[
            smem(0), smem(0), smem(1), smem(1),
            pl.BlockSpec((tm, LANES), lambda i: (i, 0)),
            pl.BlockSpec((tm, D_MODEL), lambda i: (i, 0)),
            pl.BlockSpec((1, D_MODEL), lambda i: (0, 0)),
            pl.BlockSpec(memory_space=pl.ANY),
        ],
        out_specs=pl.BlockSpec((tm, D_MODEL), lambda i: (i, 0)),
        out_shape=jax.ShapeDtypeStruct((T, D_MODEL), F32),
        scratch_shapes=[pltpu.VMEM((2, 2, tm, D_MODEL), F32), pltpu.SemaphoreType.DMA((2,))],
        compiler_params=_cparams(("arbitrary",)),
        name="moe_combine",
    )(d1, d2, d1, d2, meta, h3, gf, ys)


def _moe_plan(meta, counts, n_tiles):
    R = MOE_ROWS
    cnt = counts[0, :N_EXPERTS].astype(jnp.int32)
    tiles = (cnt + R - 1) // R
    tile_end = jnp.cumsum(tiles)
    row_start = (tile_end - tiles) * R
    e1 = meta[:, META_E1].astype(jnp.int32)
    e2 = meta[:, META_E2].astype(jnp.int32)
    d1 = row_start[e1] + meta[:, META_R1].astype(jnp.int32)
    d2 = row_start[e2] + meta[:, META_R2].astype(jnp.int32)
    n_used = tile_end[-1:]
    j = jnp.minimum(jnp.arange(n_tiles, dtype=jnp.int32), n_used - 1)
    tile_expert = jnp.sum(j[:, None] >= tile_end[None, :], axis=1).astype(jnp.int32)
    pad_start = (row_start + cnt).astype(jnp.int32)
    pad_len = (tiles * R - cnt).astype(jnp.int32)
    return d1, d2, tile_expert, n_used.astype(jnp.int32), pad_start, pad_len


def _lanes(parts):
    ref = next(p for p in parts if not isinstance(p, int))
    return jnp.concatenate([jnp.zeros(ref.shape[:-1] + (p,), ref.dtype) if isinstance(p, int) else p
                            for p in parts], axis=-1)


def _mla_head_lanes(nope, rope):
    half = MLA_ROPE // 2
    cut = 64 - half
    r1, r2 = (half, half) if isinstance(rope, int) else (rope[..., :half], rope[..., half:])
    n1, n2 = (cut, MLA_NOPE - cut) if isinstance(nope, int) else (nope[..., :cut], nope[..., cut:])
    return _lanes([r1, n1, r2, n2, LANES - MLA_QK])


def _dsw_pair_lanes(w):
    half = DSW_ROT // 2
    w = w.reshape(w.shape[0], DSW_HEADS // 2, 2, DSW_HD)
    a, b = w[:, :, 0, :], w[:, :, 1, :]
    blk = _lanes([a[..., :half], b[..., :half], a[..., DSW_ROT:], a[..., half:DSW_ROT], b[..., half:DSW_ROT],
                  b[..., DSW_ROT:]])
    return blk.reshape(w.shape[0], DSW_WIDTH)


def _rope_lane_tables(half, copies):
    inv_freq = jnp.power(jnp.float32(ROPE_THETA), -jnp.arange(half, dtype=F32) * (2.0 / (2 * half)))
    on = copies * half
    group = [inv_freq] * copies + [64 - on]
    sign = np.zeros(LANES, np.float32)
    sign[:on], sign[64:64 + on] = -1.0, 1.0
    return _lanes(group + group).reshape(1, LANES), jnp.asarray(sign).reshape(1, LANES)


def kernel(x, positions, norm_attn, norm_ffn, mla_w_down, mla_q_norm, mla_w_uq, mla_kv_norm, mla_w_ukv,
           mla_w_o, dsw_kv_norm, dsw_w_kv, dsw_w_q, dsw_w_o, ffn_w_gate, ffn_w_up, ffn_w_down, moe_router,
           moe_w_gate, moe_w_up, moe_w_down, final_norm):
    B, S, D = x.shape
    T = B * S
    pos = positions.reshape(B, S, 1)

    H = MLA_HEADS
    n_lat = MLA_Q_LORA + MLA_KV_LORA
    w_down = mla_w_down.reshape(D, n_lat + MLA_ROPE)
    wdn = _lanes([w_down[:, :n_lat], _mla_head_lanes(MLA_NOPE, w_down[:, n_lat:])]).astype(BF16)
    w_uq = mla_w_uq.reshape(MLA_Q_LORA, H, MLA_QK)
    wuq = _mla_head_lanes(w_uq[..., :MLA_NOPE], w_uq[..., MLA_NOPE:]).reshape(MLA_Q_LORA, H * LANES).astype(BF16)
    w_ukv = mla_w_ukv.reshape(MLA_KV_LORA, H, MLA_NOPE + MLA_V)
    wk = _mla_head_lanes(w_ukv[..., :MLA_NOPE], MLA_ROPE).reshape(MLA_KV_LORA, H * LANES).astype(BF16)
    wv = w_ukv[..., MLA_NOPE:].reshape(MLA_KV_LORA, H * MLA_V).astype(BF16)
    mla_freq, mla_sign = _rope_lane_tables(MLA_ROPE // 2, 1)

    q, k, v = _mla_proj(x, pos, norm_attn[0:1], wdn, mla_q_norm[0:1], mla_kv_norm[0:1], wuq, wk, wv,
                        mla_freq, mla_sign, tm=MLA_TILE)
    o, (moe_wg, moe_wu, moe_wd) = _mla_attn(q, k, v, (moe_w_gate.reshape(N_EXPERTS, D, EXPERT_DIM),
                                                      moe_w_up.reshape(N_EXPERTS, D, EXPERT_DIM),
                                                      moe_w_down.reshape(N_EXPERTS, EXPERT_DIM, D)))
    h = _attn_ffn(x.reshape(T, D), o.reshape(T, D), mla_w_o[0].astype(BF16), norm_ffn[0:1],
                  ffn_w_gate[0].astype(BF16), ffn_w_up[0].astype(BF16), ffn_w_down[0].astype(BF16))

    w_q = dsw_w_q.reshape(D, N_BR * DSW_WIDTH)
    branch = lambda w, g: w[:, g * DSW_WIDTH:(g + 1) * DSW_WIDTH]
    wq_b = jnp.stack([_dsw_pair_lanes(branch(w_q, g)) for g in range(N_BR)]).astype(BF16)
    wk_b = jnp.stack([_dsw_pair_lanes(branch(dsw_w_kv, g)) for g in range(N_BR)]).astype(BF16)
    wv_b = jnp.stack([branch(dsw_w_kv, N_BR + g) for g in range(N_BR)]).astype(BF16)
    dsw_freq, dsw_sign = _rope_lane_tables(DSW_ROT // 2, 2)

    qkv = _dsw_proj(h.reshape(B, S, D), pos, norm_attn[1:2], dsw_kv_norm.reshape(1, D), wq_b, wk_b, wv_b,
                    dsw_freq, dsw_sign)
    o = _dsw_attn(qkv)
    wr = jnp.pad(moe_router[0], ((0, 0), (0, LANES - N_EXPERTS)))
    wr_hi = wr.astype(BF16)
    wr = jnp.concatenate([wr_hi, (wr - wr_hi.astype(F32)).astype(BF16)], axis=1)
    h3, hn, meta, counts = _dsw_router(h, o.reshape(T, D), dsw_w_o[0].astype(BF16), norm_ffn[1:2], wr)

    n_tiles = (2 * T) // MOE_ROWS + N_EXPERTS
    d1, d2, tile_expert, n_used, pad_start, pad_len = _moe_plan(meta, counts, n_tiles)
    xs = _moe_dispatch(pad_start, pad_len, n_used, d1, d2, hn, n_tiles * MOE_ROWS)
    ys = _moe_group(tile_expert, n_used, xs, moe_wg, moe_wu, moe_wd)
    out = _moe_combine(d1, d2, meta, h3, final_norm.reshape(1, D), ys)
    return out.reshape(B, S, D)
```

```python
import functools
import math

import numpy as np
import jax
import jax.numpy as jnp
from jax import lax
from jax.experimental import pallas as pl
from jax.experimental.pallas import tpu as pltpu

F32 = jnp.float32
BF16 = jnp.bfloat16

D_MODEL = 1024
RMS_EPS = 1e-6
ROPE_THETA = 500000.0
NEG_INF = -1e30
LANES = 128

MLA_HEADS = 16
MLA_NOPE = 64
MLA_ROPE = 32
MLA_V = 64
MLA_QK = MLA_NOPE + MLA_ROPE
MLA_Q_LORA = 384
MLA_KV_LORA = 256

DSW_BRANCHES = ((128, 1), (512, 4), (2048, 16))
N_BR = 3
DSW_HEADS = 16
DSW_HD = 64
DSW_ROT = 16
DSW_WIDTH = DSW_HEADS * DSW_HD
DSW_BLOCK = 128

FFN_DIM = 2816
N_EXPERTS = 8
EXPERT_DIM = 3584

VMEM_LIMIT = 56 * 1024 * 1024


def _cparams(sem):
    return pltpu.CompilerParams(dimension_semantics=sem, vmem_limit_bytes=VMEM_LIMIT)


def _resident(shape):
    nd = len(shape)
    return pl.BlockSpec(shape, lambda *_: (0,) * nd, pipeline_mode=pl.Buffered(1))


def _rms(x, g):
    return x * lax.rsqrt(jnp.mean(x * x, axis=-1, keepdims=True) + RMS_EPS) * g


def _dot(a, b):
    return jnp.dot(a, b, preferred_element_type=F32)


def _dot_nt(a, b):
    return lax.dot_general(a, b, (((1,), (1,)), ((), ())), preferred_element_type=F32)


def _rope(t, cos, sin_signed):
    return t * cos + pltpu.roll(t, 64, 1) * sin_signed


def _mla_proj_kernel(x_ref, pos_ref, g_ref, wdn_ref, qn_ref, kvn_ref, wuq_ref, wk_ref, wv_ref,
                     freq_ref, sign_ref, q_ref, k_ref, v_ref):
    xn = _rms(x_ref[...], g_ref[...]).astype(BF16)
    down = _dot(xn, wdn_ref[...])
    cq = _rms(down[:, :MLA_Q_LORA], qn_ref[...]).astype(BF16)
    ckv = _rms(down[:, MLA_Q_LORA:MLA_Q_LORA + MLA_KV_LORA], kvn_ref[...]).astype(BF16)
    kr = down[:, MLA_Q_LORA + MLA_KV_LORA:]
    ang = pos_ref[...].astype(F32) * freq_ref[...]
    cos = jnp.cos(ang)
    sin = jnp.sin(ang) * sign_ref[...]
    kr = _rope(kr, cos, sin)
    scale = LOG2E / math.sqrt(MLA_QK)
    cos_q = cos * scale
    sin_q = sin * scale
    q = _dot(cq, wuq_ref[...])
    for h in range(MLA_HEADS):
        q_ref[h] = _rope(q[:, h * LANES:(h + 1) * LANES], cos_q, sin_q).T.astype(BF16)
    kp = _dot(ckv, wk_ref[...])
    for h in range(MLA_HEADS):
        k_ref[h] = (kp[:, h * LANES:(h + 1) * LANES] + kr).astype(BF16)
    v = _dot(ckv, wv_ref[...])
    for j in range(MLA_HEADS // 2):
        v_ref[j] = v[:, j * LANES:(j + 1) * LANES].T.astype(BF16)


def _mla_proj(x, pos, g, wdn, qn, kvn, wuq, wk, wv, freq, sign, tm):
    B, S, _ = x.shape
    H = MLA_HEADS
    return pl.pallas_call(
        _mla_proj_kernel,
        grid=(B, S // tm),
        in_specs=[
            pl.BlockSpec((None, tm, D_MODEL), lambda b, i: (b, i, 0)),
            pl.BlockSpec((None, tm, 1), lambda b, i: (b, i, 0)),
            _resident(g.shape), _resident(wdn.shape), _resident(qn.shape), _resident(kvn.shape),
            _resident(wuq.shape), _resident(wk.shape), _resident(wv.shape),
            _resident(freq.shape), _resident(sign.shape),
        ],
        out_specs=[
            pl.BlockSpec((None, H, None, LANES, tm), lambda b, i: (b, 0, i, 0, 0)),
            pl.BlockSpec((None, H, tm, LANES), lambda b, i: (b, 0, i, 0)),
            pl.BlockSpec((None, H // 2, None, LANES, tm), lambda b, i: (b, 0, i, 0, 0)),
        ],
        out_shape=[
            jax.ShapeDtypeStruct((B, H, S // tm, LANES, tm), BF16),
            jax.ShapeDtypeStruct((B, H, S, LANES), BF16),
            jax.ShapeDtypeStruct((B, H // 2, S // tm, LANES, tm), BF16),
        ],
        compiler_params=_cparams(("parallel", "parallel")),
        name="mla_proj",
    )(x, pos, g, wdn, qn, kvn, wuq, wk, wv, freq, sign)


MLA_TILE = 512
MLA_ONES = 16
LOG2E = 1.4426950408889634

def _mla_attn_kernel(qT_ref, k_ref, vT_ref, *rest, t, n_cast):
    cast_in, o_ref, cast_out = rest[:n_cast], rest[n_cast], rest[n_cast + 1:2 * n_cast + 1]
    m_scr, acc_scr, s_scr = rest[2 * n_cast + 1:]
    for src, dst in zip(cast_in, cast_out):
        dst[...] = src[...].astype(BF16)
    n = qT_ref.shape[1]
    ones = jnp.ones((MLA_ONES, t), BF16)
    key = lax.broadcasted_iota(jnp.int32, (t, t), 0)
    qry = lax.broadcasted_iota(jnp.int32, (t, t), 1)

    def scores(i, c, hh):
        return _dot(k_ref[hh, c * t:(c + 1) * t, :], qT_ref[hh, i])

    def update(s, c, hh):
        m_prev = m_scr[hh]
        m_new = jnp.maximum(m_prev, jnp.max(s, axis=0, keepdims=True))
        alpha = jnp.exp2(m_prev - m_new)
        p = jnp.exp2(s - m_new).astype(BF16)
        lhs = jnp.concatenate([vT_ref[c, hh * MLA_V:(hh + 1) * MLA_V, :], ones], axis=0)
        acc_scr[hh] = alpha * acc_scr[hh] + _dot(lhs, p)
        m_scr[hh] = m_new

    pairs = [(i, c) for i in range(n) for c in range(i + 1)]
    for hh in range(2):
        s_scr[hh] = scores(0, 0, hh)
    for idx, (i, c) in enumerate(pairs):
        if c == 0:
            m_scr[...] = jnp.full(m_scr.shape, NEG_INF, F32)
            acc_scr[...] = jnp.zeros(acc_scr.shape, F32)
        for hh in range(2):
            s = s_scr[hh]
            if idx + 1 < len(pairs):
                s_scr[hh] = scores(*pairs[idx + 1], hh)
            if c == i:
                s = jnp.where(key <= qry, s, NEG_INF)
            update(s, c, hh)
        if c == i:
            o = [acc_scr[hh, :MLA_V, :] * (1.0 / acc_scr[hh, MLA_V:MLA_V + 1, :]) for hh in range(2)]
            o_ref[i * t:(i + 1) * t, :] = jnp.concatenate(o, axis=0).T.astype(BF16)


def _mla_attn(qT, k, vT, cast=()):
    B, H, n, _, t = qT.shape
    S = n * t
    P = H // 2
    steps = B * P
    cast_specs, cast_shapes = [], []
    for w in cast:
        E, rows, cols = w.shape
        cr = E * rows // steps
        assert E * rows == cr * steps and rows % cr == 0 and cr % 16 == 0, (w.shape, steps)
        per = rows // cr
        cast_specs.append(pl.BlockSpec(
            (None, cr, cols), lambda b, j, per=per: ((b * P + j) // per, (b * P + j) % per, 0)))
        cast_shapes.append(jax.ShapeDtypeStruct(w.shape, BF16))
    outs = pl.pallas_call(
        functools.partial(_mla_attn_kernel, t=t, n_cast=len(cast)),
        grid=(B, P),
        in_specs=[
            pl.BlockSpec((None, 2, n, LANES, t), lambda b, j: (b, j, 0, 0, 0)),
            pl.BlockSpec((None, 2, S, LANES), lambda b, j: (b, j, 0, 0)),
            pl.BlockSpec((None, None, n, LANES, t), lambda b, j: (b, j, 0, 0, 0)),
        ] + cast_specs,
        out_specs=[pl.BlockSpec((None, S, LANES), lambda b, j: (b, 0, j))] + cast_specs,
        out_shape=[jax.ShapeDtypeStruct((B, S, P * LANES), BF16)] + cast_shapes,
        scratch_shapes=[pltpu.VMEM((2, 1, t), F32), pltpu.VMEM((2, MLA_V + MLA_ONES, t), F32),
                        pltpu.VMEM((2, t, t), F32)],
        compiler_params=_cparams(("parallel", "parallel")),
        name="mla_attn",
    )(qT, k, vT, *cast)
    return outs[0], outs[1:]


def _swiglu_acc(hn, wg_ref, wu_ref, wd_ref, chunk, scale=None):
    acc = None
    n = wg_ref.shape[-1]
    for c0 in range(0, n, chunk):
        g = _dot(hn, wg_ref[:, c0:c0 + chunk])
        u = _dot(hn, wu_ref[:, c0:c0 + chunk])
        a = g * jax.nn.sigmoid(g) * u
        if scale is not None:
            a = a * scale
        part = _dot(a.astype(BF16), wd_ref[c0:c0 + chunk, :])
        acc = part if acc is None else acc + part
    return acc


def _attn_ffn_kernel(x_ref, o_ref, wo_ref, g_ref, wg_ref, wu_ref, wd_ref, h_ref):
    h1 = x_ref[...] + _dot(o_ref[...], wo_ref[...])
    hn = _rms(h1, g_ref[...]).astype(BF16)
    h_ref[...] = h1 + _swiglu_acc(hn, wg_ref, wu_ref, wd_ref, FFN_DIM // 2)


def _attn_ffn(x, o, wo, g, wg, wu, wd, tm=512):
    T = x.shape[0]
    return pl.pallas_call(
        _attn_ffn_kernel,
        grid=(T // tm,),
        in_specs=[
            pl.BlockSpec((tm, D_MODEL), lambda i: (i, 0)),
            pl.BlockSpec((tm, D_MODEL), lambda i: (i, 0)),
            _resident(wo.shape), _resident(g.shape), _resident(wg.shape), _resident(wu.shape),
            _resident(wd.shape),
        ],
        out_specs=pl.BlockSpec((tm, D_MODEL), lambda i: (i, 0)),
        out_shape=jax.ShapeDtypeStruct((T, D_MODEL), F32),
        compiler_params=_cparams(("parallel",)),
        name="attn_ffn",
    )(x, o, wo, g, wg, wu, wd)


def _dsw_proj_kernel(h_ref, pos_ref, gq_ref, gkv_ref, wq_ref, wk_ref, wv_ref, freq_ref, sign_ref,
                     q1, k1, v1, q2, k2, v2, q3, k3, v3,
                     xq_scr, xkv_scr, cos_scr, sin_scr, pq_scr, pkv_scr, pcos_scr, psin_scr, *, tm):
    h = h_ref[...]
    r = lax.rsqrt(jnp.mean(h * h, axis=-1, keepdims=True) + RMS_EPS)
    hr = h * r
    nblk = D_MODEL // LANES
    xq_f = hr * gq_ref[...]
    xkv_f = hr * gkv_ref[...]
    for cb in range(nblk):
        xq_scr[cb] = xq_f[:, cb * LANES:(cb + 1) * LANES]
        xkv_scr[cb] = xkv_f[:, cb * LANES:(cb + 1) * LANES]
    ang = pos_ref[...].astype(F32) * freq_ref[...]
    cos_scr[...] = jnp.cos(ang)
    sin_scr[...] = jnp.sin(ang) * sign_ref[...]
    scale = LOG2E / math.sqrt(DSW_HD)
    outs = ((q1, k1, v1), (q2, k2, v2), (q3, k3, v3))
    for g, (_, d) in enumerate(DSW_BRANCHES):
        n = tm // d
        q_out, k_out, v_out = outs[g]
        if d == 1:
            xq = xq_f.astype(BF16)
            xkv = xkv_f.astype(BF16)
            cos = cos_scr[...]
            sin = sin_scr[...]
        else:
            for rr in range(d):
                rows = slice(rr * n, (rr + 1) * n)
                for cb in range(nblk):
                    cols = slice(cb * LANES, (cb + 1) * LANES)
                    pq_scr[rows, cols] = xq_scr[cb, pl.ds(rr, n, stride=d), :].astype(BF16)
                    pkv_scr[rows, cols] = xkv_scr[cb, pl.ds(rr, n, stride=d), :].astype(BF16)
                pcos_scr[rows, :] = cos_scr[pl.ds(rr, n, stride=d), :]
                psin_scr[rows, :] = sin_scr[pl.ds(rr, n, stride=d), :]
            xq = pq_scr[...]
            xkv = pkv_scr[...]
            cos = pcos_scr[...]
            sin = psin_scr[...]
        q = _dot(xq, wq_ref[g])
        k = _dot(xkv, wk_ref[g])
        v = _dot(xkv, wv_ref[g])
        cos_q = cos * scale
        sin_q = sin * scale
        for j in range(DSW_HEADS // 2):
            sl = slice(j * LANES, (j + 1) * LANES)
            qj = _rope(q[:, sl], cos_q, sin_q).astype(BF16)
            kj = _rope(k[:, sl], cos, sin).astype(BF16)
            vj = v[:, sl].astype(BF16)
            for rr in range(d):
                q_out[j, rr] = qj[rr * n:(rr + 1) * n]
                k_out[j, rr] = kj[rr * n:(rr + 1) * n]
                v_out[j, rr] = vj[rr * n:(rr + 1) * n]


def _dsw_proj(h, pos, gq, gkv, wq, wk, wv, freq, sign, tm=512):
    B, S, _ = h.shape
    P = DSW_HEADS // 2
    out_specs, out_shape = [], []
    for _, d in DSW_BRANCHES:
        for _ in range(3):
            out_specs.append(pl.BlockSpec((None, P, d, tm // d, LANES), lambda b, i: (b, 0, 0, i, 0)))
            out_shape.append(jax.ShapeDtypeStruct((B, P, d, S // d, LANES), BF16))
    return pl.pallas_call(
        functools.partial(_dsw_proj_kernel, tm=tm),
        grid=(B, S // tm),
        in_specs=[
            pl.BlockSpec((None, tm, D_MODEL), lambda b, i: (b, i, 0)),
            pl.BlockSpec((None, tm, 1), lambda b, i: (b, i, 0)),
            _resident(gq.shape), _resident(gkv.shape), _resident(wq.shape), _resident(wk.shape),
            _resident(wv.shape), _resident(freq.shape), _resident(sign.shape),
        ],
        out_specs=out_specs,
        out_shape=out_shape,
        scratch_shapes=[
            pltpu.VMEM((D_MODEL // LANES, tm, LANES), F32), pltpu.VMEM((D_MODEL // LANES, tm, LANES), F32),
            pltpu.VMEM((tm, LANES), F32), pltpu.VMEM((tm, LANES), F32),
            pltpu.VMEM((tm, D_MODEL), BF16), pltpu.VMEM((tm, D_MODEL), BF16),
            pltpu.VMEM((tm, LANES), F32), pltpu.VMEM((tm, LANES), F32),
        ],
        compiler_params=_cparams(("parallel", "parallel")),
        name="dsw_proj",
    )(h, pos, gq, gkv, wq, wk, wv, freq, sign)


def _dsw_attn_kernel(q1, k1, v1, q2, k2, v2, q3, k3, v3, o_ref,
                     o1_scr, l1_scr, o2_scr, l2_scr, o3_scr, l3_scr, s_scr, *, S):
    blk = DSW_BLOCK
    lane = lax.broadcasted_iota(jnp.int32, (blk, LANES), 1)
    mask_a = (lane < 8) | ((lane >= 16) & (lane < 72))
    first_half = lane < DSW_HD
    row1 = lax.broadcasted_iota(jnp.int32, (2 * blk, blk), 0)
    col1 = lax.broadcasted_iota(jnp.int32, (2 * blk, blk), 1)
    row1 = jnp.where(row1 >= blk, row1 - blk, row1)
    valid_first = col1 <= row1
    row2 = lax.broadcasted_iota(jnp.int32, (2 * blk, 2 * blk), 0)
    col2 = lax.broadcasted_iota(jnp.int32, (2 * blk, 2 * blk), 1)
    row2 = jnp.where(row2 >= blk, row2 - blk, row2)
    valid_next = ((col2 < blk) & (col2 >= row2)) | ((col2 >= blk) & (col2 - blk <= row2))

    def scores(q_ref, k_ref, off, nkeys):
        qf = q_ref[off:off + blk, :].astype(F32)
        q2 = jnp.concatenate([jnp.where(mask_a, qf, 0.0), jnp.where(mask_a, 0.0, qf)], axis=0).astype(BF16)
        return _dot_nt(q2, k_ref[off + blk - nkeys:off + blk, :])

    def finish(s, v_ref, off, valid, o_tok, l_tok, tok0, d):
        nkeys = valid.shape[1]
        s = jnp.where(valid, s, NEG_INF)
        m = jnp.max(s, axis=-1, keepdims=True)
        p = jnp.exp2(s - m)
        l = jnp.sum(p, axis=-1, keepdims=True)
        o2 = _dot(p.astype(BF16), v_ref[off + blk - nkeys:off + blk, :]) / l
        lse = m + jnp.log2(l)
        rows = pl.ds(tok0, blk, stride=d) if d > 1 else pl.ds(tok0, blk)
        o_tok[rows, :] = jnp.where(first_half, o2[:blk], o2[blk:])
        l_tok[rows, :] = jnp.where(first_half, lse[:blk], lse[blk:])

    for (_, d), (q_ref, k_ref, v_ref, o_tok, l_tok) in zip(
            DSW_BRANCHES, ((q1, k1, v1, o1_scr, l1_scr), (q2, k2, v2, o2_scr, l2_scr), (q3, k3, v3, o3_scr, l3_scr))):
        L = S // d
        blocks = [(rr * L + ii * blk, ii * blk * d + rr, valid_first if ii == 0 else valid_next)
                  for rr in range(d) for ii in range(L // blk)]
        off, _, valid = blocks[0]
        s_scr[:, :valid.shape[1]] = scores(q_ref, k_ref, off, valid.shape[1])
        for n, (off, tok0, valid) in enumerate(blocks):
            s = s_scr[:, :valid.shape[1]]
            if n + 1 < len(blocks):
                noff, _, nvalid = blocks[n + 1]
                s_scr[:, :nvalid.shape[1]] = scores(q_ref, k_ref, noff, nvalid.shape[1])
            finish(s, v_ref, off, valid, o_tok, l_tok, tok0, d)

    ch = 256
    for c0 in range(0, S, ch):
        sl = slice(c0, c0 + ch)
        la, lb, lc = l1_scr[sl, :], l2_scr[sl, :], l3_scr[sl, :]
        m = jnp.maximum(jnp.maximum(la, lb), lc)
        wa, wb, wc = jnp.exp2(la - m), jnp.exp2(lb - m), jnp.exp2(lc - m)
        num = wa * o1_scr[sl, :] + wb * o2_scr[sl, :] + wc * o3_scr[sl, :]
        o_ref[sl, :] = (num / (wa + wb + wc)).astype(BF16)


def _dsw_attn(qkv):
    B, P = qkv[0].shape[0], qkv[0].shape[1]
    S = qkv[0].shape[2] * qkv[0].shape[3]
    flat = [a.reshape(B, P, S, LANES) for a in qkv]
    spec = pl.BlockSpec((None, None, S, LANES), lambda b, j: (b, j, 0, 0))
    return pl.pallas_call(
        functools.partial(_dsw_attn_kernel, S=S),
        grid=(B, P),
        in_specs=[spec] * 9,
        out_specs=pl.BlockSpec((None, S, LANES), lambda b, j: (b, 0, j)),
        out_shape=jax.ShapeDtypeStruct((B, S, P * LANES), BF16),
        scratch_shapes=[pltpu.VMEM((S, LANES), F32)] * 6 + [pltpu.VMEM((2 * DSW_BLOCK, 2 * DSW_BLOCK), F32)],
        compiler_params=_cparams(("parallel", "parallel")),
        name="dsw_attn",
    )(*flat)


META_E1, META_E2, META_R1, META_R2, META_G1, META_G2 = range(6)


def _dsw_router_kernel(h_ref, o_ref, wo_ref, g_ref, wr_ref, h3_ref, hn_ref, meta_ref, meta_t_ref, cnt_ref,
                       base_scr):
    @pl.when(pl.program_id(0) == 0)
    def _():
        base_scr[...] = jnp.zeros_like(base_scr)

    h3 = h_ref[...] + _dot(o_ref[...], wo_ref[...])
    h3_ref[...] = h3
    hn = _rms(h3, g_ref[...])
    hn_ref[...] = hn
    hi = hn.astype(BF16)
    lo = (hn - hi.astype(F32)).astype(BF16)
    part = _dot(hi, wr_ref[...])
    logits = part[:, :LANES] + part[:, LANES:] + _dot(lo, wr_ref[:, :LANES])
    tm = logits.shape[0]
    lane = lax.broadcasted_iota(jnp.int32, logits.shape, 1).astype(F32)
    logits = jnp.where(lane < N_EXPERTS, logits, -jnp.inf)
    v1 = jnp.max(logits, axis=-1, keepdims=True)
    i1 = jnp.min(jnp.where(logits == v1, lane, float(LANES)), axis=-1, keepdims=True)
    rest = jnp.where(lane == i1, -jnp.inf, logits)
    v2 = jnp.max(rest, axis=-1, keepdims=True)
    i2 = jnp.min(jnp.where(rest == v2, lane, float(LANES)), axis=-1, keepdims=True)
    e2 = jnp.exp(v2 - v1)
    den = 1.0 + e2
    onehot = jnp.where((lane == i1) | (lane == i2), 1.0, 0.0)
    row = lax.broadcasted_iota(jnp.int32, (tm, tm), 0)
    col = lax.broadcasted_iota(jnp.int32, (tm, tm), 1)
    earlier = jnp.where(col < row, 1.0, 0.0).astype(BF16)
    before = _dot(earlier, onehot.astype(BF16)) + base_scr[0:1, :]
    r1 = jnp.sum(jnp.where(lane == i1, before, 0.0), axis=-1, keepdims=True)
    r2 = jnp.sum(jnp.where(lane == i2, before, 0.0), axis=-1, keepdims=True)
    total = base_scr[0:1, :] + jnp.sum(onehot, axis=0, keepdims=True)
    base_scr[...] = jnp.broadcast_to(total, base_scr.shape)
    cnt_ref[...] = jnp.broadcast_to(total, cnt_ref.shape)
    meta = jnp.zeros_like(logits)
    for k, val in ((META_E1, i1), (META_E2, i2), (META_R1, r1), (META_R2, r2),
                   (META_G1, 1.0 / den), (META_G2, e2 / den)):
        meta = jnp.where(lane == float(k), val, meta)
    meta_ref[...] = meta
    meta_t_ref[...] = meta.T[:8, :]


def _dsw_router(h, o, wo, g, wr, tm=512):
    T = h.shape[0]
    return pl.pallas_call(
        _dsw_router_kernel,
        grid=(T // tm,),
        in_specs=[
            pl.BlockSpec((tm, D_MODEL), lambda i: (i, 0)),
            pl.BlockSpec((tm, D_MODEL), lambda i: (i, 0)),
            _resident(wo.shape), _resident(g.shape), _resident(wr.shape),
        ],
        out_specs=[
            pl.BlockSpec((tm, D_MODEL), lambda i: (i, 0)),
            pl.BlockSpec((tm, D_MODEL), lambda i: (i, 0)),
            pl.BlockSpec((tm, LANES), lambda i: (i, 0)),
            pl.BlockSpec((8, tm), lambda i: (0, i)),
            pl.BlockSpec((8, LANES), lambda i: (0, 0)),
        ],
        out_shape=[
            jax.ShapeDtypeStruct((T, D_MODEL), F32),
            jax.ShapeDtypeStruct((T, D_MODEL), F32),
            jax.ShapeDtypeStruct((T, LANES), F32),
            jax.ShapeDtypeStruct((8, T), F32),
            jax.ShapeDtypeStruct((8, LANES), F32),
        ],
        scratch_shapes=[pltpu.VMEM((8, LANES), F32)],
        compiler_params=_cparams(("arbitrary",)),
        name="dsw_router",
    )(h, o, wo, g, wr)


MOE_ROWS = 512


def _row_copy(src, src_row, dst, dst_row, sem):
    return pltpu.make_async_copy(src.at[pl.ds(src_row, 1)], dst.at[pl.ds(dst_row, 1)], sem)


def _moe_dispatch_kernel(pad_start_ref, pad_len_ref, nu_ref, d1_ref, d2_ref, hn_ref, xs_hbm, zero_scr, sem, zsem,
                         *, tm):
    @pl.when(pl.program_id(0) == 0)
    def _():
        zero_scr[...] = jnp.zeros_like(zero_scr)
        R = zero_scr.shape[0]

        def zero_tile(j, c):
            cp = pltpu.make_async_copy(zero_scr, xs_hbm.at[pl.ds(pl.multiple_of(j * R, R), R)], zsem)
            cp.start()
            cp.wait()
            return c

        lax.fori_loop(nu_ref[0], xs_hbm.shape[0] // R, zero_tile, 0)
        def pad_copies(act):
            for e in range(N_EXPERTS):
                n, start = pad_len_ref[e], pad_start_ref[e]
                head = n & 7

                def row(r, c, start=start):
                    act(_row_copy(zero_scr, 0, xs_hbm, start + r, zsem))
                    return c

                lax.fori_loop(0, head, row, 0)
                p = R // 2
                while p >= 8:
                    @pl.when((n & p) != 0)
                    def _(p=p, n=n, start=start, head=head):
                        at = pl.multiple_of(start + head + ((n - head) & (-2 * p)), 8)
                        act(pltpu.make_async_copy(zero_scr.at[pl.ds(0, p)], xs_hbm.at[pl.ds(at, p)], zsem))
                    p //= 2

        pad_copies(lambda cp: cp.start())
        pad_copies(lambda cp: cp.wait())

    def body(r, c):
        _row_copy(hn_ref, r, xs_hbm, d1_ref[0, r], sem).start(priority=0)
        _row_copy(hn_ref, r, xs_hbm, d2_ref[0, r], sem).start(priority=1)
        return c

    lax.fori_loop(0, tm, body, 0, unroll=8)
    for _ in range(2):
        pltpu.make_async_copy(hn_ref, xs_hbm.at[pl.ds(0, tm)], sem).wait()


def _moe_dispatch(pad_start, pad_len, n_used, d1, d2, hn, n_rows, tm=512):
    T = hn.shape[0]
    smem = lambda: pl.BlockSpec((None, 1, tm), lambda i, ps, pn, nu: (i, 0, 0), memory_space=pltpu.SMEM)
    return pl.pallas_call(
        functools.partial(_moe_dispatch_kernel, tm=tm),
        grid_spec=pltpu.PrefetchScalarGridSpec(
            num_scalar_prefetch=3,
            grid=(T // tm,),
            in_specs=[smem(), smem(), pl.BlockSpec((tm, D_MODEL), lambda i, ps, pn, nu: (i, 0))],
            out_specs=pl.BlockSpec(memory_space=pl.ANY),
            scratch_shapes=[pltpu.VMEM((MOE_ROWS, D_MODEL), F32), pltpu.SemaphoreType.DMA,
                            pltpu.SemaphoreType.DMA],
        ),
        out_shape=jax.ShapeDtypeStruct((n_rows, D_MODEL), F32),
        compiler_params=_cparams(("arbitrary",)),
        name="moe_dispatch",
    )(pad_start, pad_len, n_used, d1.reshape(T // tm, 1, tm), d2.reshape(T // tm, 1, tm), hn)


def _moe_group_kernel(te_ref, nu_ref, xs_ref, wg_ref, wu_ref, wd_ref, ys_ref, xb_scr, acc_scr):
    j = pl.program_id(0)
    f = pl.program_id(1)
    used = j < nu_ref[0]

    @pl.when(used & (f == 0))
    def _():
        xb_scr[...] = xs_ref[...].astype(BF16)
        acc_scr[...] = jnp.zeros_like(acc_scr)

    @pl.when(used)
    def _():
        acc_scr[...] += _swiglu_acc(xb_scr[...], wg_ref, wu_ref, wd_ref, wg_ref.shape[-1] // 2)

    @pl.when(f == pl.num_programs(1) - 1)
    def _():
        ys_ref[...] = jnp.where(used, acc_scr[...], 0.0)


def _moe_group(tile_expert, n_used, xs, wg, wu, wd, tf=1792):
    n_rows = xs.shape[0]
    R = MOE_ROWS
    E, _, Fe = wg.shape
    nf = Fe // tf

    def w_col(j, f, te, nu):
        return jnp.where(j < nu[0], f, nf - 1)

    return pl.pallas_call(
        _moe_group_kernel,
        grid_spec=pltpu.PrefetchScalarGridSpec(
            num_scalar_prefetch=2,
            grid=(n_rows // R, nf),
            in_specs=[
                pl.BlockSpec((R, D_MODEL), lambda j, f, te, nu: (jnp.maximum(jnp.minimum(j, nu[0] - 1), 0), 0)),
                pl.BlockSpec((None, D_MODEL, tf), lambda j, f, te, nu: (te[j], 0, w_col(j, f, te, nu))),
                pl.BlockSpec((None, D_MODEL, tf), lambda j, f, te, nu: (te[j], 0, w_col(j, f, te, nu))),
                pl.BlockSpec((None, tf, D_MODEL), lambda j, f, te, nu: (te[j], w_col(j, f, te, nu), 0)),
            ],
            out_specs=pl.BlockSpec((R, D_MODEL), lambda j, f, te, nu: (j, 0)),
            scratch_shapes=[pltpu.VMEM((R, D_MODEL), BF16), pltpu.VMEM((R, D_MODEL), F32)],
        ),
        out_shape=jax.ShapeDtypeStruct((n_rows, D_MODEL), F32),
        compiler_params=_cparams(("arbitrary", "arbitrary")),
        name="moe_group",
    )(tile_expert, n_used, xs, wg, wu, wd)


def _moe_combine_kernel(d1_ref, d2_ref, n1_ref, n2_ref, meta_ref, h3_ref, gf_ref, ys_hbm, out_ref,
                        y_scr, sems, *, tm):
    i = pl.program_id(0)
    slot = i % 2

    def gather(da_ref, db_ref, s):
        def body(r, c):
            _row_copy(ys_hbm, da_ref[0, r], y_scr.at[s, 0], r, sems.at[s]).start(priority=0)
            _row_copy(ys_hbm, db_ref[0, r], y_scr.at[s, 1], r, sems.at[s]).start(priority=1)
            return c

        lax.fori_loop(0, tm, body, 0, unroll=8)

    @pl.when(i == 0)
    def _():
        gather(d1_ref, d2_ref, 0)

    @pl.when(i + 1 < pl.num_programs(0))
    def _():
        gather(n1_ref, n2_ref, 1 - slot)

    for k in range(2):
        pltpu.make_async_copy(ys_hbm.at[pl.ds(0, tm)], y_scr.at[slot, k], sems.at[slot]).wait()
    meta = meta_ref[...]
    lane = lax.broadcasted_iota(jnp.int32, meta.shape, 1)
    g1 = jnp.sum(jnp.where(lane == META_G1, meta, 0.0), axis=-1, keepdims=True)
    g2 = jnp.sum(jnp.where(lane == META_G2, meta, 0.0), axis=-1, keepdims=True)
    out_ref[...] = _rms(h3_ref[...] + g1 * y_scr[slot, 0] + g2 * y_scr[slot, 1], gf_ref[...])


def _moe_combine(d1, d2, meta, h3, gf, ys, tm=512):
    T = h3.shape[0]
    n = T // tm
    smem = lambda nxt: pl.BlockSpec((None, 1, tm), lambda i: (jnp.minimum(i + nxt, n - 1), 0, 0),
                                    memory_space=pltpu.SMEM)
    d1 = d1.reshape(n, 1, tm)
    d2 = d2.reshape(n, 1, tm)
    return pl.pallas_call(
        functools.partial(_moe_combine_kernel, tm=tm),
        grid=(n,),
        in_specs=[
            smem(0), smem(0), smem(1), smem(1),
            pl.BlockSpec((tm, LANES), lambda i: (i, 0)),
            pl.BlockSpec((tm, D_MODEL), lambda i: (i, 0)),
            pl.BlockSpec((1, D_MODEL), lambda i: (0, 0)),
            pl.BlockSpec(memory_space=pl.ANY),
        ],
        out_specs=pl.BlockSpec((tm, D_MODEL), lambda i: (i, 0)),
        out_shape=jax.ShapeDtypeStruct((T, D_MODEL), F32),
        scratch_shapes=[pltpu.VMEM((2, 2, tm, D_MODEL), F32), pltpu.SemaphoreType.DMA((2,))],
        compiler_params=_cparams(("arbitrary",)),
        name="moe_combine",
    )(d1, d2, d1, d2, meta, h3, gf, ys)


def _moe_plan(meta_t, counts, n_tiles):
    R = MOE_ROWS
    cnt = counts[0, :N_EXPERTS].astype(jnp.int32)
    tiles = (cnt + R - 1) // R
    tile_end = jnp.cumsum(tiles)
    row_start = (tile_end - tiles) * R
    fields = meta_t.astype(jnp.int32)
    d1 = row_start[fields[META_E1]] + fields[META_R1]
    d2 = row_start[fields[META_E2]] + fields[META_R2]
    n_used = tile_end[-1:]
    j = jnp.minimum(jnp.arange(n_tiles, dtype=jnp.int32), n_used - 1)
    tile_expert = jnp.sum(j[:, None] >= tile_end[None, :], axis=1).astype(jnp.int32)
    pad_start = (row_start + cnt).astype(jnp.int32)
    pad_len = (tiles * R - cnt).astype(jnp.int32)
    return d1, d2, tile_expert, n_used.astype(jnp.int32), pad_start, pad_len


def _lanes(parts):
    ref = next(p for p in parts if not isinstance(p, int))
    return jnp.concatenate([jnp.zeros(ref.shape[:-1] + (p,), ref.dtype) if isinstance(p, int) else p
                            for p in parts], axis=-1)


def _mla_head_lanes(nope, rope):
    half = MLA_ROPE // 2
    cut = 64 - half
    r1, r2 = (half, half) if isinstance(rope, int) else (rope[..., :half], rope[..., half:])
    n1, n2 = (cut, MLA_NOPE - cut) if isinstance(nope, int) else (nope[..., :cut], nope[..., cut:])
    return _lanes([r1, n1, r2, n2, LANES - MLA_QK])


def _dsw_pair_lanes(w):
    half = DSW_ROT // 2
    w = w.reshape(w.shape[0], DSW_HEADS // 2, 2, DSW_HD)
    a, b = w[:, :, 0, :], w[:, :, 1, :]
    blk = _lanes([a[..., :half], b[..., :half], a[..., DSW_ROT:], a[..., half:DSW_ROT], b[..., half:DSW_ROT],
                  b[..., DSW_ROT:]])
    return blk.reshape(w.shape[0], DSW_WIDTH)


def _rope_lane_tables(half, copies):
    inv_freq = jnp.power(jnp.float32(ROPE_THETA), -jnp.arange(half, dtype=F32) * (2.0 / (2 * half)))
    on = copies * half
    group = [inv_freq] * copies + [64 - on]
    sign = np.zeros(LANES, np.float32)
    sign[:on], sign[64:64 + on] = -1.0, 1.0
    return _lanes(group + group).reshape(1, LANES), jnp.asarray(sign).reshape(1, LANES)


def kernel(x, positions, norm_attn, norm_ffn, mla_w_down, mla_q_norm, mla_w_uq, mla_kv_norm, mla_w_ukv,
           mla_w_o, dsw_kv_norm, dsw_w_kv, dsw_w_q, dsw_w_o, ffn_w_gate, ffn_w_up, ffn_w_down, moe_router,
           moe_w_gate, moe_w_up, moe_w_down, final_norm):
    B, S, D = x.shape
    T = B * S
    pos = positions.reshape(B, S, 1)

    H = MLA_HEADS
    n_lat = MLA_Q_LORA + MLA_KV_LORA
    w_down = mla_w_down.reshape(D, n_lat + MLA_ROPE)
    wdn = _lanes([w_down[:, :n_lat], _mla_head_lanes(MLA_NOPE, w_down[:, n_lat:])]).astype(BF16)
    w_uq = mla_w_uq.reshape(MLA_Q_LORA, H, MLA_QK)
    wuq = _mla_head_lanes(w_uq[..., :MLA_NOPE], w_uq[..., MLA_NOPE:]).reshape(MLA_Q_LORA, H * LANES).astype(BF16)
    w_ukv = mla_w_ukv.reshape(MLA_KV_LORA, H, MLA_NOPE + MLA_V)
    wk = _mla_head_lanes(w_ukv[..., :MLA_NOPE], MLA_ROPE).reshape(MLA_KV_LORA, H * LANES).astype(BF16)
    wv = w_ukv[..., MLA_NOPE:].reshape(MLA_KV_LORA, H * MLA_V).astype(BF16)
    mla_freq, mla_sign = _rope_lane_tables(MLA_ROPE // 2, 1)

    q, k, v = _mla_proj(x, pos, norm_attn[0:1], wdn, mla_q_norm[0:1], mla_kv_norm[0:1], wuq, wk, wv,
                        mla_freq, mla_sign, tm=MLA_TILE)
    o, (moe_wg, moe_wu, moe_wd) = _mla_attn(q, k, v, (moe_w_gate.reshape(N_EXPERTS, D, EXPERT_DIM),
                                                      moe_w_up.reshape(N_EXPERTS, D, EXPERT_DIM),
                                                      moe_w_down.reshape(N_EXPERTS, EXPERT_DIM, D)))
    h = _attn_ffn(x.reshape(T, D), o.reshape(T, D), mla_w_o[0].astype(BF16), norm_ffn[0:1],
                  ffn_w_gate[0].astype(BF16), ffn_w_up[0].astype(BF16), ffn_w_down[0].astype(BF16))

    w_q = dsw_w_q.reshape(D, N_BR * DSW_WIDTH)
    branch = lambda w, g: w[:, g * DSW_WIDTH:(g + 1) * DSW_WIDTH]
    wq_b = jnp.stack([_dsw_pair_lanes(branch(w_q, g)) for g in range(N_BR)]).astype(BF16)
    wk_b = jnp.stack([_dsw_pair_lanes(branch(dsw_w_kv, g)) for g in range(N_BR)]).astype(BF16)
    wv_b = jnp.stack([branch(dsw_w_kv, N_BR + g) for g in range(N_BR)]).astype(BF16)
    dsw_freq, dsw_sign = _rope_lane_tables(DSW_ROT // 2, 2)

    qkv = _dsw_proj(h.reshape(B, S, D), pos, norm_attn[1:2], dsw_kv_norm.reshape(1, D), wq_b, wk_b, wv_b,
                    dsw_freq, dsw_sign)
    o = _dsw_attn(qkv)
    wr = jnp.pad(moe_router[0], ((0, 0), (0, LANES - N_EXPERTS)))
    wr_hi = wr.astype(BF16)
    wr = jnp.concatenate([wr_hi, (wr - wr_hi.astype(F32)).astype(BF16)], axis=1)
    h3, hn, meta, meta_t, counts = _dsw_router(h, o.reshape(T, D), dsw_w_o[0].astype(BF16), norm_ffn[1:2], wr)

    n_tiles = (2 * T) // MOE_ROWS + N_EXPERTS
    d1, d2, tile_expert, n_used, pad_start, pad_len = _moe_plan(meta_t, counts, n_tiles)
    xs = _moe_dispatch(pad_start, pad_len, n_used, d1, d2, hn, n_tiles * MOE_ROWS)
    ys = _moe_group(tile_expert, n_used, xs, moe_wg, moe_wu, moe_wd)
    out = _moe_combine(d1, d2, meta, h3, final_norm.reshape(1, D), ys)
    return out.reshape(B, S, D)
```

```python
import functools
import math

import numpy as np
import jax
import jax.numpy as jnp
from jax import lax
from jax.experimental import pallas as pl
from jax.experimental.pallas import tpu as pltpu

F32 = jnp.float32
BF16 = jnp.bfloat16

D_MODEL = 1024
RMS_EPS = 1e-6
ROPE_THETA = 500000.0
NEG_INF = -1e30
LANES = 128

MLA_HEADS = 16
MLA_NOPE = 64
MLA_ROPE = 32
MLA_V = 64
MLA_QK = MLA_NOPE + MLA_ROPE
MLA_Q_LORA = 384
MLA_KV_LORA = 256

DSW_BRANCHES = ((128, 1), (512, 4), (2048, 16))
N_BR = 3
DSW_HEADS = 16
DSW_HD = 64
DSW_ROT = 16
DSW_WIDTH = DSW_HEADS * DSW_HD
DSW_BLOCK = 128

FFN_DIM = 2816
N_EXPERTS = 8
EXPERT_DIM = 3584

VMEM_LIMIT = 56 * 1024 * 1024


def _cparams(sem):
    return pltpu.CompilerParams(dimension_semantics=sem, vmem_limit_bytes=VMEM_LIMIT)


def _resident(shape):
    nd = len(shape)
    return pl.BlockSpec(shape, lambda *_: (0,) * nd, pipeline_mode=pl.Buffered(1))


def _rms(x, g):
    return x * lax.rsqrt(jnp.mean(x * x, axis=-1, keepdims=True) + RMS_EPS) * g


def _dot(a, b):
    return jnp.dot(a, b, preferred_element_type=F32)


def _dot_nt(a, b):
    return lax.dot_general(a, b, (((1,), (1,)), ((), ())), preferred_element_type=F32)


def _rope(t, cos, sin_signed):
    return t * cos + pltpu.roll(t, 64, 1) * sin_signed


def _mla_proj_kernel(x_ref, pos_ref, g_ref, wdn_ref, qn_ref, kvn_ref, wuq_ref, wk_ref, wv_ref,
                     freq_ref, sign_ref, q_ref, k_ref, v_ref):
    xn = _rms(x_ref[...], g_ref[...]).astype(BF16)
    down = _dot(xn, wdn_ref[...])
    cq = _rms(down[:, :MLA_Q_LORA], qn_ref[...]).astype(BF16)
    ckv = _rms(down[:, MLA_Q_LORA:MLA_Q_LORA + MLA_KV_LORA], kvn_ref[...]).astype(BF16)
    kr = down[:, MLA_Q_LORA + MLA_KV_LORA:]
    ang = pos_ref[...].astype(F32) * freq_ref[...]
    cos = jnp.cos(ang)
    sin = jnp.sin(ang) * sign_ref[...]
    kr = _rope(kr, cos, sin)
    scale = LOG2E / math.sqrt(MLA_QK)
    cos_q = cos * scale
    sin_q = sin * scale
    q = _dot(cq, wuq_ref[...])
    for h in range(MLA_HEADS):
        q_ref[h] = _rope(q[:, h * LANES:(h + 1) * LANES], cos_q, sin_q).T.astype(BF16)
    kp = _dot(ckv, wk_ref[...])
    for h in range(MLA_HEADS):
        k_ref[h] = (kp[:, h * LANES:(h + 1) * LANES] + kr).astype(BF16)
    v = _dot(ckv, wv_ref[...])
    for j in range(MLA_HEADS // 2):
        v_ref[j] = v[:, j * LANES:(j + 1) * LANES].T.astype(BF16)


def _mla_proj(x, pos, g, wdn, qn, kvn, wuq, wk, wv, freq, sign, tm):
    B, S, _ = x.shape
    H = MLA_HEADS
    return pl.pallas_call(
        _mla_proj_kernel,
        grid=(B, S // tm),
        in_specs=[
            pl.BlockSpec((None, tm, D_MODEL), lambda b, i: (b, i, 0)),
            pl.BlockSpec((None, tm, 1), lambda b, i: (b, i, 0)),
            _resident(g.shape), _resident(wdn.shape), _resident(qn.shape), _resident(kvn.shape),
            _resident(wuq.shape), _resident(wk.shape), _resident(wv.shape),
            _resident(freq.shape), _resident(sign.shape),
        ],
        out_specs=[
            pl.BlockSpec((None, H, None, LANES, tm), lambda b, i: (b, 0, i, 0, 0)),
            pl.BlockSpec((None, H, tm, LANES), lambda b, i: (b, 0, i, 0)),
            pl.BlockSpec((None, H // 2, None, LANES, tm), lambda b, i: (b, 0, i, 0, 0)),
        ],
        out_shape=[
            jax.ShapeDtypeStruct((B, H, S // tm, LANES, tm), BF16),
            jax.ShapeDtypeStruct((B, H, S, LANES), BF16),
            jax.ShapeDtypeStruct((B, H // 2, S // tm, LANES, tm), BF16),
        ],
        compiler_params=_cparams(("parallel", "parallel")),
        name="mla_proj",
    )(x, pos, g, wdn, qn, kvn, wuq, wk, wv, freq, sign)


MLA_TILE = 512
MLA_ONES = 16
LOG2E = 1.4426950408889634

def _mla_attn_kernel(qT_ref, k_ref, vT_ref, *rest, t, n_cast):
    cast_in, o_ref, cast_out = rest[:n_cast], rest[n_cast], rest[n_cast + 1:2 * n_cast + 1]
    m_scr, acc_scr, s_scr = rest[2 * n_cast + 1:]
    for src, dst in zip(cast_in, cast_out):
        dst[...] = src[...].astype(BF16)
    n = qT_ref.shape[1]
    ones = jnp.ones((MLA_ONES, t), BF16)
    key = lax.broadcasted_iota(jnp.int32, (t, t), 0)
    qry = lax.broadcasted_iota(jnp.int32, (t, t), 1)

    def scores(i, c, hh):
        return _dot(k_ref[hh, c * t:(c + 1) * t, :], qT_ref[hh, i])

    def update(s, c, hh):
        m_prev = m_scr[hh]
        m_new = jnp.maximum(m_prev, jnp.max(s, axis=0, keepdims=True))
        alpha = jnp.exp2(m_prev - m_new)
        p = jnp.exp2(s - m_new).astype(BF16)
        lhs = jnp.concatenate([vT_ref[c, hh * MLA_V:(hh + 1) * MLA_V, :], ones], axis=0)
        acc_scr[hh] = alpha * acc_scr[hh] + _dot(lhs, p)
        m_scr[hh] = m_new

    pairs = [(i, c) for i in range(n) for c in range(i + 1)]
    for hh in range(2):
        s_scr[hh] = scores(0, 0, hh)
    for idx, (i, c) in enumerate(pairs):
        if c == 0:
            m_scr[...] = jnp.full(m_scr.shape, NEG_INF, F32)
            acc_scr[...] = jnp.zeros(acc_scr.shape, F32)
        for hh in range(2):
            s = s_scr[hh]
            if idx + 1 < len(pairs):
                s_scr[hh] = scores(*pairs[idx + 1], hh)
            if c == i:
                s = jnp.where(key <= qry, s, NEG_INF)
            update(s, c, hh)
        if c == i:
            o = [acc_scr[hh, :MLA_V, :] * (1.0 / acc_scr[hh, MLA_V:MLA_V + 1, :]) for hh in range(2)]
            o_ref[i * t:(i + 1) * t, :] = jnp.concatenate(o, axis=0).T.astype(BF16)


def _mla_attn(qT, k, vT, cast=()):
    B, H, n, _, t = qT.shape
    S = n * t
    P = H // 2
    steps = B * P
    cast_specs, cast_shapes = [], []
    for w in cast:
        E, rows, cols = w.shape
        cr = E * rows // steps
        assert E * rows == cr * steps and rows % cr == 0 and cr % 16 == 0, (w.shape, steps)
        per = rows // cr
        cast_specs.append(pl.BlockSpec(
            (None, cr, cols), lambda b, j, per=per: ((b * P + j) // per, (b * P + j) % per, 0)))
        cast_shapes.append(jax.ShapeDtypeStruct(w.shape, BF16))
    outs = pl.pallas_call(
        functools.partial(_mla_attn_kernel, t=t, n_cast=len(cast)),
        grid=(B, P),
        in_specs=[
            pl.BlockSpec((None, 2, n, LANES, t), lambda b, j: (b, j, 0, 0, 0)),
            pl.BlockSpec((None, 2, S, LANES), lambda b, j: (b, j, 0, 0)),
            pl.BlockSpec((None, None, n, LANES, t), lambda b, j: (b, j, 0, 0, 0)),
        ] + cast_specs,
        out_specs=[pl.BlockSpec((None, S, LANES), lambda b, j: (b, 0, j))] + cast_specs,
        out_shape=[jax.ShapeDtypeStruct((B, S, P * LANES), BF16)] + cast_shapes,
        scratch_shapes=[pltpu.VMEM((2, 1, t), F32), pltpu.VMEM((2, MLA_V + MLA_ONES, t), F32),
                        pltpu.VMEM((2, t, t), F32)],
        compiler_params=_cparams(("parallel", "parallel")),
        name="mla_attn",
    )(qT, k, vT, *cast)
    return outs[0], outs[1:]


def _swiglu_acc(hn, wg_ref, wu_ref, wd_ref, chunk, scale=None):
    acc = None
    n = wg_ref.shape[-1]
    for c0 in range(0, n, chunk):
        g = _dot(hn, wg_ref[:, c0:c0 + chunk])
        u = _dot(hn, wu_ref[:, c0:c0 + chunk])
        a = g * jax.nn.sigmoid(g) * u
        if scale is not None:
            a = a * scale
        part = _dot(a.astype(BF16), wd_ref[c0:c0 + chunk, :])
        acc = part if acc is None else acc + part
    return acc


def _attn_ffn_kernel(x_ref, o_ref, wo_ref, g_ref, wg_ref, wu_ref, wd_ref, h_ref):
    h1 = x_ref[...] + _dot(o_ref[...], wo_ref[...])
    hn = _rms(h1, g_ref[...]).astype(BF16)
    h_ref[...] = h1 + _swiglu_acc(hn, wg_ref, wu_ref, wd_ref, FFN_DIM // 2)


def _attn_ffn(x, o, wo, g, wg, wu, wd, tm=512):
    T = x.shape[0]
    return pl.pallas_call(
        _attn_ffn_kernel,
        grid=(T // tm,),
        in_specs=[
            pl.BlockSpec((tm, D_MODEL), lambda i: (i, 0)),
            pl.BlockSpec((tm, D_MODEL), lambda i: (i, 0)),
            _resident(wo.shape), _resident(g.shape), _resident(wg.shape), _resident(wu.shape),
            _resident(wd.shape),
        ],
        out_specs=pl.BlockSpec((tm, D_MODEL), lambda i: (i, 0)),
        out_shape=jax.ShapeDtypeStruct((T, D_MODEL), F32),
        compiler_params=_cparams(("parallel",)),
        name="attn_ffn",
    )(x, o, wo, g, wg, wu, wd)


def _dsw_proj_kernel(h_ref, pos_ref, gq_ref, gkv_ref, wq_ref, wk_ref, wv_ref, freq_ref, sign_ref,
                     q1, k1, v1, q2, k2, v2, q3, k3, v3,
                     xq_scr, xkv_scr, cos_scr, sin_scr, pq_scr, pkv_scr, pcos_scr, psin_scr, *, tm):
    h = h_ref[...]
    r = lax.rsqrt(jnp.mean(h * h, axis=-1, keepdims=True) + RMS_EPS)
    hr = h * r
    nblk = D_MODEL // LANES
    xq_f = hr * gq_ref[...]
    xkv_f = hr * gkv_ref[...]
    for cb in range(nblk):
        xq_scr[cb] = xq_f[:, cb * LANES:(cb + 1) * LANES]
        xkv_scr[cb] = xkv_f[:, cb * LANES:(cb + 1) * LANES]
    ang = pos_ref[...].astype(F32) * freq_ref[...]
    cos_scr[...] = jnp.cos(ang)
    sin_scr[...] = jnp.sin(ang) * sign_ref[...]
    scale = LOG2E / math.sqrt(DSW_HD)
    outs = ((q1, k1, v1), (q2, k2, v2), (q3, k3, v3))
    for g, (_, d) in enumerate(DSW_BRANCHES):
        n = tm // d
        q_out, k_out, v_out = outs[g]
        if d == 1:
            xq = xq_f.astype(BF16)
            xkv = xkv_f.astype(BF16)
            cos = cos_scr[...]
            sin = sin_scr[...]
        else:
            for rr in range(d):
                rows = slice(rr * n, (rr + 1) * n)
                for cb in range(nblk):
                    cols = slice(cb * LANES, (cb + 1) * LANES)
                    pq_scr[rows, cols] = xq_scr[cb, pl.ds(rr, n, stride=d), :].astype(BF16)
                    pkv_scr[rows, cols] = xkv_scr[cb, pl.ds(rr, n, stride=d), :].astype(BF16)
                pcos_scr[rows, :] = cos_scr[pl.ds(rr, n, stride=d), :]
                psin_scr[rows, :] = sin_scr[pl.ds(rr, n, stride=d), :]
            xq = pq_scr[...]
            xkv = pkv_scr[...]
            cos = pcos_scr[...]
            sin = psin_scr[...]
        q = _dot(xq, wq_ref[g])
        k = _dot(xkv, wk_ref[g])
        v = _dot(xkv, wv_ref[g])
        cos_q = cos * scale
        sin_q = sin * scale
        for j in range(DSW_HEADS // 2):
            sl = slice(j * LANES, (j + 1) * LANES)
            qj = _rope(q[:, sl], cos_q, sin_q).astype(BF16)
            kj = _rope(k[:, sl], cos, sin).astype(BF16)
            vj = v[:, sl].astype(BF16)
            for rr in range(d):
                q_out[j, rr] = qj[rr * n:(rr + 1) * n]
                k_out[j, rr] = kj[rr * n:(rr + 1) * n]
                v_out[j, rr] = vj[rr * n:(rr + 1) * n]


def _dsw_proj(h, pos, gq, gkv, wq, wk, wv, freq, sign, tm=512):
    B, S, _ = h.shape
    P = DSW_HEADS // 2
    out_specs, out_shape = [], []
    for _, d in DSW_BRANCHES:
        for _ in range(3):
            out_specs.append(pl.BlockSpec((None, P, d, tm // d, LANES), lambda b, i: (b, 0, 0, i, 0)))
            out_shape.append(jax.ShapeDtypeStruct((B, P, d, S // d, LANES), BF16))
    return pl.pallas_call(
        functools.partial(_dsw_proj_kernel, tm=tm),
        grid=(B, S // tm),
        in_specs=[
            pl.BlockSpec((None, tm, D_MODEL), lambda b, i: (b, i, 0)),
            pl.BlockSpec((None, tm, 1), lambda b, i: (b, i, 0)),
            _resident(gq.shape), _resident(gkv.shape), _resident(wq.shape), _resident(wk.shape),
            _resident(wv.shape), _resident(freq.shape), _resident(sign.shape),
        ],
        out_specs=out_specs,
        out_shape=out_shape,
        scratch_shapes=[
            pltpu.VMEM((D_MODEL // LANES, tm, LANES), F32), pltpu.VMEM((D_MODEL // LANES, tm, LANES), F32),
            pltpu.VMEM((tm, LANES), F32), pltpu.VMEM((tm, LANES), F32),
            pltpu.VMEM((tm, D_MODEL), BF16), pltpu.VMEM((tm, D_MODEL), BF16),
            pltpu.VMEM((tm, LANES), F32), pltpu.VMEM((tm, LANES), F32),
        ],
        compiler_params=_cparams(("parallel", "parallel")),
        name="dsw_proj",
    )(h, pos, gq, gkv, wq, wk, wv, freq, sign)


def _dsw_attn_kernel(q1, k1, v1, q2, k2, v2, q3, k3, v3, o_ref,
                     o1_scr, l1_scr, o2_scr, l2_scr, o3_scr, l3_scr, s_scr, *, S):
    blk = DSW_BLOCK
    lane = lax.broadcasted_iota(jnp.int32, (blk, LANES), 1)
    mask_a = (lane < 8) | ((lane >= 16) & (lane < 72))
    first_half = lane < DSW_HD
    row1 = lax.broadcasted_iota(jnp.int32, (2 * blk, blk), 0)
    col1 = lax.broadcasted_iota(jnp.int32, (2 * blk, blk), 1)
    row1 = jnp.where(row1 >= blk, row1 - blk, row1)
    valid_first = col1 <= row1
    row2 = lax.broadcasted_iota(jnp.int32, (2 * blk, 2 * blk), 0)
    col2 = lax.broadcasted_iota(jnp.int32, (2 * blk, 2 * blk), 1)
    row2 = jnp.where(row2 >= blk, row2 - blk, row2)
    valid_next = ((col2 < blk) & (col2 >= row2)) | ((col2 >= blk) & (col2 - blk <= row2))

    def scores(q_ref, k_ref, off, nkeys):
        qf = q_ref[off:off + blk, :].astype(F32)
        q2 = jnp.concatenate([jnp.where(mask_a, qf, 0.0), jnp.where(mask_a, 0.0, qf)], axis=0).astype(BF16)
        return _dot_nt(q2, k_ref[off + blk - nkeys:off + blk, :])

    def finish(s, v_ref, off, valid, o_tok, l_tok, tok0, d):
        nkeys = valid.shape[1]
        s = jnp.where(valid, s, NEG_INF)
        m = jnp.max(s, axis=-1, keepdims=True)
        p = jnp.exp2(s - m)
        l = jnp.sum(p, axis=-1, keepdims=True)
        o2 = _dot(p.astype(BF16), v_ref[off + blk - nkeys:off + blk, :]) / l
        lse = m + jnp.log2(l)
        rows = pl.ds(tok0, blk, stride=d) if d > 1 else pl.ds(tok0, blk)
        o_tok[rows, :] = jnp.where(first_half, o2[:blk], o2[blk:])
        l_tok[rows, :] = jnp.where(first_half, lse[:blk], lse[blk:])

    for (_, d), (q_ref, k_ref, v_ref, o_tok, l_tok) in zip(
            DSW_BRANCHES, ((q1, k1, v1, o1_scr, l1_scr), (q2, k2, v2, o2_scr, l2_scr), (q3, k3, v3, o3_scr, l3_scr))):
        L = S // d
        blocks = [(rr * L + ii * blk, ii * blk * d + rr, valid_first if ii == 0 else valid_next)
                  for rr in range(d) for ii in range(L // blk)]
        off, _, valid = blocks[0]
        s_scr[:, :valid.shape[1]] = scores(q_ref, k_ref, off, valid.shape[1])
        for n, (off, tok0, valid) in enumerate(blocks):
            s = s_scr[:, :valid.shape[1]]
            if n + 1 < len(blocks):
                noff, _, nvalid = blocks[n + 1]
                s_scr[:, :nvalid.shape[1]] = scores(q_ref, k_ref, noff, nvalid.shape[1])
            finish(s, v_ref, off, valid, o_tok, l_tok, tok0, d)

    ch = 256
    for c0 in range(0, S, ch):
        sl = slice(c0, c0 + ch)
        la, lb, lc = l1_scr[sl, :], l2_scr[sl, :], l3_scr[sl, :]
        m = jnp.maximum(jnp.maximum(la, lb), lc)
        wa, wb, wc = jnp.exp2(la - m), jnp.exp2(lb - m), jnp.exp2(lc - m)
        num = wa * o1_scr[sl, :] + wb * o2_scr[sl, :] + wc * o3_scr[sl, :]
        o_ref[sl, :] = (num / (wa + wb + wc)).astype(BF16)


def _dsw_attn(qkv):
    B, P = qkv[0].shape[0], qkv[0].shape[1]
    S = qkv[0].shape[2] * qkv[0].shape[3]
    flat = [a.reshape(B, P, S, LANES) for a in qkv]
    spec = pl.BlockSpec((None, None, S, LANES), lambda b, j: (b, j, 0, 0))
    return pl.pallas_call(
        functools.partial(_dsw_attn_kernel, S=S),
        grid=(B, P),
        in_specs=[spec] * 9,
        out_specs=pl.BlockSpec((None, S, LANES), lambda b, j: (b, 0, j)),
        out_shape=jax.ShapeDtypeStruct((B, S, P * LANES), BF16),
        scratch_shapes=[pltpu.VMEM((S, LANES), F32)] * 6 + [pltpu.VMEM((2 * DSW_BLOCK, 2 * DSW_BLOCK), F32)],
        compiler_params=_cparams(("parallel", "parallel")),
        name="dsw_attn",
    )(*flat)


META_E1, META_E2, META_R1, META_R2, META_G1, META_G2 = range(6)


def _dsw_router_kernel(h_ref, o_ref, wo_ref, g_ref, wr_ref, h3_ref, hn_ref, meta_ref, meta_t_ref, cnt_ref,
                       base_scr):
    @pl.when(pl.program_id(0) == 0)
    def _():
        base_scr[...] = jnp.zeros_like(base_scr)

    h3 = h_ref[...] + _dot(o_ref[...], wo_ref[...])
    h3_ref[...] = h3
    hn = _rms(h3, g_ref[...])
    hn_ref[...] = hn
    hi = hn.astype(BF16)
    lo = (hn - hi.astype(F32)).astype(BF16)
    part = _dot(hi, wr_ref[...])
    logits = part[:, :LANES] + part[:, LANES:] + _dot(lo, wr_ref[:, :LANES])
    tm = logits.shape[0]
    lane = lax.broadcasted_iota(jnp.int32, logits.shape, 1).astype(F32)
    logits = jnp.where(lane < N_EXPERTS, logits, -jnp.inf)
    v1 = jnp.max(logits, axis=-1, keepdims=True)
    i1 = jnp.min(jnp.where(logits == v1, lane, float(LANES)), axis=-1, keepdims=True)
    rest = jnp.where(lane == i1, -jnp.inf, logits)
    v2 = jnp.max(rest, axis=-1, keepdims=True)
    i2 = jnp.min(jnp.where(rest == v2, lane, float(LANES)), axis=-1, keepdims=True)
    e2 = jnp.exp(v2 - v1)
    den = 1.0 + e2
    onehot = jnp.where((lane == i1) | (lane == i2), 1.0, 0.0)
    row = lax.broadcasted_iota(jnp.int32, (tm, tm), 0)
    col = lax.broadcasted_iota(jnp.int32, (tm, tm), 1)
    earlier = jnp.where(col < row, 1.0, 0.0).astype(BF16)
    before = _dot(earlier, onehot.astype(BF16)) + base_scr[0:1, :]
    r1 = jnp.sum(jnp.where(lane == i1, before, 0.0), axis=-1, keepdims=True)
    r2 = jnp.sum(jnp.where(lane == i2, before, 0.0), axis=-1, keepdims=True)
    total = base_scr[0:1, :] + jnp.sum(onehot, axis=0, keepdims=True)
    base_scr[...] = jnp.broadcast_to(total, base_scr.shape)
    cnt_ref[...] = jnp.broadcast_to(total, cnt_ref.shape)
    meta = jnp.zeros_like(logits)
    for k, val in ((META_E1, i1), (META_E2, i2), (META_R1, r1), (META_R2, r2),
                   (META_G1, 1.0 / den), (META_G2, e2 / den)):
        meta = jnp.where(lane == float(k), val, meta)
    meta_ref[...] = meta
    meta_t_ref[...] = meta.T[:8, :]


def _dsw_router(h, o, wo, g, wr, tm=512):
    T = h.shape[0]
    return pl.pallas_call(
        _dsw_router_kernel,
        grid=(T // tm,),
        in_specs=[
            pl.BlockSpec((tm, D_MODEL), lambda i: (i, 0)),
            pl.BlockSpec((tm, D_MODEL), lambda i: (i, 0)),
            _resident(wo.shape), _resident(g.shape), _resident(wr.shape),
        ],
        out_specs=[
            pl.BlockSpec((tm, D_MODEL), lambda i: (i, 0)),
            pl.BlockSpec((tm, D_MODEL), lambda i: (i, 0)),
            pl.BlockSpec((tm, LANES), lambda i: (i, 0)),
            pl.BlockSpec((8, tm), lambda i: (0, i)),
            pl.BlockSpec((8, LANES), lambda i: (0, 0)),
        ],
        out_shape=[
            jax.ShapeDtypeStruct((T, D_MODEL), F32),
            jax.ShapeDtypeStruct((T, D_MODEL), F32),
            jax.ShapeDtypeStruct((T, LANES), F32),
            jax.ShapeDtypeStruct((8, T), F32),
            jax.ShapeDtypeStruct((8, LANES), F32),
        ],
        scratch_shapes=[pltpu.VMEM((8, LANES), F32)],
        compiler_params=_cparams(("arbitrary",)),
        name="dsw_router",
    )(h, o, wo, g, wr)


MOE_ROWS = 512
MOE_SUB = 128


def _row_copy(src, src_row, dst, dst_row, sem):
    return pltpu.make_async_copy(src.at[pl.ds(src_row, 1)], dst.at[pl.ds(dst_row, 1)], sem)


def _moe_dispatch_kernel(pad_start_ref, pad_len_ref, nu_ref, d1_ref, d2_ref, hn_ref, xs_hbm, zero_scr, sem, zsem,
                         *, tm):
    @pl.when(pl.program_id(0) == 0)
    def _():
        zero_scr[...] = jnp.zeros_like(zero_scr)
        R = zero_scr.shape[0]

        def zero_tile(j, c):
            cp = pltpu.make_async_copy(zero_scr, xs_hbm.at[pl.ds(pl.multiple_of(j * R, R), R)], zsem)
            cp.start()
            cp.wait()
            return c

        lax.fori_loop(nu_ref[0], xs_hbm.shape[0] // R, zero_tile, 0)
        def pad_copies(act):
            for e in range(N_EXPERTS):
                n, start = pad_len_ref[e], pad_start_ref[e]
                head = n & 7

                def row(r, c, start=start):
                    act(_row_copy(zero_scr, 0, xs_hbm, start + r, zsem))
                    return c

                lax.fori_loop(0, head, row, 0)
                p = R // 2
                while p >= 8:
                    @pl.when((n & p) != 0)
                    def _(p=p, n=n, start=start, head=head):
                        at = pl.multiple_of(start + head + ((n - head) & (-2 * p)), 8)
                        act(pltpu.make_async_copy(zero_scr.at[pl.ds(0, p)], xs_hbm.at[pl.ds(at, p)], zsem))
                    p //= 2

        pad_copies(lambda cp: cp.start())
        pad_copies(lambda cp: cp.wait())

    def body(r, c):
        _row_copy(hn_ref, r, xs_hbm, d1_ref[0, r], sem).start(priority=0)
        _row_copy(hn_ref, r, xs_hbm, d2_ref[0, r], sem).start(priority=1)
        return c

    lax.fori_loop(0, tm, body, 0, unroll=8)
    for _ in range(2):
        pltpu.make_async_copy(hn_ref, xs_hbm.at[pl.ds(0, tm)], sem).wait()


def _moe_dispatch(pad_start, pad_len, n_used, d1, d2, hn, n_rows, tm=512):
    T = hn.shape[0]
    smem = lambda: pl.BlockSpec((None, 1, tm), lambda i, ps, pn, nu: (i, 0, 0), memory_space=pltpu.SMEM)
    return pl.pallas_call(
        functools.partial(_moe_dispatch_kernel, tm=tm),
        grid_spec=pltpu.PrefetchScalarGridSpec(
            num_scalar_prefetch=3,
            grid=(T // tm,),
            in_specs=[smem(), smem(), pl.BlockSpec((tm, D_MODEL), lambda i, ps, pn, nu: (i, 0))],
            out_specs=pl.BlockSpec(memory_space=pl.ANY),
            scratch_shapes=[pltpu.VMEM((MOE_ROWS, D_MODEL), F32), pltpu.SemaphoreType.DMA,
                            pltpu.SemaphoreType.DMA],
        ),
        out_shape=jax.ShapeDtypeStruct((n_rows, D_MODEL), F32),
        compiler_params=_cparams(("arbitrary",)),
        name="moe_dispatch",
    )(pad_start, pad_len, n_used, d1.reshape(T // tm, 1, tm), d2.reshape(T // tm, 1, tm), hn)


def _moe_group_kernel(te_ref, nu_ref, nr_ref, xs_ref, wg_ref, wu_ref, wd_ref, ys_ref, xb_scr, acc_scr):
    j = pl.program_id(0)
    f = pl.program_id(1)
    used = j < nu_ref[0]

    @pl.when(used & (f == 0))
    def _():
        xb_scr[...] = xs_ref[...].astype(BF16)
        acc_scr[...] = jnp.zeros_like(acc_scr)

    need = (nr_ref[j] + (MOE_SUB - 1)) // MOE_SUB
    for k in range(1, MOE_ROWS // MOE_SUB + 1):
        @pl.when(used & (need == k))
        def _(m=k * MOE_SUB):
            acc_scr[:m, :] += _swiglu_acc(xb_scr[:m, :], wg_ref, wu_ref, wd_ref, wg_ref.shape[-1] // 2)

    @pl.when(f == pl.num_programs(1) - 1)
    def _():
        ys_ref[...] = jnp.where(used, acc_scr[...], 0.0)


def _moe_group(tile_expert, n_used, tile_rows, xs, wg, wu, wd, tf=1792):
    n_rows = xs.shape[0]
    R = MOE_ROWS
    E, _, Fe = wg.shape
    nf = Fe // tf

    def w_col(j, f, te, nu):
        return jnp.where(j < nu[0], f, nf - 1)

    return pl.pallas_call(
        _moe_group_kernel,
        grid_spec=pltpu.PrefetchScalarGridSpec(
            num_scalar_prefetch=3,
            grid=(n_rows // R, nf),
            in_specs=[
                pl.BlockSpec((R, D_MODEL),
                             lambda j, f, te, nu, nr: (jnp.maximum(jnp.minimum(j, nu[0] - 1), 0), 0)),
                pl.BlockSpec((None, D_MODEL, tf), lambda j, f, te, nu, nr: (te[j], 0, w_col(j, f, te, nu))),
                pl.BlockSpec((None, D_MODEL, tf), lambda j, f, te, nu, nr: (te[j], 0, w_col(j, f, te, nu))),
                pl.BlockSpec((None, tf, D_MODEL), lambda j, f, te, nu, nr: (te[j], w_col(j, f, te, nu), 0)),
            ],
            out_specs=pl.BlockSpec((R, D_MODEL), lambda j, f, te, nu, nr: (j, 0)),
            scratch_shapes=[pltpu.VMEM((R, D_MODEL), BF16), pltpu.VMEM((R, D_MODEL), F32)],
        ),
        out_shape=jax.ShapeDtypeStruct((n_rows, D_MODEL), F32),
        compiler_params=_cparams(("arbitrary", "arbitrary")),
        name="moe_group",
    )(tile_expert, n_used, tile_rows, xs, wg, wu, wd)


def _moe_combine_kernel(d1_ref, d2_ref, n1_ref, n2_ref, meta_ref, h3_ref, gf_ref, ys_hbm, out_ref,
                        y_scr, sems, *, tm):
    i = pl.program_id(0)
    slot = i % 2

    def gather(da_ref, db_ref, s):
        def body(r, c):
            _row_copy(ys_hbm, da_ref[0, r], y_scr.at[s, 0], r, sems.at[s]).start(priority=0)
            _row_copy(ys_hbm, db_ref[0, r], y_scr.at[s, 1], r, sems.at[s]).start(priority=1)
            return c

        lax.fori_loop(0, tm, body, 0, unroll=8)

    @pl.when(i == 0)
    def _():
        gather(d1_ref, d2_ref, 0)

    @pl.when(i + 1 < pl.num_programs(0))
    def _():
        gather(n1_ref, n2_ref, 1 - slot)

    for k in range(2):
        pltpu.make_async_copy(ys_hbm.at[pl.ds(0, tm)], y_scr.at[slot, k], sems.at[slot]).wait()
    meta = meta_ref[...]
    lane = lax.broadcasted_iota(jnp.int32, meta.shape, 1)
    g1 = jnp.sum(jnp.where(lane == META_G1, meta, 0.0), axis=-1, keepdims=True)
    g2 = jnp.sum(jnp.where(lane == META_G2, meta, 0.0), axis=-1, keepdims=True)
    out_ref[...] = _rms(h3_ref[...] + g1 * y_scr[slot, 0] + g2 * y_scr[slot, 1], gf_ref[...])


def _moe_combine(d1, d2, meta, h3, gf, ys, tm=512):
    T = h3.shape[0]
    n = T // tm
    smem = lambda nxt: pl.BlockSpec((None, 1, tm), lambda i: (jnp.minimum(i + nxt, n - 1), 0, 0),
                                    memory_space=pltpu.SMEM)
    d1 = d1.reshape(n, 1, tm)
    d2 = d2.reshape(n, 1, tm)
    return pl.pallas_call(
        functools.partial(_moe_combine_kernel, tm=tm),
        grid=(n,),
        in_specs=[
            smem(0), smem(0), smem(1), smem(1),
            pl.BlockSpec((tm, LANES), lambda i: (i, 0)),
            pl.BlockSpec((tm, D_MODEL), lambda i: (i, 0)),
            pl.BlockSpec((1, D_MODEL), lambda i: (0, 0)),
            pl.BlockSpec(memory_space=pl.ANY),
        ],
        out_specs=pl.BlockSpec((tm, D_MODEL), lambda i: (i, 0)),
        out_shape=jax.ShapeDtypeStruct((T, D_MODEL), F32),
        scratch_shapes=[pltpu.VMEM((2, 2, tm, D_MODEL), F32), pltpu.SemaphoreType.DMA((2,))],
        compiler_params=_cparams(("arbitrary",)),
        name="moe_combine",
    )(d1, d2, d1, d2, meta, h3, gf, ys)


def _moe_plan(meta_t, counts, n_tiles):
    R = MOE_ROWS
    cnt = counts[0, :N_EXPERTS].astype(jnp.int32)
    tiles = (cnt + R - 1) // R
    tile_end = jnp.cumsum(tiles)
    row_start = (tile_end - tiles) * R
    fields = meta_t.astype(jnp.int32)
    d1 = row_start[fields[META_E1]] + fields[META_R1]
    d2 = row_start[fields[META_E2]] + fields[META_R2]
    n_used = tile_end[-1:]
    j = jnp.minimum(jnp.arange(n_tiles, dtype=jnp.int32), n_used - 1)
    tile_expert = jnp.sum(j[:, None] >= tile_end[None, :], axis=1).astype(jnp.int32)
    pad_start = (row_start + cnt).astype(jnp.int32)
    pad_len = (tiles * R - cnt).astype(jnp.int32)
    tile_rows = jnp.clip(pad_start[tile_expert] - j * R, 0, R).astype(jnp.int32)
    return d1, d2, tile_expert, n_used.astype(jnp.int32), pad_start, pad_len, tile_rows


def _lanes(parts):
    ref = next(p for p in parts if not isinstance(p, int))
    return jnp.concatenate([jnp.zeros(ref.shape[:-1] + (p,), ref.dtype) if isinstance(p, int) else p
                            for p in parts], axis=-1)


def _mla_head_lanes(nope, rope):
    half = MLA_ROPE // 2
    cut = 64 - half
    r1, r2 = (half, half) if isinstance(rope, int) else (rope[..., :half], rope[..., half:])
    n1, n2 = (cut, MLA_NOPE - cut) if isinstance(nope, int) else (nope[..., :cut], nope[..., cut:])
    return _lanes([r1, n1, r2, n2, LANES - MLA_QK])


def _dsw_pair_lanes(w):
    half = DSW_ROT // 2
    w = w.reshape(w.shape[0], DSW_HEADS // 2, 2, DSW_HD)
    a, b = w[:, :, 0, :], w[:, :, 1, :]
    blk = _lanes([a[..., :half], b[..., :half], a[..., DSW_ROT:], a[..., half:DSW_ROT], b[..., half:DSW_ROT],
                  b[..., DSW_ROT:]])
    return blk.reshape(w.shape[0], DSW_WIDTH)


def _rope_lane_tables(half, copies):
    inv_freq = jnp.power(jnp.float32(ROPE_THETA), -jnp.arange(half, dtype=F32) * (2.0 / (2 * half)))
    on = copies * half
    group = [inv_freq] * copies + [64 - on]
    sign = np.zeros(LANES, np.float32)
    sign[:on], sign[64:64 + on] = -1.0, 1.0
    return _lanes(group + group).reshape(1, LANES), jnp.asarray(sign).reshape(1, LANES)


def kernel(x, positions, norm_attn, norm_ffn, mla_w_down, mla_q_norm, mla_w_uq, mla_kv_norm, mla_w_ukv,
           mla_w_o, dsw_kv_norm, dsw_w_kv, dsw_w_q, dsw_w_o, ffn_w_gate, ffn_w_up, ffn_w_down, moe_router,
           moe_w_gate, moe_w_up, moe_w_down, final_norm):
    B, S, D = x.shape
    T = B * S
    pos = positions.reshape(B, S, 1)

    H = MLA_HEADS
    n_lat = MLA_Q_LORA + MLA_KV_LORA
    w_down = mla_w_down.reshape(D, n_lat + MLA_ROPE)
    wdn = _lanes([w_down[:, :n_lat], _mla_head_lanes(MLA_NOPE, w_down[:, n_lat:])]).astype(BF16)
    w_uq = mla_w_uq.reshape(MLA_Q_LORA, H, MLA_QK)
    wuq = _mla_head_lanes(w_uq[..., :MLA_NOPE], w_uq[..., MLA_NOPE:]).reshape(MLA_Q_LORA, H * LANES).astype(BF16)
    w_ukv = mla_w_ukv.reshape(MLA_KV_LORA, H, MLA_NOPE + MLA_V)
    wk = _mla_head_lanes(w_ukv[..., :MLA_NOPE], MLA_ROPE).reshape(MLA_KV_LORA, H * LANES).astype(BF16)
    wv = w_ukv[..., MLA_NOPE:].reshape(MLA_KV_LORA, H * MLA_V).astype(BF16)
    mla_freq, mla_sign = _rope_lane_tables(MLA_ROPE // 2, 1)

    q, k, v = _mla_proj(x, pos, norm_attn[0:1], wdn, mla_q_norm[0:1], mla_kv_norm[0:1], wuq, wk, wv,
                        mla_freq, mla_sign, tm=MLA_TILE)
    o, (moe_wg, moe_wu, moe_wd) = _mla_attn(q, k, v, (moe_w_gate.reshape(N_EXPERTS, D, EXPERT_DIM),
                                                      moe_w_up.reshape(N_EXPERTS, D, EXPERT_DIM),
                                                      moe_w_down.reshape(N_EXPERTS, EXPERT_DIM, D)))
    h = _attn_ffn(x.reshape(T, D), o.reshape(T, D), mla_w_o[0].astype(BF16), norm_ffn[0:1],
                  ffn_w_gate[0].astype(BF16), ffn_w_up[0].astype(BF16), ffn_w_down[0].astype(BF16))

    w_q = dsw_w_q.reshape(D, N_BR * DSW_WIDTH)
    branch = lambda w, g: w[:, g * DSW_WIDTH:(g + 1) * DSW_WIDTH]
    wq_b = jnp.stack([_dsw_pair_lanes(branch(w_q, g)) for g in range(N_BR)]).astype(BF16)
    wk_b = jnp.stack([_dsw_pair_lanes(branch(dsw_w_kv, g)) for g in range(N_BR)]).astype(BF16)
    wv_b = jnp.stack([branch(dsw_w_kv, N_BR + g) for g in range(N_BR)]).astype(BF16)
    dsw_freq, dsw_sign = _rope_lane_tables(DSW_ROT // 2, 2)

    qkv = _dsw_proj(h.reshape(B, S, D), pos, norm_attn[1:2], dsw_kv_norm.reshape(1, D), wq_b, wk_b, wv_b,
                    dsw_freq, dsw_sign)
    o = _dsw_attn(qkv)
    wr = jnp.pad(moe_router[0], ((0, 0), (0, LANES - N_EXPERTS)))
    wr_hi = wr.astype(BF16)
    wr = jnp.concatenate([wr_hi, (wr - wr_hi.astype(F32)).astype(BF16)], axis=1)
    h3, hn, meta, meta_t, counts = _dsw_router(h, o.reshape(T, D), dsw_w_o[0].astype(BF16), norm_ffn[1:2], wr)

    n_tiles = (2 * T) // MOE_ROWS + N_EXPERTS
    d1, d2, tile_expert, n_used, pad_start, pad_len, tile_rows = _moe_plan(meta_t, counts, n_tiles)
    xs = _moe_dispatch(pad_start, pad_len, n_used, d1, d2, hn, n_tiles * MOE_ROWS)
    ys = _moe_group(tile_expert, n_used, tile_rows, xs, moe_wg, moe_wu, moe_wd)
    out = _moe_combine(d1, d2, meta, h3, final_norm.reshape(1, D), ys)
    return out.reshape(B, S, D)
```

```python
import functools
import math

import numpy as np
import jax
import jax.numpy as jnp
from jax import lax
from jax.experimental import pallas as pl
from jax.experimental.pallas import tpu as pltpu

F32 = jnp.float32
BF16 = jnp.bfloat16

D_MODEL = 1024
RMS_EPS = 1e-6
ROPE_THETA = 500000.0
NEG_INF = -1e30
LANES = 128

MLA_HEADS = 16
MLA_NOPE = 64
MLA_ROPE = 32
MLA_V = 64
MLA_QK = MLA_NOPE + MLA_ROPE
MLA_Q_LORA = 384
MLA_KV_LORA = 256

DSW_BRANCHES = ((128, 1), (512, 4), (2048, 16))
N_BR = 3
DSW_HEADS = 16
DSW_HD = 64
DSW_ROT = 16
DSW_WIDTH = DSW_HEADS * DSW_HD
DSW_BLOCK = 128

FFN_DIM = 2816
N_EXPERTS = 8
EXPERT_DIM = 3584

VMEM_LIMIT = 56 * 1024 * 1024


def _cparams(sem):
    return pltpu.CompilerParams(dimension_semantics=sem, vmem_limit_bytes=VMEM_LIMIT)


def _resident(shape):
    nd = len(shape)
    return pl.BlockSpec(shape, lambda *_: (0,) * nd, pipeline_mode=pl.Buffered(1))


def _rms(x, g):
    return x * lax.rsqrt(jnp.mean(x * x, axis=-1, keepdims=True) + RMS_EPS) * g


def _dot(a, b):
    return jnp.dot(a, b, preferred_element_type=F32)


def _dot_nt(a, b):
    return lax.dot_general(a, b, (((1,), (1,)), ((), ())), preferred_element_type=F32)


def _rope(t, cos, sin_signed):
    return t * cos + pltpu.roll(t, 64, 1) * sin_signed


def _mla_proj_kernel(x_ref, pos_ref, g_ref, wdn_ref, qn_ref, kvn_ref, wuq_ref, wk_ref, wv_ref,
                     freq_ref, sign_ref, q_ref, k_ref, v_ref):
    xn = _rms(x_ref[...], g_ref[...]).astype(BF16)
    down = _dot(xn, wdn_ref[...])
    cq = _rms(down[:, :MLA_Q_LORA], qn_ref[...]).astype(BF16)
    ckv = _rms(down[:, MLA_Q_LORA:MLA_Q_LORA + MLA_KV_LORA], kvn_ref[...]).astype(BF16)
    kr = down[:, MLA_Q_LORA + MLA_KV_LORA:]
    ang = pos_ref[...].astype(F32) * freq_ref[...]
    cos = jnp.cos(ang)
    sin = jnp.sin(ang) * sign_ref[...]
    kr = _rope(kr, cos, sin)
    scale = LOG2E / math.sqrt(MLA_QK)
    cos_q = cos * scale
    sin_q = sin * scale
    q = _dot(cq, wuq_ref[...])
    for h in range(MLA_HEADS):
        q_ref[h] = _rope(q[:, h * LANES:(h + 1) * LANES], cos_q, sin_q).T.astype(BF16)
    kp = _dot(ckv, wk_ref[...])
    for h in range(MLA_HEADS):
        k_ref[h] = (kp[:, h * LANES:(h + 1) * LANES] + kr).astype(BF16)
    v = _dot(ckv, wv_ref[...])
    for j in range(MLA_HEADS // 2):
        v_ref[j] = v[:, j * LANES:(j + 1) * LANES].T.astype(BF16)


def _mla_proj(x, pos, g, wdn, qn, kvn, wuq, wk, wv, freq, sign, tm):
    B, S, _ = x.shape
    H = MLA_HEADS
    return pl.pallas_call(
        _mla_proj_kernel,
        grid=(B, S // tm),
        in_specs=[
            pl.BlockSpec((None, tm, D_MODEL), lambda b, i: (b, i, 0)),
            pl.BlockSpec((None, tm, 1), lambda b, i: (b, i, 0)),
            _resident(g.shape), _resident(wdn.shape), _resident(qn.shape), _resident(kvn.shape),
            _resident(wuq.shape), _resident(wk.shape), _resident(wv.shape),
            _resident(freq.shape), _resident(sign.shape),
        ],
        out_specs=[
            pl.BlockSpec((None, H, None, LANES, tm), lambda b, i: (b, 0, i, 0, 0)),
            pl.BlockSpec((None, H, tm, LANES), lambda b, i: (b, 0, i, 0)),
            pl.BlockSpec((None, H // 2, None, LANES, tm), lambda b, i: (b, 0, i, 0, 0)),
        ],
        out_shape=[
            jax.ShapeDtypeStruct((B, H, S // tm, LANES, tm), BF16),
            jax.ShapeDtypeStruct((B, H, S, LANES), BF16),
            jax.ShapeDtypeStruct((B, H // 2, S // tm, LANES, tm), BF16),
        ],
        compiler_params=_cparams(("parallel", "parallel")),
        name="mla_proj",
    )(x, pos, g, wdn, qn, kvn, wuq, wk, wv, freq, sign)


MLA_TILE = 512
MLA_ONES = 16
LOG2E = 1.4426950408889634

def _mla_attn_kernel(qT_ref, k_ref, vT_ref, *rest, t, n_cast):
    cast_in, o_ref, cast_out = rest[:n_cast], rest[n_cast], rest[n_cast + 1:2 * n_cast + 1]
    m_scr, acc_scr, s_scr = rest[2 * n_cast + 1:]
    for src, dst in zip(cast_in, cast_out):
        dst[...] = src[...].astype(BF16)
    n = qT_ref.shape[1]
    ones = jnp.ones((MLA_ONES, t), BF16)
    key = lax.broadcasted_iota(jnp.int32, (t, t), 0)
    qry = lax.broadcasted_iota(jnp.int32, (t, t), 1)

    def scores(i, c, hh):
        return _dot(k_ref[hh, c * t:(c + 1) * t, :], qT_ref[hh, i])

    def update(s, c, hh):
        m_prev = m_scr[hh]
        m_new = jnp.maximum(m_prev, jnp.max(s, axis=0, keepdims=True))
        alpha = jnp.exp2(m_prev - m_new)
        p = jnp.exp2(s - m_new).astype(BF16)
        lhs = jnp.concatenate([vT_ref[c, hh * MLA_V:(hh + 1) * MLA_V, :], ones], axis=0)
        acc_scr[hh] = alpha * acc_scr[hh] + _dot(lhs, p)
        m_scr[hh] = m_new

    pairs = [(i, c) for i in range(n) for c in range(i + 1)]
    for hh in range(2):
        s_scr[hh] = scores(0, 0, hh)
    for idx, (i, c) in enumerate(pairs):
        if c == 0:
            m_scr[...] = jnp.full(m_scr.shape, NEG_INF, F32)
            acc_scr[...] = jnp.zeros(acc_scr.shape, F32)
        for hh in range(2):
            s = s_scr[hh]
            if idx + 1 < len(pairs):
                s_scr[hh] = scores(*pairs[idx + 1], hh)
            if c == i:
                s = jnp.where(key <= qry, s, NEG_INF)
            update(s, c, hh)
        if c == i:
            o = [acc_scr[hh, :MLA_V, :] * (1.0 / acc_scr[hh, MLA_V:MLA_V + 1, :]) for hh in range(2)]
            o_ref[i * t:(i + 1) * t, :] = jnp.concatenate(o, axis=0).T.astype(BF16)


def _mla_attn(qT, k, vT, cast=()):
    B, H, n, _, t = qT.shape
    S = n * t
    P = H // 2
    steps = B * P
    cast_specs, cast_shapes = [], []
    for w in cast:
        E, rows, cols = w.shape
        cr = E * rows // steps
        assert E * rows == cr * steps and rows % cr == 0 and cr % 16 == 0, (w.shape, steps)
        per = rows // cr
        cast_specs.append(pl.BlockSpec(
            (None, cr, cols), lambda b, j, per=per: ((b * P + j) // per, (b * P + j) % per, 0)))
        cast_shapes.append(jax.ShapeDtypeStruct(w.shape, BF16))
    outs = pl.pallas_call(
        functools.partial(_mla_attn_kernel, t=t, n_cast=len(cast)),
        grid=(B, P),
        in_specs=[
            pl.BlockSpec((None, 2, n, LANES, t), lambda b, j: (b, j, 0, 0, 0)),
            pl.BlockSpec((None, 2, S, LANES), lambda b, j: (b, j, 0, 0)),
            pl.BlockSpec((None, None, n, LANES, t), lambda b, j: (b, j, 0, 0, 0)),
        ] + cast_specs,
        out_specs=[pl.BlockSpec((None, S, LANES), lambda b, j: (b, 0, j))] + cast_specs,
        out_shape=[jax.ShapeDtypeStruct((B, S, P * LANES), BF16)] + cast_shapes,
        scratch_shapes=[pltpu.VMEM((2, 1, t), F32), pltpu.VMEM((2, MLA_V + MLA_ONES, t), F32),
                        pltpu.VMEM((2, t, t), F32)],
        compiler_params=_cparams(("parallel", "parallel")),
        name="mla_attn",
    )(qT, k, vT, *cast)
    return outs[0], outs[1:]


def _swiglu_acc(hn, wg_ref, wu_ref, wd_ref, chunk, scale=None):
    acc = None
    n = wg_ref.shape[-1]
    for c0 in range(0, n, chunk):
        g = _dot(hn, wg_ref[:, c0:c0 + chunk])
        u = _dot(hn, wu_ref[:, c0:c0 + chunk])
        a = g * jax.nn.sigmoid(g) * u
        if scale is not None:
            a = a * scale
        part = _dot(a.astype(BF16), wd_ref[c0:c0 + chunk, :])
        acc = part if acc is None else acc + part
    return acc


def _attn_ffn_kernel(x_ref, o_ref, wo_ref, g_ref, wg_ref, wu_ref, wd_ref, h_ref):
    h1 = x_ref[...] + _dot(o_ref[...], wo_ref[...])
    hn = _rms(h1, g_ref[...]).astype(BF16)
    h_ref[...] = h1 + _swiglu_acc(hn, wg_ref, wu_ref, wd_ref, FFN_DIM // 2)


def _attn_ffn(x, o, wo, g, wg, wu, wd, tm=512):
    T = x.shape[0]
    return pl.pallas_call(
        _attn_ffn_kernel,
        grid=(T // tm,),
        in_specs=[
            pl.BlockSpec((tm, D_MODEL), lambda i: (i, 0)),
            pl.BlockSpec((tm, D_MODEL), lambda i: (i, 0)),
            _resident(wo.shape), _resident(g.shape), _resident(wg.shape), _resident(wu.shape),
            _resident(wd.shape),
        ],
        out_specs=pl.BlockSpec((tm, D_MODEL), lambda i: (i, 0)),
        out_shape=jax.ShapeDtypeStruct((T, D_MODEL), F32),
        compiler_params=_cparams(("parallel",)),
        name="attn_ffn",
    )(x, o, wo, g, wg, wu, wd)


def _dsw_proj_kernel(h_ref, pos_ref, gq_ref, gkv_ref, wq_ref, wk_ref, wv_ref, freq_ref, sign_ref,
                     q1, k1, v1, q2, k2, v2, q3, k3, v3,
                     xq_scr, xkv_scr, cos_scr, sin_scr, pq_scr, pkv_scr, pcos_scr, psin_scr, *, tm):
    h = h_ref[...]
    r = lax.rsqrt(jnp.mean(h * h, axis=-1, keepdims=True) + RMS_EPS)
    hr = h * r
    nblk = D_MODEL // LANES
    xq_f = hr * gq_ref[...]
    xkv_f = hr * gkv_ref[...]
    for cb in range(nblk):
        xq_scr[cb] = xq_f[:, cb * LANES:(cb + 1) * LANES]
        xkv_scr[cb] = xkv_f[:, cb * LANES:(cb + 1) * LANES]
    ang = pos_ref[...].astype(F32) * freq_ref[...]
    cos_scr[...] = jnp.cos(ang)
    sin_scr[...] = jnp.sin(ang) * sign_ref[...]
    scale = LOG2E / math.sqrt(DSW_HD)
    outs = ((q1, k1, v1), (q2, k2, v2), (q3, k3, v3))
    for g, (_, d) in enumerate(DSW_BRANCHES):
        n = tm // d
        q_out, k_out, v_out = outs[g]
        if d == 1:
            xq = xq_f.astype(BF16)
            xkv = xkv_f.astype(BF16)
            cos = cos_scr[...]
            sin = sin_scr[...]
        else:
            for rr in range(d):
                rows = slice(rr * n, (rr + 1) * n)
                for cb in range(nblk):
                    cols = slice(cb * LANES, (cb + 1) * LANES)
                    pq_scr[rows, cols] = xq_scr[cb, pl.ds(rr, n, stride=d), :].astype(BF16)
                    pkv_scr[rows, cols] = xkv_scr[cb, pl.ds(rr, n, stride=d), :].astype(BF16)
                pcos_scr[rows, :] = cos_scr[pl.ds(rr, n, stride=d), :]
                psin_scr[rows, :] = sin_scr[pl.ds(rr, n, stride=d), :]
            xq = pq_scr[...]
            xkv = pkv_scr[...]
            cos = pcos_scr[...]
            sin = psin_scr[...]
        q = _dot(xq, wq_ref[g])
        k = _dot(xkv, wk_ref[g])
        v = _dot(xkv, wv_ref[g])
        cos_q = cos * scale
        sin_q = sin * scale
        for j in range(DSW_HEADS // 2):
            sl = slice(j * LANES, (j + 1) * LANES)
            qj = _rope(q[:, sl], cos_q, sin_q).astype(BF16)
            kj = _rope(k[:, sl], cos, sin).astype(BF16)
            vj = v[:, sl].astype(BF16)
            for rr in range(d):
                q_out[j, rr] = qj[rr * n:(rr + 1) * n]
                k_out[j, rr] = kj[rr * n:(rr + 1) * n]
                v_out[j, rr] = vj[rr * n:(rr + 1) * n]


def _dsw_proj(h, pos, gq, gkv, wq, wk, wv, freq, sign, tm=512):
    B, S, _ = h.shape
    P = DSW_HEADS // 2
    out_specs, out_shape = [], []
    for _, d in DSW_BRANCHES:
        for _ in range(3):
            out_specs.append(pl.BlockSpec((None, P, d, tm // d, LANES), lambda b, i: (b, 0, 0, i, 0)))
            out_shape.append(jax.ShapeDtypeStruct((B, P, d, S // d, LANES), BF16))
    return pl.pallas_call(
        functools.partial(_dsw_proj_kernel, tm=tm),
        grid=(B, S // tm),
        in_specs=[
            pl.BlockSpec((None, tm, D_MODEL), lambda b, i: (b, i, 0)),
            pl.BlockSpec((None, tm, 1), lambda b, i: (b, i, 0)),
            _resident(gq.shape), _resident(gkv.shape), _resident(wq.shape), _resident(wk.shape),
            _resident(wv.shape), _resident(freq.shape), _resident(sign.shape),
        ],
        out_specs=out_specs,
        out_shape=out_shape,
        scratch_shapes=[
            pltpu.VMEM((D_MODEL // LANES, tm, LANES), F32), pltpu.VMEM((D_MODEL // LANES, tm, LANES), F32),
            pltpu.VMEM((tm, LANES), F32), pltpu.VMEM((tm, LANES), F32),
            pltpu.VMEM((tm, D_MODEL), BF16), pltpu.VMEM((tm, D_MODEL), BF16),
            pltpu.VMEM((tm, LANES), F32), pltpu.VMEM((tm, LANES), F32),
        ],
        compiler_params=_cparams(("parallel", "parallel")),
        name="dsw_proj",
    )(h, pos, gq, gkv, wq, wk, wv, freq, sign)


def _dsw_attn_kernel(q1, k1, v1, q2, k2, v2, q3, k3, v3, o_ref,
                     o1_scr, l1_scr, o2_scr, l2_scr, o3_scr, l3_scr, s_scr, *, S):
    blk = DSW_BLOCK
    lane = lax.broadcasted_iota(jnp.int32, (blk, LANES), 1)
    mask_a = (lane < 8) | ((lane >= 16) & (lane < 72))
    first_half = lane < DSW_HD
    row1 = lax.broadcasted_iota(jnp.int32, (2 * blk, blk), 0)
    col1 = lax.broadcasted_iota(jnp.int32, (2 * blk, blk), 1)
    row1 = jnp.where(row1 >= blk, row1 - blk, row1)
    valid_first = col1 <= row1
    row2 = lax.broadcasted_iota(jnp.int32, (2 * blk, 2 * blk), 0)
    col2 = lax.broadcasted_iota(jnp.int32, (2 * blk, 2 * blk), 1)
    row2 = jnp.where(row2 >= blk, row2 - blk, row2)
    valid_next = ((col2 < blk) & (col2 >= row2)) | ((col2 >= blk) & (col2 - blk <= row2))

    def scores(q_ref, k_ref, off, nkeys):
        qf = q_ref[off:off + blk, :].astype(F32)
        q2 = jnp.concatenate([jnp.where(mask_a, qf, 0.0), jnp.where(mask_a, 0.0, qf)], axis=0).astype(BF16)
        return _dot_nt(q2, k_ref[off + blk - nkeys:off + blk, :])

    def finish(s, v_ref, off, valid, o_tok, l_tok, tok0, d):
        nkeys = valid.shape[1]
        s = jnp.where(valid, s, NEG_INF)
        m = jnp.max(s, axis=-1, keepdims=True)
        p = jnp.exp2(s - m)
        l = jnp.sum(p, axis=-1, keepdims=True)
        o2 = _dot(p.astype(BF16), v_ref[off + blk - nkeys:off + blk, :]) / l
        lse = m + jnp.log2(l)
        rows = pl.ds(tok0, blk, stride=d) if d > 1 else pl.ds(tok0, blk)
        o_tok[rows, :] = jnp.where(first_half, o2[:blk], o2[blk:])
        l_tok[rows, :] = jnp.where(first_half, lse[:blk], lse[blk:])

    for (_, d), (q_ref, k_ref, v_ref, o_tok, l_tok) in zip(
            DSW_BRANCHES, ((q1, k1, v1, o1_scr, l1_scr), (q2, k2, v2, o2_scr, l2_scr), (q3, k3, v3, o3_scr, l3_scr))):
        L = S // d
        blocks = [(rr * L + ii * blk, ii * blk * d + rr, valid_first if ii == 0 else valid_next)
                  for rr in range(d) for ii in range(L // blk)]
        off, _, valid = blocks[0]
        s_scr[:, :valid.shape[1]] = scores(q_ref, k_ref, off, valid.shape[1])
        for n, (off, tok0, valid) in enumerate(blocks):
            s = s_scr[:, :valid.shape[1]]
            if n + 1 < len(blocks):
                noff, _, nvalid = blocks[n + 1]
                s_scr[:, :nvalid.shape[1]] = scores(q_ref, k_ref, noff, nvalid.shape[1])
            finish(s, v_ref, off, valid, o_tok, l_tok, tok0, d)

    ch = 256
    for c0 in range(0, S, ch):
        sl = slice(c0, c0 + ch)
        la, lb, lc = l1_scr[sl, :], l2_scr[sl, :], l3_scr[sl, :]
        m = jnp.maximum(jnp.maximum(la, lb), lc)
        wa, wb, wc = jnp.exp2(la - m), jnp.exp2(lb - m), jnp.exp2(lc - m)
        num = wa * o1_scr[sl, :] + wb * o2_scr[sl, :] + wc * o3_scr[sl, :]
        o_ref[sl, :] = (num / (wa + wb + wc)).astype(BF16)


def _dsw_attn(qkv):
    B, P = qkv[0].shape[0], qkv[0].shape[1]
    S = qkv[0].shape[2] * qkv[0].shape[3]
    flat = [a.reshape(B, P, S, LANES) for a in qkv]
    spec = pl.BlockSpec((None, None, S, LANES), lambda b, j: (b, j, 0, 0))
    return pl.pallas_call(
        functools.partial(_dsw_attn_kernel, S=S),
        grid=(B, P),
        in_specs=[spec] * 9,
        out_specs=pl.BlockSpec((None, S, LANES), lambda b, j: (b, 0, j)),
        out_shape=jax.ShapeDtypeStruct((B, S, P * LANES), BF16),
        scratch_shapes=[pltpu.VMEM((S, LANES), F32)] * 6 + [pltpu.VMEM((2 * DSW_BLOCK, 2 * DSW_BLOCK), F32)],
        compiler_params=_cparams(("parallel", "parallel")),
        name="dsw_attn",
    )(*flat)


META_E1, META_E2, META_R1, META_R2, META_G1, META_G2 = range(6)


def _dsw_router_kernel(h_ref, o_ref, wo_ref, g_ref, wr_ref, h3_ref, hn_ref, meta_ref, meta_t_ref, cnt_ref,
                       base_scr):
    @pl.when(pl.program_id(0) == 0)
    def _():
        base_scr[...] = jnp.zeros_like(base_scr)

    h3 = h_ref[...] + _dot(o_ref[...], wo_ref[...])
    h3_ref[...] = h3
    hn = _rms(h3, g_ref[...])
    hn_ref[...] = hn
    hi = hn.astype(BF16)
    lo = (hn - hi.astype(F32)).astype(BF16)
    part = _dot(hi, wr_ref[...])
    logits = part[:, :LANES] + part[:, LANES:] + _dot(lo, wr_ref[:, :LANES])
    tm = logits.shape[0]
    lane = lax.broadcasted_iota(jnp.int32, logits.shape, 1).astype(F32)
    logits = jnp.where(lane < N_EXPERTS, logits, -jnp.inf)
    v1 = jnp.max(logits, axis=-1, keepdims=True)
    i1 = jnp.min(jnp.where(logits == v1, lane, float(LANES)), axis=-1, keepdims=True)
    rest = jnp.where(lane == i1, -jnp.inf, logits)
    v2 = jnp.max(rest, axis=-1, keepdims=True)
    i2 = jnp.min(jnp.where(rest == v2, lane, float(LANES)), axis=-1, keepdims=True)
    e2 = jnp.exp(v2 - v1)
    den = 1.0 + e2
    onehot = jnp.where((lane == i1) | (lane == i2), 1.0, 0.0)
    row = lax.broadcasted_iota(jnp.int32, (tm, tm), 0)
    col = lax.broadcasted_iota(jnp.int32, (tm, tm), 1)
    earlier = jnp.where(col < row, 1.0, 0.0).astype(BF16)
    before = _dot(earlier, onehot.astype(BF16)) + base_scr[0:1, :]
    r1 = jnp.sum(jnp.where(lane == i1, before, 0.0), axis=-1, keepdims=True)
    r2 = jnp.sum(jnp.where(lane == i2, before, 0.0), axis=-1, keepdims=True)
    total = base_scr[0:1, :] + jnp.sum(onehot, axis=0, keepdims=True)
    base_scr[...] = jnp.broadcast_to(total, base_scr.shape)
    cnt_ref[...] = jnp.broadcast_to(total, cnt_ref.shape)
    meta = jnp.zeros_like(logits)
    for k, val in ((META_E1, i1), (META_E2, i2), (META_R1, r1), (META_R2, r2),
                   (META_G1, 1.0 / den), (META_G2, e2 / den)):
        meta = jnp.where(lane == float(k), val, meta)
    meta_ref[...] = meta
    meta_t_ref[...] = meta.T[:8, :]


def _dsw_router(h, o, wo, g, wr, tm=512):
    T = h.shape[0]
    return pl.pallas_call(
        _dsw_router_kernel,
        grid=(T // tm,),
        in_specs=[
            pl.BlockSpec((tm, D_MODEL), lambda i: (i, 0)),
            pl.BlockSpec((tm, D_MODEL), lambda i: (i, 0)),
            _resident(wo.shape), _resident(g.shape), _resident(wr.shape),
        ],
        out_specs=[
            pl.BlockSpec((tm, D_MODEL), lambda i: (i, 0)),
            pl.BlockSpec((tm, D_MODEL), lambda i: (i, 0)),
            pl.BlockSpec((tm, LANES), lambda i: (i, 0)),
            pl.BlockSpec((8, tm), lambda i: (0, i)),
            pl.BlockSpec((8, LANES), lambda i: (0, 0)),
        ],
        out_shape=[
            jax.ShapeDtypeStruct((T, D_MODEL), F32),
            jax.ShapeDtypeStruct((T, D_MODEL), F32),
            jax.ShapeDtypeStruct((T, LANES), F32),
            jax.ShapeDtypeStruct((8, T), F32),
            jax.ShapeDtypeStruct((8, LANES), F32),
        ],
        scratch_shapes=[pltpu.VMEM((8, LANES), F32)],
        compiler_params=_cparams(("arbitrary",)),
        name="dsw_router",
    )(h, o, wo, g, wr)


MOE_ROWS = 512
MOE_SUB = 128


def _row_copy(src, src_row, dst, dst_row, sem):
    return pltpu.make_async_copy(src.at[pl.ds(src_row, 1)], dst.at[pl.ds(dst_row, 1)], sem)


def _moe_dispatch_kernel(pad_start_ref, pad_len_ref, nu_ref, d1_ref, d2_ref, hn_ref, xs_hbm, zero_scr, sem, zsem,
                         *, tm):
    @pl.when(pl.program_id(0) == 0)
    def _():
        zero_scr[...] = jnp.zeros_like(zero_scr)
        R = zero_scr.shape[0]

        def zero_tile(j, c):
            cp = pltpu.make_async_copy(zero_scr, xs_hbm.at[pl.ds(pl.multiple_of(j * R, R), R)], zsem)
            cp.start()
            cp.wait()
            return c

        lax.fori_loop(nu_ref[0], xs_hbm.shape[0] // R, zero_tile, 0)
        def pad_copies(act):
            for e in range(N_EXPERTS):
                n, start = pad_len_ref[e], pad_start_ref[e]
                head = n & 7

                def row(r, c, start=start):
                    act(_row_copy(zero_scr, 0, xs_hbm, start + r, zsem))
                    return c

                lax.fori_loop(0, head, row, 0)
                p = R // 2
                while p >= 8:
                    @pl.when((n & p) != 0)
                    def _(p=p, n=n, start=start, head=head):
                        at = pl.multiple_of(start + head + ((n - head) & (-2 * p)), 8)
                        act(pltpu.make_async_copy(zero_scr.at[pl.ds(0, p)], xs_hbm.at[pl.ds(at, p)], zsem))
                    p //= 2

        pad_copies(lambda cp: cp.start())
        pad_copies(lambda cp: cp.wait())

    def body(r, c):
        _row_copy(hn_ref, r, xs_hbm, d1_ref[0, r], sem).start(priority=0)
        _row_copy(hn_ref, r, xs_hbm, d2_ref[0, r], sem).start(priority=1)
        return c

    lax.fori_loop(0, tm, body, 0, unroll=32)
    for _ in range(2):
        pltpu.make_async_copy(hn_ref, xs_hbm.at[pl.ds(0, tm)], sem).wait()


def _moe_dispatch(pad_start, pad_len, n_used, d1, d2, hn, n_rows, tm=512):
    T = hn.shape[0]
    smem = lambda: pl.BlockSpec((None, 1, tm), lambda i, ps, pn, nu: (i, 0, 0), memory_space=pltpu.SMEM)
    return pl.pallas_call(
        functools.partial(_moe_dispatch_kernel, tm=tm),
        grid_spec=pltpu.PrefetchScalarGridSpec(
            num_scalar_prefetch=3,
            grid=(T // tm,),
            in_specs=[smem(), smem(), pl.BlockSpec((tm, D_MODEL), lambda i, ps, pn, nu: (i, 0))],
            out_specs=pl.BlockSpec(memory_space=pl.ANY),
            scratch_shapes=[pltpu.VMEM((MOE_ROWS, D_MODEL), F32), pltpu.SemaphoreType.DMA,
                            pltpu.SemaphoreType.DMA],
        ),
        out_shape=jax.ShapeDtypeStruct((n_rows, D_MODEL), F32),
        compiler_params=_cparams(("arbitrary",)),
        name="moe_dispatch",
    )(pad_start, pad_len, n_used, d1.reshape(T // tm, 1, tm), d2.reshape(T // tm, 1, tm), hn)


def _moe_group_kernel(te_ref, nu_ref, nr_ref, xs_ref, wg_ref, wu_ref, wd_ref, ys_ref, xb_scr, acc_scr):
    j = pl.program_id(0)
    f = pl.program_id(1)
    used = j < nu_ref[0]

    @pl.when(used & (f == 0))
    def _():
        xb_scr[...] = xs_ref[...].astype(BF16)
        acc_scr[...] = jnp.zeros_like(acc_scr)

    need = (nr_ref[j] + (MOE_SUB - 1)) // MOE_SUB
    for k in range(1, MOE_ROWS // MOE_SUB + 1):
        @pl.when(used & (need == k))
        def _(m=k * MOE_SUB):
            acc_scr[:m, :] += _swiglu_acc(xb_scr[:m, :], wg_ref, wu_ref, wd_ref, wg_ref.shape[-1] // 2)

    @pl.when(f == pl.num_programs(1) - 1)
    def _():
        ys_ref[...] = jnp.where(used, acc_scr[...], 0.0)


def _moe_group(tile_expert, n_used, tile_rows, xs, wg, wu, wd, tf=1792):
    n_rows = xs.shape[0]
    R = MOE_ROWS
    E, _, Fe = wg.shape
    nf = Fe // tf

    def w_col(j, f, te, nu):
        return jnp.where(j < nu[0], f, nf - 1)

    return pl.pallas_call(
        _moe_group_kernel,
        grid_spec=pltpu.PrefetchScalarGridSpec(
            num_scalar_prefetch=3,
            grid=(n_rows // R, nf),
            in_specs=[
                pl.BlockSpec((R, D_MODEL),
                             lambda j, f, te, nu, nr: (jnp.maximum(jnp.minimum(j, nu[0] - 1), 0), 0)),
                pl.BlockSpec((None, D_MODEL, tf), lambda j, f, te, nu, nr: (te[j], 0, w_col(j, f, te, nu))),
                pl.BlockSpec((None, D_MODEL, tf), lambda j, f, te, nu, nr: (te[j], 0, w_col(j, f, te, nu))),
                pl.BlockSpec((None, tf, D_MODEL), lambda j, f, te, nu, nr: (te[j], w_col(j, f, te, nu), 0)),
            ],
            out_specs=pl.BlockSpec((R, D_MODEL), lambda j, f, te, nu, nr: (j, 0)),
            scratch_shapes=[pltpu.VMEM((R, D_MODEL), BF16), pltpu.VMEM((R, D_MODEL), F32)],
        ),
        out_shape=jax.ShapeDtypeStruct((n_rows, D_MODEL), F32),
        compiler_params=_cparams(("arbitrary", "arbitrary")),
        name="moe_group",
    )(tile_expert, n_used, tile_rows, xs, wg, wu, wd)


def _moe_combine_kernel(d1_ref, d2_ref, n1_ref, n2_ref, meta_ref, h3_ref, gf_ref, ys_hbm, out_ref,
                        y_scr, sems, *, tm):
    i = pl.program_id(0)
    slot = i % 2

    def gather(da_ref, db_ref, s):
        def body(r, c):
            _row_copy(ys_hbm, da_ref[0, r], y_scr.at[s, 0], r, sems.at[s]).start(priority=0)
            _row_copy(ys_hbm, db_ref[0, r], y_scr.at[s, 1], r, sems.at[s]).start(priority=1)
            return c

        lax.fori_loop(0, tm, body, 0, unroll=32)

    @pl.when(i == 0)
    def _():
        gather(d1_ref, d2_ref, 0)

    @pl.when(i + 1 < pl.num_programs(0))
    def _():
        gather(n1_ref, n2_ref, 1 - slot)

    for k in range(2):
        pltpu.make_async_copy(ys_hbm.at[pl.ds(0, tm)], y_scr.at[slot, k], sems.at[slot]).wait()
    meta = meta_ref[...]
    lane = lax.broadcasted_iota(jnp.int32, meta.shape, 1)
    g1 = jnp.sum(jnp.where(lane == META_G1, meta, 0.0), axis=-1, keepdims=True)
    g2 = jnp.sum(jnp.where(lane == META_G2, meta, 0.0), axis=-1, keepdims=True)
    out_ref[...] = _rms(h3_ref[...] + g1 * y_scr[slot, 0] + g2 * y_scr[slot, 1], gf_ref[...])


def _moe_combine(d1, d2, meta, h3, gf, ys, tm=512):
    T = h3.shape[0]
    n = T // tm
    smem = lambda nxt: pl.BlockSpec((None, 1, tm), lambda i: (jnp.minimum(i + nxt, n - 1), 0, 0),
                                    memory_space=pltpu.SMEM)
    d1 = d1.reshape(n, 1, tm)
    d2 = d2.reshape(n, 1, tm)
    return pl.pallas_call(
        functools.partial(_moe_combine_kernel, tm=tm),
        grid=(n,),
        in_specs=[
            smem(0), smem(0), smem(1), smem(1),
            pl.BlockSpec((tm, LANES), lambda i: (i, 0)),
            pl.BlockSpec((tm, D_MODEL), lambda i: (i, 0)),
            pl.BlockSpec((1, D_MODEL), lambda i: (0, 0)),
            pl.BlockSpec(memory_space=pl.ANY),
        ],
        out_specs=pl.BlockSpec((tm, D_MODEL), lambda i: (i, 0)),
        out_shape=jax.ShapeDtypeStruct((T, D_MODEL), F32),
        scratch_shapes=[pltpu.VMEM((2, 2, tm, D_MODEL), F32), pltpu.SemaphoreType.DMA((2,))],
        compiler_params=_cparams(("arbitrary",)),
        name="moe_combine",
    )(d1, d2, d1, d2, meta, h3, gf, ys)


def _moe_plan(meta_t, counts, n_tiles):
    R = MOE_ROWS
    cnt = counts[0, :N_EXPERTS].astype(jnp.int32)
    tiles = (cnt + R - 1) // R
    tile_end = jnp.cumsum(tiles)
    row_start = (tile_end - tiles) * R
    fields = meta_t.astype(jnp.int32)
    d1 = row_start[fields[META_E1]] + fields[META_R1]
    d2 = row_start[fields[META_E2]] + fields[META_R2]
    n_used = tile_end[-1:]
    j = jnp.minimum(jnp.arange(n_tiles, dtype=jnp.int32), n_used - 1)
    tile_expert = jnp.sum(j[:, None] >= tile_end[None, :], axis=1).astype(jnp.int32)
    pad_start = (row_start + cnt).astype(jnp.int32)
    pad_len = (tiles * R - cnt).astype(jnp.int32)
    tile_rows = jnp.clip(pad_start[tile_expert] - j * R, 0, R).astype(jnp.int32)
    return d1, d2, tile_expert, n_used.astype(jnp.int32), pad_start, pad_len, tile_rows


def _lanes(parts):
    ref = next(p for p in parts if not isinstance(p, int))
    return jnp.concatenate([jnp.zeros(ref.shape[:-1] + (p,), ref.dtype) if isinstance(p, int) else p
                            for p in parts], axis=-1)


def _mla_head_lanes(nope, rope):
    half = MLA_ROPE // 2
    cut = 64 - half
    r1, r2 = (half, half) if isinstance(rope, int) else (rope[..., :half], rope[..., half:])
    n1, n2 = (cut, MLA_NOPE - cut) if isinstance(nope, int) else (nope[..., :cut], nope[..., cut:])
    return _lanes([r1, n1, r2, n2, LANES - MLA_QK])


def _dsw_pair_lanes(w):
    half = DSW_ROT // 2
    w = w.reshape(w.shape[0], DSW_HEADS // 2, 2, DSW_HD)
    a, b = w[:, :, 0, :], w[:, :, 1, :]
    blk = _lanes([a[..., :half], b[..., :half], a[..., DSW_ROT:], a[..., half:DSW_ROT], b[..., half:DSW_ROT],
                  b[..., DSW_ROT:]])
    return blk.reshape(w.shape[0], DSW_WIDTH)


def _rope_lane_tables(half, copies):
    inv_freq = jnp.power(jnp.float32(ROPE_THETA), -jnp.arange(half, dtype=F32) * (2.0 / (2 * half)))
    on = copies * half
    group = [inv_freq] * copies + [64 - on]
    sign = np.zeros(LANES, np.float32)
    sign[:on], sign[64:64 + on] = -1.0, 1.0
    return _lanes(group + group).reshape(1, LANES), jnp.asarray(sign).reshape(1, LANES)


def kernel(x, positions, norm_attn, norm_ffn, mla_w_down, mla_q_norm, mla_w_uq, mla_kv_norm, mla_w_ukv,
           mla_w_o, dsw_kv_norm, dsw_w_kv, dsw_w_q, dsw_w_o, ffn_w_gate, ffn_w_up, ffn_w_down, moe_router,
           moe_w_gate, moe_w_up, moe_w_down, final_norm):
    B, S, D = x.shape
    T = B * S
    pos = positions.reshape(B, S, 1)

    H = MLA_HEADS
    n_lat = MLA_Q_LORA + MLA_KV_LORA
    w_down = mla_w_down.reshape(D, n_lat + MLA_ROPE)
    wdn = _lanes([w_down[:, :n_lat], _mla_head_lanes(MLA_NOPE, w_down[:, n_lat:])]).astype(BF16)
    w_uq = mla_w_uq.reshape(MLA_Q_LORA, H, MLA_QK)
    wuq = _mla_head_lanes(w_uq[..., :MLA_NOPE], w_uq[..., MLA_NOPE:]).reshape(MLA_Q_LORA, H * LANES).astype(BF16)
    w_ukv = mla_w_ukv.reshape(MLA_KV_LORA, H, MLA_NOPE + MLA_V)
    wk = _mla_head_lanes(w_ukv[..., :MLA_NOPE], MLA_ROPE).reshape(MLA_KV_LORA, H * LANES).astype(BF16)
    wv = w_ukv[..., MLA_NOPE:].reshape(MLA_KV_LORA, H * MLA_V).astype(BF16)
    mla_freq, mla_sign = _rope_lane_tables(MLA_ROPE // 2, 1)

    q, k, v = _mla_proj(x, pos, norm_attn[0:1], wdn, mla_q_norm[0:1], mla_kv_norm[0:1], wuq, wk, wv,
                        mla_freq, mla_sign, tm=MLA_TILE)
    o, (moe_wg, moe_wu, moe_wd) = _mla_attn(q, k, v, (moe_w_gate.reshape(N_EXPERTS, D, EXPERT_DIM),
                                                      moe_w_up.reshape(N_EXPERTS, D, EXPERT_DIM),
                                                      moe_w_down.reshape(N_EXPERTS, EXPERT_DIM, D)))
    h = _attn_ffn(x.reshape(T, D), o.reshape(T, D), mla_w_o[0].astype(BF16), norm_ffn[0:1],
                  ffn_w_gate[0].astype(BF16), ffn_w_up[0].astype(BF16), ffn_w_down[0].astype(BF16))

    w_q = dsw_w_q.reshape(D, N_BR * DSW_WIDTH)
    branch = lambda w, g: w[:, g * DSW_WIDTH:(g + 1) * DSW_WIDTH]
    wq_b = jnp.stack([_dsw_pair_lanes(branch(w_q, g)) for g in range(N_BR)]).astype(BF16)
    wk_b = jnp.stack([_dsw_pair_lanes(branch(dsw_w_kv, g)) for g in range(N_BR)]).astype(BF16)
    wv_b = jnp.stack([branch(dsw_w_kv, N_BR + g) for g in range(N_BR)]).astype(BF16)
    dsw_freq, dsw_sign = _rope_lane_tables(DSW_ROT // 2, 2)

    qkv = _dsw_proj(h.reshape(B, S, D), pos, norm_attn[1:2], dsw_kv_norm.reshape(1, D), wq_b, wk_b, wv_b,
                    dsw_freq, dsw_sign)
    o = _dsw_attn(qkv)
    wr = jnp.pad(moe_router[0], ((0, 0), (0, LANES - N_EXPERTS)))
    wr_hi = wr.astype(BF16)
    wr = jnp.concatenate([wr_hi, (wr - wr_hi.astype(F32)).astype(BF16)], axis=1)
    h3, hn, meta, meta_t, counts = _dsw_router(h, o.reshape(T, D), dsw_w_o[0].astype(BF16), norm_ffn[1:2], wr)

    n_tiles = (2 * T) // MOE_ROWS + N_EXPERTS
    d1, d2, tile_expert, n_used, pad_start, pad_len, tile_rows = _moe_plan(meta_t, counts, n_tiles)
    xs = _moe_dispatch(pad_start, pad_len, n_used, d1, d2, hn, n_tiles * MOE_ROWS)
    ys = _moe_group(tile_expert, n_used, tile_rows, xs, moe_wg, moe_wu, moe_wd)
    out = _moe_combine(d1, d2, meta, h3, final_norm.reshape(1, D), ys)
    return out.reshape(B, S, D)
```

```python
import functools
import math

import numpy as np
import jax
import jax.numpy as jnp
from jax import lax
from jax.experimental import pallas as pl
from jax.experimental.pallas import tpu as pltpu

F32 = jnp.float32
BF16 = jnp.bfloat16

D_MODEL = 1024
RMS_EPS = 1e-6
ROPE_THETA = 500000.0
NEG_INF = -1e30
LANES = 128

MLA_HEADS = 16
MLA_NOPE = 64
MLA_ROPE = 32
MLA_V = 64
MLA_QK = MLA_NOPE + MLA_ROPE
MLA_Q_LORA = 384
MLA_KV_LORA = 256

DSW_BRANCHES = ((128, 1), (512, 4), (2048, 16))
N_BR = 3
DSW_HEADS = 16
DSW_HD = 64
DSW_ROT = 16
DSW_WIDTH = DSW_HEADS * DSW_HD
DSW_BLOCK = 128

FFN_DIM = 2816
N_EXPERTS = 8
EXPERT_DIM = 3584

VMEM_LIMIT = 56 * 1024 * 1024


def _cparams(sem):
    return pltpu.CompilerParams(dimension_semantics=sem, vmem_limit_bytes=VMEM_LIMIT)


def _resident(shape):
    nd = len(shape)
    return pl.BlockSpec(shape, lambda *_: (0,) * nd, pipeline_mode=pl.Buffered(1))


def _rms(x, g):
    return x * lax.rsqrt(jnp.mean(x * x, axis=-1, keepdims=True) + RMS_EPS) * g


def _dot(a, b):
    return jnp.dot(a, b, preferred_element_type=F32)


def _dot_nt(a, b):
    return lax.dot_general(a, b, (((1,), (1,)), ((), ())), preferred_element_type=F32)


def _rope(t, cos, sin_signed):
    return t * cos + pltpu.roll(t, 64, 1) * sin_signed


def _mla_proj_kernel(x_ref, pos_ref, g_ref, wdn_ref, qn_ref, kvn_ref, wuq_ref, wk_ref, wv_ref,
                     freq_ref, sign_ref, q_ref, k_ref, v_ref):
    xn = _rms(x_ref[...], g_ref[...]).astype(BF16)
    down = _dot(xn, wdn_ref[...])
    cq = _rms(down[:, :MLA_Q_LORA], qn_ref[...]).astype(BF16)
    ckv = _rms(down[:, MLA_Q_LORA:MLA_Q_LORA + MLA_KV_LORA], kvn_ref[...]).astype(BF16)
    kr = down[:, MLA_Q_LORA + MLA_KV_LORA:]
    ang = pos_ref[...].astype(F32) * freq_ref[...]
    cos = jnp.cos(ang)
    sin = jnp.sin(ang) * sign_ref[...]
    kr = _rope(kr, cos, sin)
    scale = LOG2E / math.sqrt(MLA_QK)
    cos_q = cos * scale
    sin_q = sin * scale
    q = _dot(cq, wuq_ref[...])
    for h in range(MLA_HEADS):
        q_ref[h] = _rope(q[:, h * LANES:(h + 1) * LANES], cos_q, sin_q).T.astype(BF16)
    kp = _dot(ckv, wk_ref[...])
    for h in range(MLA_HEADS):
        k_ref[h] = (kp[:, h * LANES:(h + 1) * LANES] + kr).astype(BF16)
    v = _dot(ckv, wv_ref[...])
    for j in range(MLA_HEADS // 2):
        v_ref[j] = v[:, j * LANES:(j + 1) * LANES].T.astype(BF16)


def _mla_proj(x, pos, g, wdn, qn, kvn, wuq, wk, wv, freq, sign, tm):
    B, S, _ = x.shape
    H = MLA_HEADS
    return pl.pallas_call(
        _mla_proj_kernel,
        grid=(B, S // tm),
        in_specs=[
            pl.BlockSpec((None, tm, D_MODEL), lambda b, i: (b, i, 0)),
            pl.BlockSpec((None, tm, 1), lambda b, i: (b, i, 0)),
            _resident(g.shape), _resident(wdn.shape), _resident(qn.shape), _resident(kvn.shape),
            _resident(wuq.shape), _resident(wk.shape), _resident(wv.shape),
            _resident(freq.shape), _resident(sign.shape),
        ],
        out_specs=[
            pl.BlockSpec((None, H, None, LANES, tm), lambda b, i: (b, 0, i, 0, 0)),
            pl.BlockSpec((None, H, tm, LANES), lambda b, i: (b, 0, i, 0)),
            pl.BlockSpec((None, H // 2, None, LANES, tm), lambda b, i: (b, 0, i, 0, 0)),
        ],
        out_shape=[
            jax.ShapeDtypeStruct((B, H, S // tm, LANES, tm), BF16),
            jax.ShapeDtypeStruct((B, H, S, LANES), BF16),
            jax.ShapeDtypeStruct((B, H // 2, S // tm, LANES, tm), BF16),
        ],
        compiler_params=_cparams(("parallel", "parallel")),
        name="mla_proj",
    )(x, pos, g, wdn, qn, kvn, wuq, wk, wv, freq, sign)


MLA_TILE = 512
MLA_ONES = 16
LOG2E = 1.4426950408889634

def _mla_attn_kernel(qT_ref, k_ref, vT_ref, *rest, t, n_cast):
    cast_in, o_ref, cast_out = rest[:n_cast], rest[n_cast], rest[n_cast + 1:2 * n_cast + 1]
    m_scr, acc_scr, s_scr = rest[2 * n_cast + 1:]
    for src, dst in zip(cast_in, cast_out):
        dst[...] = src[...].astype(BF16)
    n = qT_ref.shape[1]
    ones = jnp.ones((MLA_ONES, t), BF16)
    key = lax.broadcasted_iota(jnp.int32, (t, t), 0)
    qry = lax.broadcasted_iota(jnp.int32, (t, t), 1)

    def scores(i, c, hh):
        return _dot(k_ref[hh, c * t:(c + 1) * t, :], qT_ref[hh, i])

    def update(s, c, hh):
        m_prev = m_scr[hh]
        m_new = jnp.maximum(m_prev, jnp.max(s, axis=0, keepdims=True))
        alpha = jnp.exp2(m_prev - m_new)
        p = jnp.exp2(s - m_new).astype(BF16)
        lhs = jnp.concatenate([vT_ref[c, hh * MLA_V:(hh + 1) * MLA_V, :], ones], axis=0)
        acc_scr[hh] = alpha * acc_scr[hh] + _dot(lhs, p)
        m_scr[hh] = m_new

    pairs = [(i, c) for i in range(n) for c in range(i + 1)]
    for hh in range(2):
        s_scr[hh] = scores(0, 0, hh)
    for idx, (i, c) in enumerate(pairs):
        if c == 0:
            m_scr[...] = jnp.full(m_scr.shape, NEG_INF, F32)
            acc_scr[...] = jnp.zeros(acc_scr.shape, F32)
        for hh in range(2):
            s = s_scr[hh]
            if idx + 1 < len(pairs):
                s_scr[hh] = scores(*pairs[idx + 1], hh)
            if c == i:
                s = jnp.where(key <= qry, s, NEG_INF)
            update(s, c, hh)
        if c == i:
            o = [acc_scr[hh, :MLA_V, :] * (1.0 / acc_scr[hh, MLA_V:MLA_V + 1, :]) for hh in range(2)]
            o_ref[i * t:(i + 1) * t, :] = jnp.concatenate(o, axis=0).T.astype(BF16)


def _mla_attn(qT, k, vT, cast=()):
    B, H, n, _, t = qT.shape
    S = n * t
    P = H // 2
    steps = B * P
    cast_specs, cast_shapes = [], []
    for w in cast:
        E, rows, cols = w.shape
        cr = E * rows // steps
        assert E * rows == cr * steps and rows % cr == 0 and cr % 16 == 0, (w.shape, steps)
        per = rows // cr
        cast_specs.append(pl.BlockSpec(
            (None, cr, cols), lambda b, j, per=per: ((b * P + j) // per, (b * P + j) % per, 0)))
        cast_shapes.append(jax.ShapeDtypeStruct(w.shape, BF16))
    outs = pl.pallas_call(
        functools.partial(_mla_attn_kernel, t=t, n_cast=len(cast)),
        grid=(B, P),
        in_specs=[
            pl.BlockSpec((None, 2, n, LANES, t), lambda b, j: (b, j, 0, 0, 0)),
            pl.BlockSpec((None, 2, S, LANES), lambda b, j: (b, j, 0, 0)),
            pl.BlockSpec((None, None, n, LANES, t), lambda b, j: (b, j, 0, 0, 0)),
        ] + cast_specs,
        out_specs=[pl.BlockSpec((None, S, LANES), lambda b, j: (b, 0, j))] + cast_specs,
        out_shape=[jax.ShapeDtypeStruct((B, S, P * LANES), BF16)] + cast_shapes,
        scratch_shapes=[pltpu.VMEM((2, 1, t), F32), pltpu.VMEM((2, MLA_V + MLA_ONES, t), F32),
                        pltpu.VMEM((2, t, t), F32)],
        compiler_params=_cparams(("parallel", "parallel")),
        name="mla_attn",
    )(qT, k, vT, *cast)
    return outs[0], outs[1:]


def _swiglu_acc(hn, wg_ref, wu_ref, wd_ref, chunk, scale=None):
    acc = None
    n = wg_ref.shape[-1]
    for c0 in range(0, n, chunk):
        g = _dot(hn, wg_ref[:, c0:c0 + chunk])
        u = _dot(hn, wu_ref[:, c0:c0 + chunk])
        a = g * jax.nn.sigmoid(g) * u
        if scale is not None:
            a = a * scale
        part = _dot(a.astype(BF16), wd_ref[c0:c0 + chunk, :])
        acc = part if acc is None else acc + part
    return acc


def _attn_ffn_kernel(x_ref, o_ref, wo_ref, g_ref, wg_ref, wu_ref, wd_ref, h_ref):
    h1 = x_ref[...] + _dot(o_ref[...], wo_ref[...])
    hn = _rms(h1, g_ref[...]).astype(BF16)
    h_ref[...] = h1 + _swiglu_acc(hn, wg_ref, wu_ref, wd_ref, FFN_DIM // 2)


def _attn_ffn(x, o, wo, g, wg, wu, wd, tm=512):
    T = x.shape[0]
    return pl.pallas_call(
        _attn_ffn_kernel,
        grid=(T // tm,),
        in_specs=[
            pl.BlockSpec((tm, D_MODEL), lambda i: (i, 0)),
            pl.BlockSpec((tm, D_MODEL), lambda i: (i, 0)),
            _resident(wo.shape), _resident(g.shape), _resident(wg.shape), _resident(wu.shape),
            _resident(wd.shape),
        ],
        out_specs=pl.BlockSpec((tm, D_MODEL), lambda i: (i, 0)),
        out_shape=jax.ShapeDtypeStruct((T, D_MODEL), F32),
        compiler_params=_cparams(("parallel",)),
        name="attn_ffn",
    )(x, o, wo, g, wg, wu, wd)


def _dsw_proj_kernel(h_ref, pos_ref, gq_ref, gkv_ref, wq_ref, wk_ref, wv_ref, freq_ref, sign_ref,
                     q1, k1, v1, q2, k2, v2, q3, k3, v3,
                     hr_scr, cos_scr, sin_scr, pq_scr, pkv_scr, pcos_scr, psin_scr, *, tm):
    h = h_ref[...]
    r = lax.rsqrt(jnp.mean(h * h, axis=-1, keepdims=True) + RMS_EPS)
    hr = h * r
    nblk = D_MODEL // LANES
    for cb in range(nblk):
        hr_scr[cb] = hr[:, cb * LANES:(cb + 1) * LANES]
    ang = pos_ref[...].astype(F32) * freq_ref[...]
    cos_scr[...] = jnp.cos(ang)
    sin_scr[...] = jnp.sin(ang) * sign_ref[...]
    scale = LOG2E / math.sqrt(DSW_HD)
    outs = ((q1, k1, v1), (q2, k2, v2), (q3, k3, v3))
    for g, (_, d) in enumerate(DSW_BRANCHES):
        n = tm // d
        q_out, k_out, v_out = outs[g]
        if d == 1:
            xq = (hr * gq_ref[...]).astype(BF16)
            xkv = (hr * gkv_ref[...]).astype(BF16)
            cos = cos_scr[...]
            sin = sin_scr[...]
        else:
            for rr in range(d):
                rows = slice(rr * n, (rr + 1) * n)
                for cb in range(nblk):
                    cols = slice(cb * LANES, (cb + 1) * LANES)
                    part = hr_scr[cb, pl.ds(rr, n, stride=d), :]
                    pq_scr[rows, cols] = (part * gq_ref[:, cols]).astype(BF16)
                    pkv_scr[rows, cols] = (part * gkv_ref[:, cols]).astype(BF16)
                pcos_scr[rows, :] = cos_scr[pl.ds(rr, n, stride=d), :]
                psin_scr[rows, :] = sin_scr[pl.ds(rr, n, stride=d), :]
            xq = pq_scr[...]
            xkv = pkv_scr[...]
            cos = pcos_scr[...]
            sin = psin_scr[...]
        q = _dot(xq, wq_ref[g])
        k = _dot(xkv, wk_ref[g])
        v = _dot(xkv, wv_ref[g])
        cos_q = cos * scale
        sin_q = sin * scale
        for j in range(DSW_HEADS // 2):
            sl = slice(j * LANES, (j + 1) * LANES)
            qj = _rope(q[:, sl], cos_q, sin_q).astype(BF16)
            kj = _rope(k[:, sl], cos, sin).astype(BF16)
            vj = v[:, sl].astype(BF16)
            for rr in range(d):
                q_out[j, rr] = qj[rr * n:(rr + 1) * n]
                k_out[j, rr] = kj[rr * n:(rr + 1) * n]
                v_out[j, rr] = vj[rr * n:(rr + 1) * n]


def _dsw_proj(h, pos, gq, gkv, wq, wk, wv, freq, sign, tm=512):
    B, S, _ = h.shape
    P = DSW_HEADS // 2
    out_specs, out_shape = [], []
    for _, d in DSW_BRANCHES:
        for _ in range(3):
            out_specs.append(pl.BlockSpec((None, P, d, tm // d, LANES), lambda b, i: (b, 0, 0, i, 0)))
            out_shape.append(jax.ShapeDtypeStruct((B, P, d, S // d, LANES), BF16))
    return pl.pallas_call(
        functools.partial(_dsw_proj_kernel, tm=tm),
        grid=(B, S // tm),
        in_specs=[
            pl.BlockSpec((None, tm, D_MODEL), lambda b, i: (b, i, 0)),
            pl.BlockSpec((None, tm, 1), lambda b, i: (b, i, 0)),
            _resident(gq.shape), _resident(gkv.shape), _resident(wq.shape), _resident(wk.shape),
            _resident(wv.shape), _resident(freq.shape), _resident(sign.shape),
        ],
        out_specs=out_specs,
        out_shape=out_shape,
        scratch_shapes=[
            pltpu.VMEM((D_MODEL // LANES, tm, LANES), F32),
            pltpu.VMEM((tm, LANES), F32), pltpu.VMEM((tm, LANES), F32),
            pltpu.VMEM((tm, D_MODEL), BF16), pltpu.VMEM((tm, D_MODEL), BF16),
            pltpu.VMEM((tm, LANES), F32), pltpu.VMEM((tm, LANES), F32),
        ],
        compiler_params=_cparams(("parallel", "parallel")),
        name="dsw_proj",
    )(h, pos, gq, gkv, wq, wk, wv, freq, sign)


def _dsw_attn_kernel(q1, k1, v1, q2, k2, v2, q3, k3, v3, o_ref,
                     o1_scr, l1_scr, o2_scr, l2_scr, o3_scr, l3_scr, s_scr, *, S):
    blk = DSW_BLOCK
    lane = lax.broadcasted_iota(jnp.int32, (blk, LANES), 1)
    mask_a = (lane < 8) | ((lane >= 16) & (lane < 72))
    first_half = lane < DSW_HD
    row1 = lax.broadcasted_iota(jnp.int32, (2 * blk, blk), 0)
    col1 = lax.broadcasted_iota(jnp.int32, (2 * blk, blk), 1)
    row1 = jnp.where(row1 >= blk, row1 - blk, row1)
    valid_first = col1 <= row1
    row2 = lax.broadcasted_iota(jnp.int32, (2 * blk, 2 * blk), 0)
    col2 = lax.broadcasted_iota(jnp.int32, (2 * blk, 2 * blk), 1)
    row2 = jnp.where(row2 >= blk, row2 - blk, row2)
    valid_next = ((col2 < blk) & (col2 >= row2)) | ((col2 >= blk) & (col2 - blk <= row2))

    def scores(q_ref, k_ref, off, nkeys):
        qf = q_ref[off:off + blk, :].astype(F32)
        q2 = jnp.concatenate([jnp.where(mask_a, qf, 0.0), jnp.where(mask_a, 0.0, qf)], axis=0).astype(BF16)
        return _dot_nt(q2, k_ref[off + blk - nkeys:off + blk, :])

    def finish(s, v_ref, off, valid, o_tok, l_tok, tok0, d):
        nkeys = valid.shape[1]
        s = jnp.where(valid, s, NEG_INF)
        m = jnp.max(s, axis=-1, keepdims=True)
        p = jnp.exp2(s - m)
        l = jnp.sum(p, axis=-1, keepdims=True)
        o2 = _dot(p.astype(BF16), v_ref[off + blk - nkeys:off + blk, :]) / l
        lse = m + jnp.log2(l)
        rows = pl.ds(tok0, blk, stride=d) if d > 1 else pl.ds(tok0, blk)
        o_tok[rows, :] = jnp.where(first_half, o2[:blk], o2[blk:])
        l_tok[rows, :] = jnp.where(first_half, lse[:blk], lse[blk:])

    for (_, d), (q_ref, k_ref, v_ref, o_tok, l_tok) in zip(
            DSW_BRANCHES, ((q1, k1, v1, o1_scr, l1_scr), (q2, k2, v2, o2_scr, l2_scr), (q3, k3, v3, o3_scr, l3_scr))):
        L = S // d
        blocks = [(rr * L + ii * blk, ii * blk * d + rr, valid_first if ii == 0 else valid_next)
                  for rr in range(d) for ii in range(L // blk)]
        off, _, valid = blocks[0]
        s_scr[:, :valid.shape[1]] = scores(q_ref, k_ref, off, valid.shape[1])
        for n, (off, tok0, valid) in enumerate(blocks):
            s = s_scr[:, :valid.shape[1]]
            if n + 1 < len(blocks):
                noff, _, nvalid = blocks[n + 1]
                s_scr[:, :nvalid.shape[1]] = scores(q_ref, k_ref, noff, nvalid.shape[1])
            finish(s, v_ref, off, valid, o_tok, l_tok, tok0, d)

    ch = 256
    for c0 in range(0, S, ch):
        sl = slice(c0, c0 + ch)
        la, lb, lc = l1_scr[sl, :], l2_scr[sl, :], l3_scr[sl, :]
        m = jnp.maximum(jnp.maximum(la, lb), lc)
        wa, wb, wc = jnp.exp2(la - m), jnp.exp2(lb - m), jnp.exp2(lc - m)
        num = wa * o1_scr[sl, :] + wb * o2_scr[sl, :] + wc * o3_scr[sl, :]
        o_ref[sl, :] = (num / (wa + wb + wc)).astype(BF16)


def _dsw_attn(qkv):
    B, P = qkv[0].shape[0], qkv[0].shape[1]
    S = qkv[0].shape[2] * qkv[0].shape[3]
    flat = [a.reshape(B, P, S, LANES) for a in qkv]
    spec = pl.BlockSpec((None, None, S, LANES), lambda b, j: (b, j, 0, 0))
    return pl.pallas_call(
        functools.partial(_dsw_attn_kernel, S=S),
        grid=(B, P),
        in_specs=[spec] * 9,
        out_specs=pl.BlockSpec((None, S, LANES), lambda b, j: (b, 0, j)),
        out_shape=jax.ShapeDtypeStruct((B, S, P * LANES), BF16),
        scratch_shapes=[pltpu.VMEM((S, LANES), F32)] * 6 + [pltpu.VMEM((2 * DSW_BLOCK, 2 * DSW_BLOCK), F32)],
        compiler_params=_cparams(("parallel", "parallel")),
        name="dsw_attn",
    )(*flat)


META_E1, META_E2, META_R1, META_R2, META_G1, META_G2 = range(6)


def _dsw_router_kernel(h_ref, o_ref, wo_ref, g_ref, wr_ref, h3_ref, hn_ref, meta_ref, meta_t_ref, cnt_ref,
                       base_scr):
    @pl.when(pl.program_id(0) == 0)
    def _():
        base_scr[...] = jnp.zeros_like(base_scr)

    h3 = h_ref[...] + _dot(o_ref[...], wo_ref[...])
    h3_ref[...] = h3
    hn = _rms(h3, g_ref[...])
    hn_ref[...] = hn
    hi = hn.astype(BF16)
    lo = (hn - hi.astype(F32)).astype(BF16)
    part = _dot(hi, wr_ref[...])
    logits = part[:, :LANES] + part[:, LANES:] + _dot(lo, wr_ref[:, :LANES])
    tm = logits.shape[0]
    lane = lax.broadcasted_iota(jnp.int32, logits.shape, 1).astype(F32)
    logits = jnp.where(lane < N_EXPERTS, logits, -jnp.inf)
    v1 = jnp.max(logits, axis=-1, keepdims=True)
    i1 = jnp.min(jnp.where(logits == v1, lane, float(LANES)), axis=-1, keepdims=True)
    rest = jnp.where(lane == i1, -jnp.inf, logits)
    v2 = jnp.max(rest, axis=-1, keepdims=True)
    i2 = jnp.min(jnp.where(rest == v2, lane, float(LANES)), axis=-1, keepdims=True)
    e2 = jnp.exp(v2 - v1)
    den = 1.0 + e2
    onehot = jnp.where((lane == i1) | (lane == i2), 1.0, 0.0)
    row = lax.broadcasted_iota(jnp.int32, (tm, tm), 0)
    col = lax.broadcasted_iota(jnp.int32, (tm, tm), 1)
    earlier = jnp.where(col < row, 1.0, 0.0).astype(BF16)
    before = _dot(earlier, onehot.astype(BF16)) + base_scr[0:1, :]
    r1 = jnp.sum(jnp.where(lane == i1, before, 0.0), axis=-1, keepdims=True)
    r2 = jnp.sum(jnp.where(lane == i2, before, 0.0), axis=-1, keepdims=True)
    total = base_scr[0:1, :] + jnp.sum(onehot, axis=0, keepdims=True)
    base_scr[...] = jnp.broadcast_to(total, base_scr.shape)
    cnt_ref[...] = jnp.broadcast_to(total, cnt_ref.shape)
    meta = jnp.zeros_like(logits)
    for k, val in ((META_E1, i1), (META_E2, i2), (META_R1, r1), (META_R2, r2),
                   (META_G1, 1.0 / den), (META_G2, e2 / den)):
        meta = jnp.where(lane == float(k), val, meta)
    meta_ref[...] = meta
    meta_t_ref[...] = meta.T[:8, :]


def _dsw_router(h, o, wo, g, wr, tm=512):
    T = h.shape[0]
    return pl.pallas_call(
        _dsw_router_kernel,
        grid=(T // tm,),
        in_specs=[
            pl.BlockSpec((tm, D_MODEL), lambda i: (i, 0)),
            pl.BlockSpec((tm, D_MODEL), lambda i: (i, 0)),
            _resident(wo.shape), _resident(g.shape), _resident(wr.shape),
        ],
        out_specs=[
            pl.BlockSpec((tm, D_MODEL), lambda i: (i, 0)),
            pl.BlockSpec((tm, D_MODEL), lambda i: (i, 0)),
            pl.BlockSpec((tm, LANES), lambda i: (i, 0)),
            pl.BlockSpec((8, tm), lambda i: (0, i)),
            pl.BlockSpec((8, LANES), lambda i: (0, 0)),
        ],
        out_shape=[
            jax.ShapeDtypeStruct((T, D_MODEL), F32),
            jax.ShapeDtypeStruct((T, D_MODEL), F32),
            jax.ShapeDtypeStruct((T, LANES), F32),
            jax.ShapeDtypeStruct((8, T), F32),
            jax.ShapeDtypeStruct((8, LANES), F32),
        ],
        scratch_shapes=[pltpu.VMEM((8, LANES), F32)],
        compiler_params=_cparams(("arbitrary",)),
        name="dsw_router",
    )(h, o, wo, g, wr)


MOE_ROWS = 512
MOE_SUB = 128


def _row_copy(src, src_row, dst, dst_row, sem):
    return pltpu.make_async_copy(src.at[pl.ds(src_row, 1)], dst.at[pl.ds(dst_row, 1)], sem)


def _moe_dispatch_kernel(pad_start_ref, pad_len_ref, nu_ref, d1_ref, d2_ref, hn_ref, xs_hbm, zero_scr, sem, zsem,
                         *, tm):
    @pl.when(pl.program_id(0) == 0)
    def _():
        zero_scr[...] = jnp.zeros_like(zero_scr)
        R = zero_scr.shape[0]

        def zero_tile(j, c):
            cp = pltpu.make_async_copy(zero_scr, xs_hbm.at[pl.ds(pl.multiple_of(j * R, R), R)], zsem)
            cp.start()
            cp.wait()
            return c

        lax.fori_loop(nu_ref[0], xs_hbm.shape[0] // R, zero_tile, 0)
        def pad_copies(act):
            for e in range(N_EXPERTS):
                n, start = pad_len_ref[e], pad_start_ref[e]
                head = n & 7

                def row(r, c, start=start):
                    act(_row_copy(zero_scr, 0, xs_hbm, start + r, zsem))
                    return c

                lax.fori_loop(0, head, row, 0)
                p = R // 2
                while p >= 8:
                    @pl.when((n & p) != 0)
                    def _(p=p, n=n, start=start, head=head):
                        at = pl.multiple_of(start + head + ((n - head) & (-2 * p)), 8)
                        act(pltpu.make_async_copy(zero_scr.at[pl.ds(0, p)], xs_hbm.at[pl.ds(at, p)], zsem))
                    p //= 2

        pad_copies(lambda cp: cp.start())
        pad_copies(lambda cp: cp.wait())

    def body(r, c):
        _row_copy(hn_ref, r, xs_hbm, d1_ref[0, r], sem).start(priority=0)
        _row_copy(hn_ref, r, xs_hbm, d2_ref[0, r], sem).start(priority=1)
        return c

    lax.fori_loop(0, tm, body, 0, unroll=32)
    for _ in range(2):
        pltpu.make_async_copy(hn_ref, xs_hbm.at[pl.ds(0, tm)], sem).wait()


def _moe_dispatch(pad_start, pad_len, n_used, d1, d2, hn, n_rows, tm=512):
    T = hn.shape[0]
    smem = lambda: pl.BlockSpec((None, 1, tm), lambda i, ps, pn, nu: (i, 0, 0), memory_space=pltpu.SMEM)
    return pl.pallas_call(
        functools.partial(_moe_dispatch_kernel, tm=tm),
        grid_spec=pltpu.PrefetchScalarGridSpec(
            num_scalar_prefetch=3,
            grid=(T // tm,),
            in_specs=[smem(), smem(), pl.BlockSpec((tm, D_MODEL), lambda i, ps, pn, nu: (i, 0))],
            out_specs=pl.BlockSpec(memory_space=pl.ANY),
            scratch_shapes=[pltpu.VMEM((MOE_ROWS, D_MODEL), F32), pltpu.SemaphoreType.DMA,
                            pltpu.SemaphoreType.DMA],
        ),
        out_shape=jax.ShapeDtypeStruct((n_rows, D_MODEL), F32),
        compiler_params=_cparams(("arbitrary",)),
        name="moe_dispatch",
    )(pad_start, pad_len, n_used, d1.reshape(T // tm, 1, tm), d2.reshape(T // tm, 1, tm), hn)


def _moe_group_kernel(te_ref, nu_ref, nr_ref, xs_ref, wg_ref, wu_ref, wd_ref, ys_ref, xb_scr, acc_scr):
    j = pl.program_id(0)
    f = pl.program_id(1)
    used = j < nu_ref[0]

    @pl.when(used & (f == 0))
    def _():
        xb_scr[...] = xs_ref[...].astype(BF16)
        acc_scr[...] = jnp.zeros_like(acc_scr)

    need = (nr_ref[j] + (MOE_SUB - 1)) // MOE_SUB
    for k in range(1, MOE_ROWS // MOE_SUB + 1):
        @pl.when(used & (need == k))
        def _(m=k * MOE_SUB):
            acc_scr[:m, :] += _swiglu_acc(xb_scr[:m, :], wg_ref, wu_ref, wd_ref, wg_ref.shape[-1] // 2)

    @pl.when(f == pl.num_programs(1) - 1)
    def _():
        ys_ref[...] = jnp.where(used, acc_scr[...], 0.0)


def _moe_group(tile_expert, n_used, tile_rows, xs, wg, wu, wd, tf=1792):
    n_rows = xs.shape[0]
    R = MOE_ROWS
    E, _, Fe = wg.shape
    nf = Fe // tf

    def w_col(j, f, te, nu):
        return jnp.where(j < nu[0], f, nf - 1)

    return pl.pallas_call(
        _moe_group_kernel,
        grid_spec=pltpu.PrefetchScalarGridSpec(
            num_scalar_prefetch=3,
            grid=(n_rows // R, nf),
            in_specs=[
                pl.BlockSpec((R, D_MODEL),
                             lambda j, f, te, nu, nr: (jnp.maximum(jnp.minimum(j, nu[0] - 1), 0), 0)),
                pl.BlockSpec((None, D_MODEL, tf), lambda j, f, te, nu, nr: (te[j], 0, w_col(j, f, te, nu))),
                pl.BlockSpec((None, D_MODEL, tf), lambda j, f, te, nu, nr: (te[j], 0, w_col(j, f, te, nu))),
                pl.BlockSpec((None, tf, D_MODEL), lambda j, f, te, nu, nr: (te[j], w_col(j, f, te, nu), 0)),
            ],
            out_specs=pl.BlockSpec((R, D_MODEL), lambda j, f, te, nu, nr: (j, 0)),
            scratch_shapes=[pltpu.VMEM((R, D_MODEL), BF16), pltpu.VMEM((R, D_MODEL), F32)],
        ),
        out_shape=jax.ShapeDtypeStruct((n_rows, D_MODEL), F32),
        compiler_params=_cparams(("arbitrary", "arbitrary")),
        name="moe_group",
    )(tile_expert, n_used, tile_rows, xs, wg, wu, wd)


def _moe_combine_kernel(d1_ref, d2_ref, n1_ref, n2_ref, meta_ref, h3_ref, gf_ref, ys_hbm, out_ref,
                        y_scr, sems, *, tm):
    i = pl.program_id(0)
    slot = i % 2

    def gather(da_ref, db_ref, s):
        def body(r, c):
            _row_copy(ys_hbm, da_ref[0, r], y_scr.at[s, 0], r, sems.at[s]).start(priority=0)
            _row_copy(ys_hbm, db_ref[0, r], y_scr.at[s, 1], r, sems.at[s]).start(priority=1)
            return c

        lax.fori_loop(0, tm, body, 0, unroll=32)

    @pl.when(i == 0)
    def _():
        gather(d1_ref, d2_ref, 0)

    @pl.when(i + 1 < pl.num_programs(0))
    def _():
        gather(n1_ref, n2_ref, 1 - slot)

    for k in range(2):
        pltpu.make_async_copy(ys_hbm.at[pl.ds(0, tm)], y_scr.at[slot, k], sems.at[slot]).wait()
    meta = meta_ref[...]
    lane = lax.broadcasted_iota(jnp.int32, meta.shape, 1)
    g1 = jnp.sum(jnp.where(lane == META_G1, meta, 0.0), axis=-1, keepdims=True)
    g2 = jnp.sum(jnp.where(lane == META_G2, meta, 0.0), axis=-1, keepdims=True)
    out_ref[...] = _rms(h3_ref[...] + g1 * y_scr[slot, 0] + g2 * y_scr[slot, 1], gf_ref[...])


def _moe_combine(d1, d2, meta, h3, gf, ys, tm=512):
    T = h3.shape[0]
    n = T // tm
    smem = lambda nxt: pl.BlockSpec((None, 1, tm), lambda i: (jnp.minimum(i + nxt, n - 1), 0, 0),
                                    memory_space=pltpu.SMEM)
    d1 = d1.reshape(n, 1, tm)
    d2 = d2.reshape(n, 1, tm)
    return pl.pallas_call(
        functools.partial(_moe_combine_kernel, tm=tm),
        grid=(n,),
        in_specs=[
            smem(0), smem(0), smem(1), smem(1),
            pl.BlockSpec((tm, LANES), lambda i: (i, 0)),
            pl.BlockSpec((tm, D_MODEL), lambda i: (i, 0)),
            pl.BlockSpec((1, D_MODEL), lambda i: (0, 0)),
            pl.BlockSpec(memory_space=pl.ANY),
        ],
        out_specs=pl.BlockSpec((tm, D_MODEL), lambda i: (i, 0)),
        out_shape=jax.ShapeDtypeStruct((T, D_MODEL), F32),
        scratch_shapes=[pltpu.VMEM((2, 2, tm, D_MODEL), F32), pltpu.SemaphoreType.DMA((2,))],
        compiler_params=_cparams(("arbitrary",)),
        name="moe_combine",
    )(d1, d2, d1, d2, meta, h3, gf, ys)


def _moe_plan(meta_t, counts, n_tiles):
    R = MOE_ROWS
    cnt = counts[0, :N_EXPERTS].astype(jnp.int32)
    tiles = (cnt + R - 1) // R
    tile_end = jnp.cumsum(tiles)
    row_start = (tile_end - tiles) * R
    fields = meta_t.astype(jnp.int32)
    d1 = row_start[fields[META_E1]] + fields[META_R1]
    d2 = row_start[fields[META_E2]] + fields[META_R2]
    n_used = tile_end[-1:]
    j = jnp.minimum(jnp.arange(n_tiles, dtype=jnp.int32), n_used - 1)
    tile_expert = jnp.sum(j[:, None] >= tile_end[None, :], axis=1).astype(jnp.int32)
    pad_start = (row_start + cnt).astype(jnp.int32)
    pad_len = (tiles * R - cnt).astype(jnp.int32)
    tile_rows = jnp.clip(pad_start[tile_expert] - j * R, 0, R).astype(jnp.int32)
    return d1, d2, tile_expert, n_used.astype(jnp.int32), pad_start, pad_len, tile_rows


def _lanes(parts):
    ref = next(p for p in parts if not isinstance(p, int))
    return jnp.concatenate([jnp.zeros(ref.shape[:-1] + (p,), ref.dtype) if isinstance(p, int) else p
                            for p in parts], axis=-1)


def _mla_head_lanes(nope, rope):
    half = MLA_ROPE // 2
    cut = 64 - half
    r1, r2 = (half, half) if isinstance(rope, int) else (rope[..., :half], rope[..., half:])
    n1, n2 = (cut, MLA_NOPE - cut) if isinstance(nope, int) else (nope[..., :cut], nope[..., cut:])
    return _lanes([r1, n1, r2, n2, LANES - MLA_QK])


def _dsw_pair_lanes(w):
    half = DSW_ROT // 2
    w = w.reshape(w.shape[0], DSW_HEADS // 2, 2, DSW_HD)
    a, b = w[:, :, 0, :], w[:, :, 1, :]
    blk = _lanes([a[..., :half], b[..., :half], a[..., DSW_ROT:], a[..., half:DSW_ROT], b[..., half:DSW_ROT],
                  b[..., DSW_ROT:]])
    return blk.reshape(w.shape[0], DSW_WIDTH)


def _rope_lane_tables(half, copies):
    inv_freq = jnp.power(jnp.float32(ROPE_THETA), -jnp.arange(half, dtype=F32) * (2.0 / (2 * half)))
    on = copies * half
    group = [inv_freq] * copies + [64 - on]
    sign = np.zeros(LANES, np.float32)
    sign[:on], sign[64:64 + on] = -1.0, 1.0
    return _lanes(group + group).reshape(1, LANES), jnp.asarray(sign).reshape(1, LANES)


def kernel(x, positions, norm_attn, norm_ffn, mla_w_down, mla_q_norm, mla_w_uq, mla_kv_norm, mla_w_ukv,
           mla_w_o, dsw_kv_norm, dsw_w_kv, dsw_w_q, dsw_w_o, ffn_w_gate, ffn_w_up, ffn_w_down, moe_router,
           moe_w_gate, moe_w_up, moe_w_down, final_norm):
    B, S, D = x.shape
    T = B * S
    pos = positions.reshape(B, S, 1)

    H = MLA_HEADS
    n_lat = MLA_Q_LORA + MLA_KV_LORA
    w_down = mla_w_down.reshape(D, n_lat + MLA_ROPE)
    wdn = _lanes([w_down[:, :n_lat], _mla_head_lanes(MLA_NOPE, w_down[:, n_lat:])]).astype(BF16)
    w_uq = mla_w_uq.reshape(MLA_Q_LORA, H, MLA_QK)
    wuq = _mla_head_lanes(w_uq[..., :MLA_NOPE], w_uq[..., MLA_NOPE:]).reshape(MLA_Q_LORA, H * LANES).astype(BF16)
    w_ukv = mla_w_ukv.reshape(MLA_KV_LORA, H, MLA_NOPE + MLA_V)
    wk = _mla_head_lanes(w_ukv[..., :MLA_NOPE], MLA_ROPE).reshape(MLA_KV_LORA, H * LANES).astype(BF16)
    wv = w_ukv[..., MLA_NOPE:].reshape(MLA_KV_LORA, H * MLA_V).astype(BF16)
    mla_freq, mla_sign = _rope_lane_tables(MLA_ROPE // 2, 1)

    q, k, v = _mla_proj(x, pos, norm_attn[0:1], wdn, mla_q_norm[0:1], mla_kv_norm[0:1], wuq, wk, wv,
                        mla_freq, mla_sign, tm=MLA_TILE)
    o, (moe_wg, moe_wu, moe_wd) = _mla_attn(q, k, v, (moe_w_gate.reshape(N_EXPERTS, D, EXPERT_DIM),
                                                      moe_w_up.reshape(N_EXPERTS, D, EXPERT_DIM),
                                                      moe_w_down.reshape(N_EXPERTS, EXPERT_DIM, D)))
    h = _attn_ffn(x.reshape(T, D), o.reshape(T, D), mla_w_o[0].astype(BF16), norm_ffn[0:1],
                  ffn_w_gate[0].astype(BF16), ffn_w_up[0].astype(BF16), ffn_w_down[0].astype(BF16))

    w_q = dsw_w_q.reshape(D, N_BR * DSW_WIDTH)
    branch = lambda w, g: w[:, g * DSW_WIDTH:(g + 1) * DSW_WIDTH]
    wq_b = jnp.stack([_dsw_pair_lanes(branch(w_q, g)) for g in range(N_BR)]).astype(BF16)
    wk_b = jnp.stack([_dsw_pair_lanes(branch(dsw_w_kv, g)) for g in range(N_BR)]).astype(BF16)
    wv_b = jnp.stack([branch(dsw_w_kv, N_BR + g) for g in range(N_BR)]).astype(BF16)
    dsw_freq, dsw_sign = _rope_lane_tables(DSW_ROT // 2, 2)

    qkv = _dsw_proj(h.reshape(B, S, D), pos, norm_attn[1:2], dsw_kv_norm.reshape(1, D), wq_b, wk_b, wv_b,
                    dsw_freq, dsw_sign)
    o = _dsw_attn(qkv)
    wr = jnp.pad(moe_router[0], ((0, 0), (0, LANES - N_EXPERTS)))
    wr_hi = wr.astype(BF16)
    wr = jnp.concatenate([wr_hi, (wr - wr_hi.astype(F32)).astype(BF16)], axis=1)
    h3, hn, meta, meta_t, counts = _dsw_router(h, o.reshape(T, D), dsw_w_o[0].astype(BF16), norm_ffn[1:2], wr)

    n_tiles = (2 * T) // MOE_ROWS + N_EXPERTS
    d1, d2, tile_expert, n_used, pad_start, pad_len, tile_rows = _moe_plan(meta_t, counts, n_tiles)
    xs = _moe_dispatch(pad_start, pad_len, n_used, d1, d2, hn, n_tiles * MOE_ROWS)
    ys = _moe_group(tile_expert, n_used, tile_rows, xs, moe_wg, moe_wu, moe_wd)
    out = _moe_combine(d1, d2, meta, h3, final_norm.reshape(1, D), ys)
    return out.reshape(B, S, D)
```

```python
import functools
import math

import numpy as np
import jax
import jax.numpy as jnp
from jax import lax
from jax.experimental import pallas as pl
from jax.experimental.pallas import tpu as pltpu

F32 = jnp.float32
BF16 = jnp.bfloat16

D_MODEL = 1024
RMS_EPS = 1e-6
ROPE_THETA = 500000.0
NEG_INF = -1e30
LANES = 128

MLA_HEADS = 16
MLA_NOPE = 64
MLA_ROPE = 32
MLA_V = 64
MLA_QK = MLA_NOPE + MLA_ROPE
MLA_Q_LORA = 384
MLA_KV_LORA = 256

DSW_BRANCHES = ((128, 1), (512, 4), (2048, 16))
N_BR = 3
DSW_HEADS = 16
DSW_HD = 64
DSW_ROT = 16
DSW_WIDTH = DSW_HEADS * DSW_HD
DSW_BLOCK = 128

FFN_DIM = 2816
N_EXPERTS = 8
EXPERT_DIM = 3584

VMEM_LIMIT = 56 * 1024 * 1024


def _cparams(sem):
    return pltpu.CompilerParams(dimension_semantics=sem, vmem_limit_bytes=VMEM_LIMIT)


def _resident(shape):
    nd = len(shape)
    return pl.BlockSpec(shape, lambda *_: (0,) * nd, pipeline_mode=pl.Buffered(1))


def _rms(x, g):
    return x * lax.rsqrt(jnp.mean(x * x, axis=-1, keepdims=True) + RMS_EPS) * g


def _dot(a, b):
    return jnp.dot(a, b, preferred_element_type=F32)


def _dot_nt(a, b):
    return lax.dot_general(a, b, (((1,), (1,)), ((), ())), preferred_element_type=F32)


def _rope(t, cos, sin_signed):
    return t * cos + pltpu.roll(t, 64, 1) * sin_signed


def _mla_proj_kernel(x_ref, pos_ref, g_ref, wdn_ref, qn_ref, kvn_ref, wuq_ref, wk_ref, wv_ref,
                     freq_ref, sign_ref, q_ref, k_ref, v_ref):
    xn = _rms(x_ref[...], g_ref[...]).astype(BF16)
    down = _dot(xn, wdn_ref[...])
    cq = _rms(down[:, :MLA_Q_LORA], qn_ref[...]).astype(BF16)
    ckv = _rms(down[:, MLA_Q_LORA:MLA_Q_LORA + MLA_KV_LORA], kvn_ref[...]).astype(BF16)
    kr = down[:, MLA_Q_LORA + MLA_KV_LORA:]
    ang = pos_ref[...].astype(F32) * freq_ref[...]
    cos = jnp.cos(ang)
    sin = jnp.sin(ang) * sign_ref[...]
    kr = _rope(kr, cos, sin)
    scale = LOG2E / math.sqrt(MLA_QK)
    cos_q = cos * scale
    sin_q = sin * scale
    q = _dot(cq, wuq_ref[...])
    for h in range(MLA_HEADS):
        q_ref[h] = _rope(q[:, h * LANES:(h + 1) * LANES], cos_q, sin_q).T.astype(BF16)
    kp = _dot(ckv, wk_ref[...])
    for h in range(MLA_HEADS):
        k_ref[h] = (kp[:, h * LANES:(h + 1) * LANES] + kr).astype(BF16)
    v = _dot(ckv, wv_ref[...])
    for j in range(MLA_HEADS // 2):
        v_ref[j] = v[:, j * LANES:(j + 1) * LANES].T.astype(BF16)


def _mla_proj(x, pos, g, wdn, qn, kvn, wuq, wk, wv, freq, sign, tm):
    B, S, _ = x.shape
    H = MLA_HEADS
    return pl.pallas_call(
        _mla_proj_kernel,
        grid=(B, S // tm),
        in_specs=[
            pl.BlockSpec((None, tm, D_MODEL), lambda b, i: (b, i, 0)),
            pl.BlockSpec((None, tm, 1), lambda b, i: (b, i, 0)),
            _resident(g.shape), _resident(wdn.shape), _resident(qn.shape), _resident(kvn.shape),
            _resident(wuq.shape), _resident(wk.shape), _resident(wv.shape),
            _resident(freq.shape), _resident(sign.shape),
        ],
        out_specs=[
            pl.BlockSpec((None, H, None, LANES, tm), lambda b, i: (b, 0, i, 0, 0)),
            pl.BlockSpec((None, H, tm, LANES), lambda b, i: (b, 0, i, 0)),
            pl.BlockSpec((None, H // 2, None, LANES, tm), lambda b, i: (b, 0, i, 0, 0)),
        ],
        out_shape=[
            jax.ShapeDtypeStruct((B, H, S // tm, LANES, tm), BF16),
            jax.ShapeDtypeStruct((B, H, S, LANES), BF16),
            jax.ShapeDtypeStruct((B, H // 2, S // tm, LANES, tm), BF16),
        ],
        compiler_params=_cparams(("parallel", "parallel")),
        name="mla_proj",
    )(x, pos, g, wdn, qn, kvn, wuq, wk, wv, freq, sign)


MLA_TILE = 512
MLA_ONES = 16
LOG2E = 1.4426950408889634

def _mla_attn_kernel(qT_ref, k_ref, vT_ref, *rest, t, n_cast):
    cast_in, o_ref, cast_out = rest[:n_cast], rest[n_cast], rest[n_cast + 1:2 * n_cast + 1]
    m_scr, acc_scr, s_scr = rest[2 * n_cast + 1:]
    for src, dst in zip(cast_in, cast_out):
        dst[...] = src[...].astype(BF16)
    n = qT_ref.shape[1]
    ones = jnp.ones((MLA_ONES, t), BF16)
    key = lax.broadcasted_iota(jnp.int32, (t, t), 0)
    qry = lax.broadcasted_iota(jnp.int32, (t, t), 1)

    def scores(i, c, hh):
        return _dot(k_ref[hh, c * t:(c + 1) * t, :], qT_ref[hh, i])

    def update(s, c, hh):
        m_prev = m_scr[hh]
        m_new = jnp.maximum(m_prev, jnp.max(s, axis=0, keepdims=True))
        alpha = jnp.exp2(m_prev - m_new)
        p = jnp.exp2(s - m_new).astype(BF16)
        lhs = jnp.concatenate([vT_ref[c, hh * MLA_V:(hh + 1) * MLA_V, :], ones], axis=0)
        acc_scr[hh] = alpha * acc_scr[hh] + _dot(lhs, p)
        m_scr[hh] = m_new

    pairs = [(i, c) for i in range(n) for c in range(i + 1)]
    for hh in range(2):
        s_scr[hh] = scores(0, 0, hh)
    for idx, (i, c) in enumerate(pairs):
        if c == 0:
            m_scr[...] = jnp.full(m_scr.shape, NEG_INF, F32)
            acc_scr[...] = jnp.zeros(acc_scr.shape, F32)
        for hh in range(2):
            s = s_scr[hh]
            if idx + 1 < len(pairs):
                s_scr[hh] = scores(*pairs[idx + 1], hh)
            if c == i:
                s = jnp.where(key <= qry, s, NEG_INF)
            update(s, c, hh)
        if c == i:
            o = [acc_scr[hh, :MLA_V, :] * (1.0 / acc_scr[hh, MLA_V:MLA_V + 1, :]) for hh in range(2)]
            o_ref[i * t:(i + 1) * t, :] = jnp.concatenate(o, axis=0).T.astype(BF16)


def _mla_attn(qT, k, vT, cast=()):
    B, H, n, _, t = qT.shape
    S = n * t
    P = H // 2
    steps = B * P
    cast_specs, cast_shapes = [], []
    for w in cast:
        E, rows, cols = w.shape
        cr = E * rows // steps
        assert E * rows == cr * steps and rows % cr == 0 and cr % 16 == 0, (w.shape, steps)
        per = rows // cr
        cast_specs.append(pl.BlockSpec(
            (None, cr, cols), lambda b, j, per=per: ((b * P + j) // per, (b * P + j) % per, 0)))
        cast_shapes.append(jax.ShapeDtypeStruct(w.shape, BF16))
    outs = pl.pallas_call(
        functools.partial(_mla_attn_kernel, t=t, n_cast=len(cast)),
        grid=(B, P),
        in_specs=[
            pl.BlockSpec((None, 2, n, LANES, t), lambda b, j: (b, j, 0, 0, 0)),
            pl.BlockSpec((None, 2, S, LANES), lambda b, j: (b, j, 0, 0)),
            pl.BlockSpec((None, None, n, LANES, t), lambda b, j: (b, j, 0, 0, 0)),
        ] + cast_specs,
        out_specs=[pl.BlockSpec((None, S, LANES), lambda b, j: (b, 0, j))] + cast_specs,
        out_shape=[jax.ShapeDtypeStruct((B, S, P * LANES), BF16)] + cast_shapes,
        scratch_shapes=[pltpu.VMEM((2, 1, t), F32), pltpu.VMEM((2, MLA_V + MLA_ONES, t), F32),
                        pltpu.VMEM((2, t, t), F32)],
        compiler_params=_cparams(("parallel", "parallel")),
        name="mla_attn",
    )(qT, k, vT, *cast)
    return outs[0], outs[1:]


def _swiglu_acc(hn, wg_ref, wu_ref, wd_ref, chunk, scale=None):
    acc = None
    n = wg_ref.shape[-1]
    for c0 in range(0, n, chunk):
        g = _dot(hn, wg_ref[:, c0:c0 + chunk])
        u = _dot(hn, wu_ref[:, c0:c0 + chunk])
        a = g * jax.nn.sigmoid(g) * u
        if scale is not None:
            a = a * scale
        part = _dot(a.astype(BF16), wd_ref[c0:c0 + chunk, :])
        acc = part if acc is None else acc + part
    return acc


def _attn_ffn_kernel(x_ref, o_ref, wo_ref, g_ref, wg_ref, wu_ref, wd_ref, h_ref):
    h1 = x_ref[...] + _dot(o_ref[...], wo_ref[...])
    hn = _rms(h1, g_ref[...]).astype(BF16)
    h_ref[...] = h1 + _swiglu_acc(hn, wg_ref, wu_ref, wd_ref, FFN_DIM // 2)


def _attn_ffn(x, o, wo, g, wg, wu, wd, tm=512):
    T = x.shape[0]
    return pl.pallas_call(
        _attn_ffn_kernel,
        grid=(T // tm,),
        in_specs=[
            pl.BlockSpec((tm, D_MODEL), lambda i: (i, 0)),
            pl.BlockSpec((tm, D_MODEL), lambda i: (i, 0)),
            _resident(wo.shape), _resident(g.shape), _resident(wg.shape), _resident(wu.shape),
            _resident(wd.shape),
        ],
        out_specs=pl.BlockSpec((tm, D_MODEL), lambda i: (i, 0)),
        out_shape=jax.ShapeDtypeStruct((T, D_MODEL), F32),
        compiler_params=_cparams(("parallel",)),
        name="attn_ffn",
    )(x, o, wo, g, wg, wu, wd)


def _dsw_proj_kernel(h_ref, pos_ref, gq_ref, gkv_ref, wq_ref, wk_ref, wv_ref, freq_ref, sign_ref,
                     q1, k1, v1, q2, k2, v2, q3, k3, v3,
                     hr_scr, cos_scr, sin_scr, pq_scr, pkv_scr, pcos_scr, psin_scr, *, tm):
    h = h_ref[...]
    r = lax.rsqrt(jnp.mean(h * h, axis=-1, keepdims=True) + RMS_EPS)
    hr = h * r
    nblk = D_MODEL // LANES
    for cb in range(nblk):
        hr_scr[cb] = hr[:, cb * LANES:(cb + 1) * LANES]
    ang = pos_ref[...].astype(F32) * freq_ref[...]
    cos_scr[...] = jnp.cos(ang)
    sin_scr[...] = jnp.sin(ang) * sign_ref[...]
    scale = LOG2E / math.sqrt(DSW_HD)
    outs = ((q1, k1, v1), (q2, k2, v2), (q3, k3, v3))
    for g, (_, d) in enumerate(DSW_BRANCHES):
        n = tm // d
        q_out, k_out, v_out = outs[g]
        if d == 1:
            xq = (hr * gq_ref[...]).astype(BF16)
            xkv = (hr * gkv_ref[...]).astype(BF16)
            cos = cos_scr[...]
            sin = sin_scr[...]
        else:
            for rr in range(d):
                rows = slice(rr * n, (rr + 1) * n)
                for cb in range(nblk):
                    cols = slice(cb * LANES, (cb + 1) * LANES)
                    part = hr_scr[cb, pl.ds(rr, n, stride=d), :]
                    pq_scr[rows, cols] = (part * gq_ref[:, cols]).astype(BF16)
                    pkv_scr[rows, cols] = (part * gkv_ref[:, cols]).astype(BF16)
                pcos_scr[rows, :] = cos_scr[pl.ds(rr, n, stride=d), :]
                psin_scr[rows, :] = sin_scr[pl.ds(rr, n, stride=d), :]
            xq = pq_scr[...]
            xkv = pkv_scr[...]
            cos = pcos_scr[...]
            sin = psin_scr[...]
        q = _dot(xq, wq_ref[g])
        k = _dot(xkv, wk_ref[g])
        v = _dot(xkv, wv_ref[g])
        cos_q = cos * scale
        sin_q = sin * scale
        for j in range(DSW_HEADS // 2):
            sl = slice(j * LANES, (j + 1) * LANES)
            qj = _rope(q[:, sl], cos_q, sin_q).astype(BF16)
            kj = _rope(k[:, sl], cos, sin).astype(BF16)
            vj = v[:, sl].astype(BF16)
            for rr in range(d):
                q_out[j, rr] = qj[rr * n:(rr + 1) * n]
                k_out[j, rr] = kj[rr * n:(rr + 1) * n]
                v_out[j, rr] = vj[rr * n:(rr + 1) * n]


def _dsw_proj(h, pos, gq, gkv, wq, wk, wv, freq, sign, tm=512):
    B, S, _ = h.shape
    P = DSW_HEADS // 2
    out_specs, out_shape = [], []
    for _, d in DSW_BRANCHES:
        for _ in range(3):
            out_specs.append(pl.BlockSpec((None, P, d, tm // d, LANES), lambda b, i: (b, 0, 0, i, 0)))
            out_shape.append(jax.ShapeDtypeStruct((B, P, d, S // d, LANES), BF16))
    return pl.pallas_call(
        functools.partial(_dsw_proj_kernel, tm=tm),
        grid=(B, S // tm),
        in_specs=[
            pl.BlockSpec((None, tm, D_MODEL), lambda b, i: (b, i, 0)),
            pl.BlockSpec((None, tm, 1), lambda b, i: (b, i, 0)),
            _resident(gq.shape), _resident(gkv.shape), _resident(wq.shape), _resident(wk.shape),
            _resident(wv.shape), _resident(freq.shape), _resident(sign.shape),
        ],
        out_specs=out_specs,
        out_shape=out_shape,
        scratch_shapes=[
            pltpu.VMEM((D_MODEL // LANES, tm, LANES), F32),
            pltpu.VMEM((tm, LANES), F32), pltpu.VMEM((tm, LANES), F32),
            pltpu.VMEM((tm, D_MODEL), BF16), pltpu.VMEM((tm, D_MODEL), BF16),
            pltpu.VMEM((tm, LANES), F32), pltpu.VMEM((tm, LANES), F32),
        ],
        compiler_params=_cparams(("parallel", "parallel")),
        name="dsw_proj",
    )(h, pos, gq, gkv, wq, wk, wv, freq, sign)


def _dsw_attn_kernel(q1, k1, v1, q2, k2, v2, q3, k3, v3, o_ref,
                     o1_scr, l1_scr, o2_scr, l2_scr, o3_scr, l3_scr, s_scr, *, S):
    blk = DSW_BLOCK
    lane = lax.broadcasted_iota(jnp.int32, (blk, LANES), 1)
    mask_a = (lane < 8) | ((lane >= 16) & (lane < 72))
    first_half = lane < DSW_HD
    row1 = lax.broadcasted_iota(jnp.int32, (2 * blk, blk), 0)
    col1 = lax.broadcasted_iota(jnp.int32, (2 * blk, blk), 1)
    row1 = jnp.where(row1 >= blk, row1 - blk, row1)
    valid_first = col1 <= row1
    row2 = lax.broadcasted_iota(jnp.int32, (2 * blk, 2 * blk), 0)
    col2 = lax.broadcasted_iota(jnp.int32, (2 * blk, 2 * blk), 1)
    row2 = jnp.where(row2 >= blk, row2 - blk, row2)
    valid_next = ((col2 < blk) & (col2 >= row2)) | ((col2 >= blk) & (col2 - blk <= row2))

    def scores(q_ref, k_ref, off, nkeys):
        qf = q_ref[off:off + blk, :].astype(F32)
        q2 = jnp.concatenate([jnp.where(mask_a, qf, 0.0), jnp.where(mask_a, 0.0, qf)], axis=0).astype(BF16)
        return _dot_nt(q2, k_ref[off + blk - nkeys:off + blk, :])

    def finish(s, v_ref, off, valid, o_tok, l_tok, tok0, d):
        nkeys = valid.shape[1]
        s = jnp.where(valid, s, NEG_INF)
        m = jnp.max(s, axis=-1, keepdims=True)
        p = jnp.exp2(s - m)
        l = jnp.sum(p, axis=-1, keepdims=True)
        o2 = _dot(p.astype(BF16), v_ref[off + blk - nkeys:off + blk, :]) / l
        lse = m + jnp.log2(l)
        rows = pl.ds(tok0, blk, stride=d) if d > 1 else pl.ds(tok0, blk)
        o_tok[rows, :] = jnp.where(first_half, o2[:blk], o2[blk:])
        l_tok[rows, :] = jnp.where(first_half, lse[:blk], lse[blk:])

    for (_, d), (q_ref, k_ref, v_ref, o_tok, l_tok) in zip(
            DSW_BRANCHES, ((q1, k1, v1, o1_scr, l1_scr), (q2, k2, v2, o2_scr, l2_scr), (q3, k3, v3, o3_scr, l3_scr))):
        L = S // d
        blocks = [(rr * L + ii * blk, ii * blk * d + rr, valid_first if ii == 0 else valid_next)
                  for rr in range(d) for ii in range(L // blk)]
        off, _, valid = blocks[0]
        s_scr[:, :valid.shape[1]] = scores(q_ref, k_ref, off, valid.shape[1])
        for n, (off, tok0, valid) in enumerate(blocks):
            s = s_scr[:, :valid.shape[1]]
            if n + 1 < len(blocks):
                noff, _, nvalid = blocks[n + 1]
                s_scr[:, :nvalid.shape[1]] = scores(q_ref, k_ref, noff, nvalid.shape[1])
            finish(s, v_ref, off, valid, o_tok, l_tok, tok0, d)

    ch = 256
    for c0 in range(0, S, ch):
        sl = slice(c0, c0 + ch)
        la, lb, lc = l1_scr[sl, :], l2_scr[sl, :], l3_scr[sl, :]
        m = jnp.maximum(jnp.maximum(la, lb), lc)
        wa, wb, wc = jnp.exp2(la - m), jnp.exp2(lb - m), jnp.exp2(lc - m)
        num = wa * o1_scr[sl, :] + wb * o2_scr[sl, :] + wc * o3_scr[sl, :]
        o_ref[sl, :] = (num / (wa + wb + wc)).astype(BF16)


def _dsw_attn(qkv):
    B, P = qkv[0].shape[0], qkv[0].shape[1]
    S = qkv[0].shape[2] * qkv[0].shape[3]
    flat = [a.reshape(B, P, S, LANES) for a in qkv]
    spec = pl.BlockSpec((None, None, S, LANES), lambda b, j: (b, j, 0, 0))
    return pl.pallas_call(
        functools.partial(_dsw_attn_kernel, S=S),
        grid=(B, P),
        in_specs=[spec] * 9,
        out_specs=pl.BlockSpec((None, S, LANES), lambda b, j: (b, 0, j)),
        out_shape=jax.ShapeDtypeStruct((B, S, P * LANES), BF16),
        scratch_shapes=[pltpu.VMEM((S, LANES), F32)] * 6 + [pltpu.VMEM((2 * DSW_BLOCK, 2 * DSW_BLOCK), F32)],
        compiler_params=_cparams(("parallel", "parallel")),
        name="dsw_attn",
    )(*flat)


META_E1, META_E2, META_R1, META_R2, META_G1, META_G2 = range(6)


def _dsw_router_kernel(h_ref, o_ref, wo_ref, g_ref, wr_ref, h3_ref, meta_ref, meta_t_ref, cnt_ref, base_scr):
    @pl.when(pl.program_id(0) == 0)
    def _():
        base_scr[...] = jnp.zeros_like(base_scr)

    h3 = h_ref[...] + _dot(o_ref[...], wo_ref[...])
    h3_ref[...] = h3
    hn = _rms(h3, g_ref[...])
    hi = hn.astype(BF16)
    lo = (hn - hi.astype(F32)).astype(BF16)
    part = _dot(hi, wr_ref[...])
    logits = part[:, :LANES] + part[:, LANES:] + _dot(lo, wr_ref[:, :LANES])
    tm = logits.shape[0]
    lane = lax.broadcasted_iota(jnp.int32, logits.shape, 1).astype(F32)
    logits = jnp.where(lane < N_EXPERTS, logits, -jnp.inf)
    v1 = jnp.max(logits, axis=-1, keepdims=True)
    i1 = jnp.min(jnp.where(logits == v1, lane, float(LANES)), axis=-1, keepdims=True)
    rest = jnp.where(lane == i1, -jnp.inf, logits)
    v2 = jnp.max(rest, axis=-1, keepdims=True)
    i2 = jnp.min(jnp.where(rest == v2, lane, float(LANES)), axis=-1, keepdims=True)
    e2 = jnp.exp(v2 - v1)
    den = 1.0 + e2
    onehot = jnp.where((lane == i1) | (lane == i2), 1.0, 0.0)
    row = lax.broadcasted_iota(jnp.int32, (tm, tm), 0)
    col = lax.broadcasted_iota(jnp.int32, (tm, tm), 1)
    earlier = jnp.where(col < row, 1.0, 0.0).astype(BF16)
    before = _dot(earlier, onehot.astype(BF16)) + base_scr[0:1, :]
    r1 = jnp.sum(jnp.where(lane == i1, before, 0.0), axis=-1, keepdims=True)
    r2 = jnp.sum(jnp.where(lane == i2, before, 0.0), axis=-1, keepdims=True)
    total = base_scr[0:1, :] + jnp.sum(onehot, axis=0, keepdims=True)
    base_scr[...] = jnp.broadcast_to(total, base_scr.shape)
    cnt_ref[...] = jnp.broadcast_to(total, cnt_ref.shape)
    meta = jnp.zeros_like(logits)
    for k, val in ((META_E1, i1), (META_E2, i2), (META_R1, r1), (META_R2, r2),
                   (META_G1, 1.0 / den), (META_G2, e2 / den)):
        meta = jnp.where(lane == float(k), val, meta)
    meta_ref[...] = meta
    meta_t_ref[...] = meta.T[:8, :]


def _dsw_router(h, o, wo, g, wr, tm=512):
    T = h.shape[0]
    return pl.pallas_call(
        _dsw_router_kernel,
        grid=(T // tm,),
        in_specs=[
            pl.BlockSpec((tm, D_MODEL), lambda i: (i, 0)),
            pl.BlockSpec((tm, D_MODEL), lambda i: (i, 0)),
            _resident(wo.shape), _resident(g.shape), _resident(wr.shape),
        ],
        out_specs=[
            pl.BlockSpec((tm, D_MODEL), lambda i: (i, 0)),
            pl.BlockSpec((tm, LANES), lambda i: (i, 0)),
            pl.BlockSpec((8, tm), lambda i: (0, i)),
            pl.BlockSpec((8, LANES), lambda i: (0, 0)),
        ],
        out_shape=[
            jax.ShapeDtypeStruct((T, D_MODEL), F32),
            jax.ShapeDtypeStruct((T, LANES), F32),
            jax.ShapeDtypeStruct((8, T), F32),
            jax.ShapeDtypeStruct((8, LANES), F32),
        ],
        scratch_shapes=[pltpu.VMEM((8, LANES), F32)],
        compiler_params=_cparams(("arbitrary",)),
        name="dsw_router",
    )(h, o, wo, g, wr)


MOE_ROWS = 512
MOE_SUB = 128


def _row_copy(src, src_row, dst, dst_row, sem):
    return pltpu.make_async_copy(src.at[pl.ds(src_row, 1)], dst.at[pl.ds(dst_row, 1)], sem)


def _moe_dispatch_kernel(pad_start_ref, pad_len_ref, nu_ref, d1_ref, d2_ref, h3_ref, g_ref, xs_hbm, zero_scr, hn_scr,
                         sem, zsem, *, tm):
    @pl.when(pl.program_id(0) == 0)
    def _():
        zero_scr[...] = jnp.zeros_like(zero_scr)
        R = zero_scr.shape[0]

        def zero_tile(j, c):
            cp = pltpu.make_async_copy(zero_scr, xs_hbm.at[pl.ds(pl.multiple_of(j * R, R), R)], zsem)
            cp.start()
            cp.wait()
            return c

        lax.fori_loop(nu_ref[0], xs_hbm.shape[0] // R, zero_tile, 0)
        def pad_copies(act):
            for e in range(N_EXPERTS):
                n, start = pad_len_ref[e], pad_start_ref[e]
                head = n & 7

                def row(r, c, start=start):
                    act(_row_copy(zero_scr, 0, xs_hbm, start + r, zsem))
                    return c

                lax.fori_loop(0, head, row, 0)
                p = R // 2
                while p >= 8:
                    @pl.when((n & p) != 0)
                    def _(p=p, n=n, start=start, head=head):
                        at = pl.multiple_of(start + head + ((n - head) & (-2 * p)), 8)
                        act(pltpu.make_async_copy(zero_scr.at[pl.ds(0, p)], xs_hbm.at[pl.ds(at, p)], zsem))
                    p //= 2

        pad_copies(lambda cp: cp.start())
        pad_copies(lambda cp: cp.wait())

    hn_scr[...] = _rms(h3_ref[...], g_ref[...])

    def body(r, c):
        _row_copy(hn_scr, r, xs_hbm, d1_ref[0, r], sem).start(priority=0)
        _row_copy(hn_scr, r, xs_hbm, d2_ref[0, r], sem).start(priority=1)
        return c

    lax.fori_loop(0, tm, body, 0, unroll=32)
    for _ in range(2):
        pltpu.make_async_copy(hn_scr, xs_hbm.at[pl.ds(0, tm)], sem).wait()


def _moe_dispatch(pad_start, pad_len, n_used, d1, d2, h3, g, n_rows, tm=512):
    T = h3.shape[0]
    smem = lambda: pl.BlockSpec((None, 1, tm), lambda i, ps, pn, nu: (i, 0, 0), memory_space=pltpu.SMEM)
    return pl.pallas_call(
        functools.partial(_moe_dispatch_kernel, tm=tm),
        grid_spec=pltpu.PrefetchScalarGridSpec(
            num_scalar_prefetch=3,
            grid=(T // tm,),
            in_specs=[smem(), smem(), pl.BlockSpec((tm, D_MODEL), lambda i, ps, pn, nu: (i, 0)),
                      pl.BlockSpec((1, D_MODEL), lambda i, ps, pn, nu: (0, 0))],
            out_specs=pl.BlockSpec(memory_space=pl.ANY),
            scratch_shapes=[pltpu.VMEM((MOE_ROWS, D_MODEL), F32), pltpu.VMEM((tm, D_MODEL), F32),
                            pltpu.SemaphoreType.DMA, pltpu.SemaphoreType.DMA],
        ),
        out_shape=jax.ShapeDtypeStruct((n_rows, D_MODEL), F32),
        compiler_params=_cparams(("arbitrary",)),
        name="moe_dispatch",
    )(pad_start, pad_len, n_used, d1.reshape(T // tm, 1, tm), d2.reshape(T // tm, 1, tm), h3, g)


def _moe_group_kernel(te_ref, nu_ref, nr_ref, xs_ref, wg_ref, wu_ref, wd_ref, ys_ref, xb_scr, acc_scr):
    j = pl.program_id(0)
    f = pl.program_id(1)
    used = j < nu_ref[0]

    @pl.when(used & (f == 0))
    def _():
        xb_scr[...] = xs_ref[...].astype(BF16)
        acc_scr[...] = jnp.zeros_like(acc_scr)

    need = (nr_ref[j] + (MOE_SUB - 1)) // MOE_SUB
    for k in range(1, MOE_ROWS // MOE_SUB + 1):
        @pl.when(used & (need == k))
        def _(m=k * MOE_SUB):
            acc_scr[:m, :] += _swiglu_acc(xb_scr[:m, :], wg_ref, wu_ref, wd_ref, wg_ref.shape[-1] // 2)

    @pl.when(f == pl.num_programs(1) - 1)
    def _():
        ys_ref[...] = jnp.where(used, acc_scr[...], 0.0)


def _moe_group(tile_expert, n_used, tile_rows, xs, wg, wu, wd, tf=1792):
    n_rows = xs.shape[0]
    R = MOE_ROWS
    E, _, Fe = wg.shape
    nf = Fe // tf

    def w_col(j, f, te, nu):
        return jnp.where(j < nu[0], f, nf - 1)

    return pl.pallas_call(
        _moe_group_kernel,
        grid_spec=pltpu.PrefetchScalarGridSpec(
            num_scalar_prefetch=3,
            grid=(n_rows // R, nf),
            in_specs=[
                pl.BlockSpec((R, D_MODEL),
                             lambda j, f, te, nu, nr: (jnp.maximum(jnp.minimum(j, nu[0] - 1), 0), 0)),
                pl.BlockSpec((None, D_MODEL, tf), lambda j, f, te, nu, nr: (te[j], 0, w_col(j, f, te, nu))),
                pl.BlockSpec((None, D_MODEL, tf), lambda j, f, te, nu, nr: (te[j], 0, w_col(j, f, te, nu))),
                pl.BlockSpec((None, tf, D_MODEL), lambda j, f, te, nu, nr: (te[j], w_col(j, f, te, nu), 0)),
            ],
            out_specs=pl.BlockSpec((R, D_MODEL), lambda j, f, te, nu, nr: (j, 0)),
            scratch_shapes=[pltpu.VMEM((R, D_MODEL), BF16), pltpu.VMEM((R, D_MODEL), F32)],
        ),
        out_shape=jax.ShapeDtypeStruct((n_rows, D_MODEL), F32),
        compiler_params=_cparams(("arbitrary", "arbitrary")),
        name="moe_group",
    )(tile_expert, n_used, tile_rows, xs, wg, wu, wd)


def _moe_combine_kernel(d1_ref, d2_ref, n1_ref, n2_ref, meta_ref, h3_ref, gf_ref, ys_hbm, out_ref,
                        y_scr, sems, *, tm):
    i = pl.program_id(0)
    slot = i % 2

    def gather(da_ref, db_ref, s):
        def body(r, c):
            _row_copy(ys_hbm, da_ref[0, r], y_scr.at[s, 0], r, sems.at[s]).start(priority=0)
            _row_copy(ys_hbm, db_ref[0, r], y_scr.at[s, 1], r, sems.at[s]).start(priority=1)
            return c

        lax.fori_loop(0, tm, body, 0, unroll=32)

    @pl.when(i == 0)
    def _():
        gather(d1_ref, d2_ref, 0)

    @pl.when(i + 1 < pl.num_programs(0))
    def _():
        gather(n1_ref, n2_ref, 1 - slot)

    for k in range(2):
        pltpu.make_async_copy(ys_hbm.at[pl.ds(0, tm)], y_scr.at[slot, k], sems.at[slot]).wait()
    meta = meta_ref[...]
    lane = lax.broadcasted_iota(jnp.int32, meta.shape, 1)
    g1 = jnp.sum(jnp.where(lane == META_G1, meta, 0.0), axis=-1, keepdims=True)
    g2 = jnp.sum(jnp.where(lane == META_G2, meta, 0.0), axis=-1, keepdims=True)
    out_ref[...] = _rms(h3_ref[...] + g1 * y_scr[slot, 0] + g2 * y_scr[slot, 1], gf_ref[...])


def _moe_combine(d1, d2, meta, h3, gf, ys, tm=512):
    T = h3.shape[0]
    n = T // tm
    smem = lambda nxt: pl.BlockSpec((None, 1, tm), lambda i: (jnp.minimum(i + nxt, n - 1), 0, 0),
                                    memory_space=pltpu.SMEM)
    d1 = d1.reshape(n, 1, tm)
    d2 = d2.reshape(n, 1, tm)
    return pl.pallas_call(
        functools.partial(_moe_combine_kernel, tm=tm),
        grid=(n,),
        in_specs=[
            smem(0), smem(0), smem(1), smem(1),
            pl.BlockSpec((tm, LANES), lambda i: (i, 0)),
            pl.BlockSpec((tm, D_MODEL), lambda i: (i, 0)),
            pl.BlockSpec((1, D_MODEL), lambda i: (0, 0)),
            pl.BlockSpec(memory_space=pl.ANY),
        ],
        out_specs=pl.BlockSpec((tm, D_MODEL), lambda i: (i, 0)),
        out_shape=jax.ShapeDtypeStruct((T, D_MODEL), F32),
        scratch_shapes=[pltpu.VMEM((2, 2, tm, D_MODEL), F32), pltpu.SemaphoreType.DMA((2,))],
        compiler_params=_cparams(("arbitrary",)),
        name="moe_combine",
    )(d1, d2, d1, d2, meta, h3, gf, ys)


def _moe_plan(meta_t, counts, n_tiles):
    R = MOE_ROWS
    cnt = counts[0, :N_EXPERTS].astype(jnp.int32)
    tiles = (cnt + R - 1) // R
    tile_end = jnp.cumsum(tiles)
    row_start = (tile_end - tiles) * R
    fields = meta_t.astype(jnp.int32)
    d1 = row_start[fields[META_E1]] + fields[META_R1]
    d2 = row_start[fields[META_E2]] + fields[META_R2]
    n_used = tile_end[-1:]
    j = jnp.minimum(jnp.arange(n_tiles, dtype=jnp.int32), n_used - 1)
    tile_expert = jnp.sum(j[:, None] >= tile_end[None, :], axis=1).astype(jnp.int32)
    pad_start = (row_start + cnt).astype(jnp.int32)
    pad_len = (tiles * R - cnt).astype(jnp.int32)
    tile_rows = jnp.clip(pad_start[tile_expert] - j * R, 0, R).astype(jnp.int32)
    return d1, d2, tile_expert, n_used.astype(jnp.int32), pad_start, pad_len, tile_rows


def _lanes(parts):
    ref = next(p for p in parts if not isinstance(p, int))
    return jnp.concatenate([jnp.zeros(ref.shape[:-1] + (p,), ref.dtype) if isinstance(p, int) else p
                            for p in parts], axis=-1)


def _mla_head_lanes(nope, rope):
    half = MLA_ROPE // 2
    cut = 64 - half
    r1, r2 = (half, half) if isinstance(rope, int) else (rope[..., :half], rope[..., half:])
    n1, n2 = (cut, MLA_NOPE - cut) if isinstance(nope, int) else (nope[..., :cut], nope[..., cut:])
    return _lanes([r1, n1, r2, n2, LANES - MLA_QK])


def _dsw_pair_lanes(w):
    half = DSW_ROT // 2
    w = w.reshape(w.shape[0], DSW_HEADS // 2, 2, DSW_HD)
    a, b = w[:, :, 0, :], w[:, :, 1, :]
    blk = _lanes([a[..., :half], b[..., :half], a[..., DSW_ROT:], a[..., half:DSW_ROT], b[..., half:DSW_ROT],
                  b[..., DSW_ROT:]])
    return blk.reshape(w.shape[0], DSW_WIDTH)


def _rope_lane_tables(half, copies):
    inv_freq = jnp.power(jnp.float32(ROPE_THETA), -jnp.arange(half, dtype=F32) * (2.0 / (2 * half)))
    on = copies * half
    group = [inv_freq] * copies + [64 - on]
    sign = np.zeros(LANES, np.float32)
    sign[:on], sign[64:64 + on] = -1.0, 1.0
    return _lanes(group + group).reshape(1, LANES), jnp.asarray(sign).reshape(1, LANES)


def kernel(x, positions, norm_attn, norm_ffn, mla_w_down, mla_q_norm, mla_w_uq, mla_kv_norm, mla_w_ukv,
           mla_w_o, dsw_kv_norm, dsw_w_kv, dsw_w_q, dsw_w_o, ffn_w_gate, ffn_w_up, ffn_w_down, moe_router,
           moe_w_gate, moe_w_up, moe_w_down, final_norm):
    B, S, D = x.shape
    T = B * S
    pos = positions.reshape(B, S, 1)

    H = MLA_HEADS
    n_lat = MLA_Q_LORA + MLA_KV_LORA
    w_down = mla_w_down.reshape(D, n_lat + MLA_ROPE)
    wdn = _lanes([w_down[:, :n_lat], _mla_head_lanes(MLA_NOPE, w_down[:, n_lat:])]).astype(BF16)
    w_uq = mla_w_uq.reshape(MLA_Q_LORA, H, MLA_QK)
    wuq = _mla_head_lanes(w_uq[..., :MLA_NOPE], w_uq[..., MLA_NOPE:]).reshape(MLA_Q_LORA, H * LANES).astype(BF16)
    w_ukv = mla_w_ukv.reshape(MLA_KV_LORA, H, MLA_NOPE + MLA_V)
    wk = _mla_head_lanes(w_ukv[..., :MLA_NOPE], MLA_ROPE).reshape(MLA_KV_LORA, H * LANES).astype(BF16)
    wv = w_ukv[..., MLA_NOPE:].reshape(MLA_KV_LORA, H * MLA_V).astype(BF16)
    mla_freq, mla_sign = _rope_lane_tables(MLA_ROPE // 2, 1)

    q, k, v = _mla_proj(x, pos, norm_attn[0:1], wdn, mla_q_norm[0:1], mla_kv_norm[0:1], wuq, wk, wv,
                        mla_freq, mla_sign, tm=MLA_TILE)
    o, (moe_wg, moe_wu, moe_wd) = _mla_attn(q, k, v, (moe_w_gate.reshape(N_EXPERTS, D, EXPERT_DIM),
                                                      moe_w_up.reshape(N_EXPERTS, D, EXPERT_DIM),
                                                      moe_w_down.reshape(N_EXPERTS, EXPERT_DIM, D)))
    h = _attn_ffn(x.reshape(T, D), o.reshape(T, D), mla_w_o[0].astype(BF16), norm_ffn[0:1],
                  ffn_w_gate[0].astype(BF16), ffn_w_up[0].astype(BF16), ffn_w_down[0].astype(BF16))

    w_q = dsw_w_q.reshape(D, N_BR * DSW_WIDTH)
    branch = lambda w, g: w[:, g * DSW_WIDTH:(g + 1) * DSW_WIDTH]
    wq_b = jnp.stack([_dsw_pair_lanes(branch(w_q, g)) for g in range(N_BR)]).astype(BF16)
    wk_b = jnp.stack([_dsw_pair_lanes(branch(dsw_w_kv, g)) for g in range(N_BR)]).astype(BF16)
    wv_b = jnp.stack([branch(dsw_w_kv, N_BR + g) for g in range(N_BR)]).astype(BF16)
    dsw_freq, dsw_sign = _rope_lane_tables(DSW_ROT // 2, 2)

    qkv = _dsw_proj(h.reshape(B, S, D), pos, norm_attn[1:2], dsw_kv_norm.reshape(1, D), wq_b, wk_b, wv_b,
                    dsw_freq, dsw_sign)
    o = _dsw_attn(qkv)
    wr = jnp.pad(moe_router[0], ((0, 0), (0, LANES - N_EXPERTS)))
    wr_hi = wr.astype(BF16)
    wr = jnp.concatenate([wr_hi, (wr - wr_hi.astype(F32)).astype(BF16)], axis=1)
    h3, meta, meta_t, counts = _dsw_router(h, o.reshape(T, D), dsw_w_o[0].astype(BF16), norm_ffn[1:2], wr)

    n_tiles = (2 * T) // MOE_ROWS + N_EXPERTS
    d1, d2, tile_expert, n_used, pad_start, pad_len, tile_rows = _moe_plan(meta_t, counts, n_tiles)
    xs = _moe_dispatch(pad_start, pad_len, n_used, d1, d2, h3, norm_ffn[1:2], n_tiles * MOE_ROWS)
    ys = _moe_group(tile_expert, n_used, tile_rows, xs, moe_wg, moe_wu, moe_wd)
    out = _moe_combine(d1, d2, meta, h3, final_norm.reshape(1, D), ys)
    return out.reshape(B, S, D)
```
